```python
import math
import jax, jax.numpy as jnp
from jax import lax
import numpy as np

D_MODEL = 2048
BATCH = 4
SEQ = 2048
DEPTH = 2
DEC_BATCH = 32
DEC_SEQ = 8
PAST_LEN = 8192
PAGE_SIZE = 128

BRANCH_WIDTH = 512
N_BRANCH = 4
GLA_HEADS = 4
GLA_DK = 64
GLA_DV = 128
GLA_WIDTH = GLA_HEADS * GLA_DV
GLA_LOWRANK = 16
GLA_TAU = 16.0
GLA_CHUNK = 16
S5_WIDTH = BRANCH_WIDTH
S5_GROUP = 16
S5_GROUPS = S5_WIDTH // S5_GROUP
S5_N = 64
S5_DT_MIN = 1e-3
S5_DT_MAX = 1e-1
LRU_WIDTH = BRANCH_WIDTH
LRU_HEADS = 4
LRU_BLOCK = LRU_WIDTH // LRU_HEADS
LRU_CONV = 4
LRU_C = 8.0
DA_HEADS = 4
DA_DK = 64
DA_DV = 2 * DA_DK
DA_WIDTH = DA_HEADS * DA_DV
ROPE_DIM = DA_DK // 4
ROPE_THETA = 500000.0
Q_BLOCK = 128
D_FF = 5632
FFN_CONV = 3
DEEPNORM_ALPHA = (2.0 * DEPTH) ** 0.25
DEEPNORM_BETA = (8.0 * DEPTH) ** -0.25
EPS = 1e-5

IN_SIZES = (GLA_HEADS * GLA_DK, GLA_HEADS * GLA_DK, GLA_WIDTH, GLA_LOWRANK, GLA_WIDTH,
            S5_WIDTH, LRU_WIDTH, LRU_WIDTH,
            DA_HEADS * 2 * DA_DK, DA_HEADS * 2 * DA_DK, DA_WIDTH,
            N_BRANCH * D_MODEL)
N_IN = sum(IN_SIZES)

kernel_name = 'hybrid_gla_s5_rglru_diffattn_decode_step'


def _layer_norm(x, g, b):
    xf = x.astype(jnp.float32)
    mu = jnp.mean(xf, -1, keepdims=True)
    var = jnp.mean(jnp.square(xf - mu), -1, keepdims=True)
    return ((xf - mu) * lax.rsqrt(var + EPS) * g + b).astype(x.dtype)


def _rms_norm(x, g):
    xf = x.astype(jnp.float32)
    return (xf * lax.rsqrt(jnp.mean(jnp.square(xf), -1, keepdims=True) + EPS) * g).astype(x.dtype)


def _causal_conv(x, buf, w, b):
    K = w.shape[0]
    L = x.shape[1]
    xx = jnp.concatenate([buf.astype(x.dtype), x], axis=1)
    y = b + w[0] * xx[:, 0:L]
    for j in range(1, K):
        y = y + w[j] * xx[:, j:j + L]
    return y, xx[:, L:]


def _rotary(t, pos):
    half = ROPE_DIM // 2
    inv = ROPE_THETA ** (-jnp.arange(half, dtype=jnp.float32) * 2.0 / ROPE_DIM)
    ang = pos.astype(jnp.float32)[:, None] * inv
    cos = jnp.cos(ang)[:, None, None, :]
    sin = jnp.sin(ang)[:, None, None, :]
    tf = t.astype(jnp.float32)
    x1, x2 = tf[..., :half], tf[..., half:ROPE_DIM]
    out = jnp.concatenate([x1 * cos - x2 * sin, x2 * cos + x1 * sin, tf[..., ROPE_DIM:]], -1)
    return out.astype(t.dtype)


def _affine_combine(e1, e2):
    a1, b1 = e1
    a2, b2 = e2
    return (a1 * a2, a2 * b1 + b2)


def _complex_affine_combine(e1, e2):
    a1r, a1i, b1r, b1i = e1
    a2r, a2i, b2r, b2i = e2
    return (a2r * a1r - a2i * a1i, a2r * a1i + a2i * a1r,
            a2r * b1r - a2i * b1i + b2r, a2r * b1i + a2i * b1r + b2i)


def _gla(q, k, v, log_a, s0):
    Bn, L, H, DK = q.shape
    C = math.gcd(GLA_CHUNK, L)
    n = L // C

    def chunks(t):
        return jnp.swapaxes(t.astype(jnp.float32).reshape(Bn, n, C, *t.shape[2:]), 0, 1)

    qc = chunks(q) * DK ** -0.5
    kc, vc, ac = chunks(k), chunks(v), chunks(log_a)
    cum = jnp.cumsum(ac, axis=2)
    causal = jnp.tril(jnp.ones((C, C), bool))[:, :, None, None]
    rel = cum[:, :, :, None] - cum[:, :, None, :]
    decay = jnp.exp(jnp.where(causal, rel, -jnp.inf))
    att = jnp.einsum('nbthd,nbshd,nbtshd->nbtsh', qc, kc, decay)
    o_intra = jnp.einsum('nbtsh,nbshv->nbthv', att, vc)

    def step(S, inp):
        q_, k_, v_, c_ = inp
        o_int = jnp.einsum('bthd,bhdv->bthv', q_ * jnp.exp(c_), S)
        last = c_[:, -1]
        k_dec = k_ * jnp.exp(last[:, None] - c_)
        S_new = jnp.exp(last)[..., None] * S + jnp.einsum('bshd,bshv->bhdv', k_dec, v_)
        return S_new, o_int

    S_fin, o_inter = lax.scan(step, s0.astype(jnp.float32), (qc, kc, vc, cum))
    o = jnp.swapaxes(o_intra + o_inter, 0, 1).reshape(Bn, L, H, -1)
    return o, S_fin


def _s5(u, h0_re, h0_im, a_re, a_im, log_dt, b_re, b_im, c_re, c_im, d, w_glu, b_glu):
    Bn, L, _ = u.shape
    ug = u.astype(jnp.float32).reshape(Bn, L, S5_GROUPS, S5_GROUP)
    dt = jnp.exp(log_dt.astype(jnp.float32))[:, None]
    lr, li = a_re.astype(jnp.float32), a_im.astype(jnp.float32)
    mag = jnp.exp(lr * dt)
    ab_re, ab_im = mag * jnp.cos(li * dt), mag * jnp.sin(li * dt)
    den = lr * lr + li * li
    f_re = ((ab_re - 1.0) * lr + ab_im * li) / den
    f_im = (ab_im * lr - (ab_re - 1.0) * li) / den
    bb_re = f_re[..., None] * b_re - f_im[..., None] * b_im
    bb_im = f_re[..., None] * b_im + f_im[..., None] * b_re
    bu_re = jnp.einsum('blgp,gnp->blgn', ug, bb_re)
    bu_im = jnp.einsum('blgp,gnp->blgn', ug, bb_im)
    h0r, h0i = h0_re.astype(jnp.float32), h0_im.astype(jnp.float32)
    bu_re = bu_re.at[:, 0].add(ab_re * h0r - ab_im * h0i)
    bu_im = bu_im.at[:, 0].add(ab_re * h0i + ab_im * h0r)
    ar = jnp.broadcast_to(ab_re, bu_re.shape)
    ai = jnp.broadcast_to(ab_im, bu_im.shape)
    _, _, h_re, h_im = lax.associative_scan(_complex_affine_combine, (ar, ai, bu_re, bu_im), axis=1)
    y = (jnp.einsum('blgn,gpn->blgp', h_re, c_re) - jnp.einsum('blgn,gpn->blgp', h_im, c_im)
         + d.reshape(S5_GROUPS, S5_GROUP) * ug)
    y = jax.nn.gelu(y.reshape(Bn, L, S5_WIDTH))
    y = y * jax.nn.sigmoid(y @ w_glu + b_glu)
    return y, h_re[:, -1], h_im[:, -1]


def _rglru(xb, gb, conv_buf, h0, conv_w, conv_b, w_a, b_a, w_x, b_x, lam):
    xc, new_buf = _causal_conv(xb, conv_buf, conv_w, conv_b)
    Bn, L, _ = xc.shape
    xf = xc.astype(jnp.float32)
    xh = xf.reshape(Bn, L, LRU_HEADS, LRU_BLOCK)
    r = jax.nn.sigmoid(jnp.einsum('blhi,hij->blhj', xh, w_a).reshape(Bn, L, LRU_WIDTH) + b_a)
    i = jax.nn.sigmoid(jnp.einsum('blhi,hij->blhj', xh, w_x).reshape(Bn, L, LRU_WIDTH) + b_x)
    log_a = -LRU_C * r * jax.nn.softplus(-lam.astype(jnp.float32))
    a = jnp.exp(log_a)
    b = jnp.sqrt(-jnp.expm1(2.0 * log_a)) * (i * xf)
    b = b.at[:, 0].add(a[:, 0] * h0.astype(jnp.float32))
    _, h = lax.associative_scan(_affine_combine, (a, b), axis=1)
    y = h * jax.nn.gelu(gb.astype(jnp.float32))
    return y, h[:, -1], new_buf


def _diff_attention(q, k_all, v_all, q_pos, k_pos, lam):
    Bn, Lq = q.shape[:2]
    blk = math.gcd(Q_BLOCK, Lq)
    nb = Lq // blk
    qb = jnp.swapaxes(q.reshape(Bn, nb, blk, DA_HEADS, 2, DA_DK), 0, 1)
    pb = q_pos.reshape(nb, blk)
    kf = k_all.astype(jnp.float32)
    vf = v_all.astype(jnp.float32)
    scale = DA_DK ** -0.5

    def one(args):
        qq, pp = args
        s = jnp.einsum('bqhcd,bkhcd->bhcqk', qq.astype(jnp.float32), kf) * scale
        s = jnp.where(k_pos[None, :] <= pp[:, None], s, -jnp.inf)
        p = jax.nn.softmax(s, axis=-1)
        w = p[:, :, 0] - lam * p[:, :, 1]
        return jnp.einsum('bhqk,bkhv->bqhv', w, vf)

    o = lax.map(one, (qb, pb))
    return jnp.swapaxes(o, 0, 1).reshape(Bn, Lq, DA_HEADS, DA_DV)


def _layer(x, c, pos0, past_k, past_v, s_gla, s5_re, s5_im, s_lru, s_lconv, s_fconv, p, layer_idx):
    Bn, L, _ = x.shape
    dt = x.dtype
    mod = jax.nn.silu(c) @ p['w_ada'] + p['b_ada']
    sh1, sc1, g1, sh2, sc2, g2 = jnp.split(mod[:, None, :], 6, axis=-1)
    u = x * (1.0 + sc1) + sh1
    z = u @ p['w_in']
    splits = [int(v) for v in np.cumsum(IN_SIZES)[:-1]]
    (g_q, g_k, g_v, g_lr, g_og, s_u, l_x, l_g, d_q, d_k, d_v, br_g) = jnp.split(z, splits, axis=-1)

    log_a = jax.nn.log_sigmoid((g_lr @ p['gla_wa2'] + p['gla_ba']).astype(jnp.float32)) / GLA_TAU
    hd = (Bn, L, GLA_HEADS)
    o_gla, s_gla_new = _gla(g_q.reshape(*hd, GLA_DK), g_k.reshape(*hd, GLA_DK),
                            g_v.reshape(*hd, GLA_DV), log_a.reshape(*hd, GLA_DK), s_gla)
    o_gla = _rms_norm(o_gla, p['gla_norm']) * jax.nn.silu(g_og.reshape(*hd, GLA_DV).astype(jnp.float32))
    o_gla = o_gla.reshape(Bn, L, GLA_WIDTH).astype(dt)

    o_s5, s5_re_new, s5_im_new = _s5(s_u, s5_re, s5_im, p['s5_a_re'], p['s5_a_im'], p['s5_log_dt'],
                                     p['s5_b_re'], p['s5_b_im'], p['s5_c_re'], p['s5_c_im'],
                                     p['s5_d'], p['s5_w_glu'], p['s5_b_glu'])

    o_lru, s_lru_new, s_lconv_new = _rglru(l_x, l_g, s_lconv, s_lru, p['lru_conv_w'], p['lru_conv_b'],
                                           p['lru_w_a'], p['lru_b_a'], p['lru_w_x'], p['lru_b_x'],
                                           p['lru_lambda'])

    pos = pos0 + jnp.arange(L)
    q = _rotary(d_q.reshape(Bn, L, DA_HEADS, 2, DA_DK), pos)
    k = _rotary(d_k.reshape(Bn, L, DA_HEADS, 2, DA_DK), pos)
    v = d_v.reshape(Bn, L, DA_HEADS, DA_DV)
    if past_k is None:
        k_all, v_all = k, v
    else:
        k_all = jnp.concatenate([past_k.astype(dt), k], axis=1)
        v_all = jnp.concatenate([past_v.astype(dt), v], axis=1)
    k_pos = jnp.arange(k_all.shape[1])
    lam_init = 0.8 - 0.6 * math.exp(-0.3 * layer_idx)
    lam = (jnp.exp(jnp.sum(p['da_lq1'] * p['da_lk1']).astype(jnp.float32))
           - jnp.exp(jnp.sum(p['da_lq2'] * p['da_lk2']).astype(jnp.float32)) + lam_init)
    o_da = _diff_attention(q, k_all, v_all, pos, k_pos, lam)
    o_da = (_rms_norm(o_da, p['da_norm']) * (1.0 - lam_init)).reshape(Bn, L, DA_WIDTH).astype(dt)

    branches = jnp.stack([o_gla, o_s5.astype(dt), o_lru.astype(dt), o_da], axis=2)
    proj = jnp.einsum('blnw,nwd->blnd', branches, p['w_branch'])
    gates = jax.nn.sigmoid(br_g.reshape(Bn, L, N_BRANCH, D_MODEL))
    mix = jnp.sum(gates * proj, axis=2) @ p['w_out']
    x = _layer_norm(DEEPNORM_ALPHA * x + g1 * mix, p['ln1_g'], p['ln1_b'])

    u2 = x * (1.0 + sc2) + sh2
    hup = u2 @ p['ffn_w_up']
    hup, s_fconv_new = _causal_conv(hup, s_fconv, p['ffn_conv_w'], p['ffn_conv_b'])
    a_, v_ = jnp.split(hup, 2, axis=-1)
    f = (jax.nn.gelu(a_) * v_) @ p['ffn_w_down']
    x = _layer_norm(DEEPNORM_ALPHA * x + g2 * f, p['ln2_g'], p['ln2_b'])

    new = (k, v, s_gla_new.astype(dt), s5_re_new.astype(dt), s5_im_new.astype(dt),
           s_lru_new.astype(dt), s_lconv_new.astype(dt), s_fconv_new.astype(dt))
    return x, new


def _run_group(x, c, pos0, paged, states, params):
    st_gla, st_s5re, st_s5im, st_lru, st_lconv, st_fconv = states
    collected = [[] for _ in range(8)]
    for l in range(DEPTH):
        p = {name: arr[l] for name, arr in params.items()}
        if paged is None:
            past_k = past_v = None
        else:
            page_table, cache_k, cache_v = paged
            nb = page_table.shape[0]
            past_k = cache_k[l][page_table].reshape(nb, -1, DA_HEADS, 2, DA_DK)
            past_v = cache_v[l][page_table].reshape(nb, -1, DA_HEADS, DA_DV)
        x, new = _layer(x, c, pos0, past_k, past_v, st_gla[l], st_s5re[l], st_s5im[l],
                        st_lru[l], st_lconv[l], st_fconv[l], p, l)
        for lst, s in zip(collected, new):
            lst.append(s)
    return x, [jnp.stack(s) for s in collected]


def setup_inputs(seed: int = 0) -> dict:
    key = jax.random.key(seed)
    ks = iter(jax.random.split(key, 80))

    def nrm(shape, scale=1.0):
        return jax.random.normal(next(ks), shape, jnp.float32) * scale

    D = D_MODEL
    n_pages = PAST_LEN // PAGE_SIZE
    n_used = DEC_BATCH * n_pages
    n_pool = n_used + max(1, n_used // 4)
    x_prompt = nrm((BATCH, SEQ, D))
    x_sample = nrm((DEC_BATCH, DEC_SEQ, D))
    cache_k = nrm((DEPTH, n_pool, PAGE_SIZE, DA_HEADS, 2, DA_DK))
    cache_v = nrm((DEPTH, n_pool, PAGE_SIZE, DA_HEADS, DA_DV))
    state_gla = nrm((DEPTH, DEC_BATCH, GLA_HEADS, GLA_DK, GLA_DV), 0.1)
    state_s5_re = nrm((DEPTH, DEC_BATCH, S5_GROUPS, S5_N), 0.1)
    state_s5_im = nrm((DEPTH, DEC_BATCH, S5_GROUPS, S5_N), 0.1)
    state_lru = nrm((DEPTH, DEC_BATCH, LRU_WIDTH), 0.5)
    state_lru_conv = nrm((DEPTH, DEC_BATCH, LRU_CONV - 1, LRU_WIDTH))
    state_ffn_conv = nrm((DEPTH, DEC_BATCH, FFN_CONV - 1, 2 * D_FF))
    page_table = jax.random.permutation(next(ks), n_pool)[:n_used].reshape(DEC_BATCH, n_pages).astype(jnp.int32)
    c_prompt = nrm((BATCH, D))
    c_sample = nrm((DEC_BATCH, D))
    a_im0 = jnp.pi * jnp.arange(S5_N, dtype=jnp.float32)
    log_dt = jax.random.uniform(next(ks), (DEPTH, S5_GROUPS), jnp.float32,
                                math.log(S5_DT_MIN), math.log(S5_DT_MAX))
    s_lru = jax.random.uniform(next(ks), (DEPTH, LRU_WIDTH), jnp.float32, 0.9, 0.999) ** (1.0 / LRU_C)
    lru_lambda = jnp.log(s_lru) - jnp.log1p(-s_lru)
    return {
        'x_prompt': x_prompt, 'x_sample': x_sample,
        'cache_k': cache_k, 'cache_v': cache_v,
        'state_gla': state_gla, 'state_s5_re': state_s5_re, 'state_s5_im': state_s5_im,
        'state_lru': state_lru, 'state_lru_conv': state_lru_conv, 'state_ffn_conv': state_ffn_conv,
        'page_table': page_table, 'c_prompt': c_prompt, 'c_sample': c_sample,
        'w_ada': nrm((DEPTH, D, 6 * D), 0.5 * D ** -0.5),
        'b_ada': nrm((DEPTH, 6 * D), 0.02),
        'w_in': nrm((DEPTH, D, N_IN), D ** -0.5),
        'gla_wa2': nrm((DEPTH, GLA_LOWRANK, GLA_HEADS * GLA_DK), GLA_LOWRANK ** -0.5),
        'gla_ba': nrm((DEPTH, GLA_HEADS * GLA_DK), 0.1),
        'gla_norm': 1.0 + nrm((DEPTH, GLA_DV), 0.01),
        's5_a_re': -0.5 + nrm((DEPTH, S5_GROUPS, S5_N), 0.01),
        's5_a_im': a_im0 + nrm((DEPTH, S5_GROUPS, S5_N), 0.01),
        's5_log_dt': log_dt,
        's5_b_re': nrm((DEPTH, S5_GROUPS, S5_N, S5_GROUP), (2.0 * S5_GROUP) ** -0.5),
        's5_b_im': nrm((DEPTH, S5_GROUPS, S5_N, S5_GROUP), (2.0 * S5_GROUP) ** -0.5),
        's5_c_re': nrm((DEPTH, S5_GROUPS, S5_GROUP, S5_N), (2.0 * S5_N) ** -0.5),
        's5_c_im': nrm((DEPTH, S5_GROUPS, S5_GROUP, S5_N), (2.0 * S5_N) ** -0.5),
        's5_d': nrm((DEPTH, S5_WIDTH)),
        's5_w_glu': nrm((DEPTH, S5_WIDTH, S5_WIDTH), S5_WIDTH ** -0.5),
        's5_b_glu': nrm((DEPTH, S5_WIDTH), 0.02),
        'lru_conv_w': nrm((DEPTH, LRU_CONV, LRU_WIDTH), LRU_CONV ** -0.5),
        'lru_conv_b': nrm((DEPTH, LRU_WIDTH), 0.02),
        'lru_w_a': nrm((DEPTH, LRU_HEADS, LRU_BLOCK, LRU_BLOCK), LRU_BLOCK ** -0.5),
        'lru_b_a': nrm((DEPTH, LRU_WIDTH), 0.02),
        'lru_w_x': nrm((DEPTH, LRU_HEADS, LRU_BLOCK, LRU_BLOCK), LRU_BLOCK ** -0.5),
        'lru_b_x': nrm((DEPTH, LRU_WIDTH), 0.02),
        'lru_lambda': lru_lambda,
        'da_lq1': nrm((DEPTH, DA_DK), 0.1), 'da_lk1': nrm((DEPTH, DA_DK), 0.1),
        'da_lq2': nrm((DEPTH, DA_DK), 0.1), 'da_lk2': nrm((DEPTH, DA_DK), 0.1),
        'da_norm': 1.0 + nrm((DEPTH, DA_DV), 0.01),
        'w_branch': nrm((DEPTH, N_BRANCH, BRANCH_WIDTH, D), BRANCH_WIDTH ** -0.5),
        'w_out': nrm((DEPTH, D, D), DEEPNORM_BETA * D ** -0.5),
        'ln1_g': 1.0 + nrm((DEPTH, D), 0.01), 'ln1_b': nrm((DEPTH, D), 0.01),
        'ffn_w_up': nrm((DEPTH, D, 2 * D_FF), D ** -0.5),
        'ffn_conv_w': nrm((DEPTH, FFN_CONV, 2 * D_FF), FFN_CONV ** -0.5),
        'ffn_conv_b': nrm((DEPTH, 2 * D_FF), 0.02),
        'ffn_w_down': nrm((DEPTH, D_FF, D), DEEPNORM_BETA * D_FF ** -0.5),
        'ln2_g': 1.0 + nrm((DEPTH, D), 0.01), 'ln2_b': nrm((DEPTH, D), 0.01),
    }


def reference(x_prompt, x_sample, cache_k, cache_v, state_gla, state_s5_re, state_s5_im, state_lru,
              state_lru_conv, state_ffn_conv, page_table, c_prompt, c_sample,
              w_ada, b_ada, w_in, gla_wa2, gla_ba, gla_norm,
              s5_a_re, s5_a_im, s5_log_dt, s5_b_re, s5_b_im, s5_c_re, s5_c_im, s5_d, s5_w_glu, s5_b_glu,
              lru_conv_w, lru_conv_b, lru_w_a, lru_b_a, lru_w_x, lru_b_x, lru_lambda,
              da_lq1, da_lk1, da_lq2, da_lk2, da_norm, w_branch, w_out, ln1_g, ln1_b,
              ffn_w_up, ffn_conv_w, ffn_conv_b, ffn_w_down, ln2_g, ln2_b):
    params = dict(w_ada=w_ada, b_ada=b_ada, w_in=w_in, gla_wa2=gla_wa2, gla_ba=gla_ba, gla_norm=gla_norm,
                  s5_a_re=s5_a_re, s5_a_im=s5_a_im, s5_log_dt=s5_log_dt, s5_b_re=s5_b_re, s5_b_im=s5_b_im,
                  s5_c_re=s5_c_re, s5_c_im=s5_c_im, s5_d=s5_d, s5_w_glu=s5_w_glu, s5_b_glu=s5_b_glu,
                  lru_conv_w=lru_conv_w, lru_conv_b=lru_conv_b, lru_w_a=lru_w_a, lru_b_a=lru_b_a,
                  lru_w_x=lru_w_x, lru_b_x=lru_b_x, lru_lambda=lru_lambda,
                  da_lq1=da_lq1, da_lk1=da_lk1, da_lq2=da_lq2, da_lk2=da_lk2, da_norm=da_norm,
                  w_branch=w_branch, w_out=w_out, ln1_g=ln1_g, ln1_b=ln1_b,
                  ffn_w_up=ffn_w_up, ffn_conv_w=ffn_conv_w, ffn_conv_b=ffn_conv_b, ffn_w_down=ffn_w_down,
                  ln2_g=ln2_g, ln2_b=ln2_b)
    bp = x_prompt.shape[0]
    dt = x_prompt.dtype
    zero_states = (jnp.zeros((DEPTH, bp, GLA_HEADS, GLA_DK, GLA_DV), dt),
                   jnp.zeros((DEPTH, bp, S5_GROUPS, S5_N), dt),
                   jnp.zeros((DEPTH, bp, S5_GROUPS, S5_N), dt),
                   jnp.zeros((DEPTH, bp, LRU_WIDTH), dt),
                   jnp.zeros((DEPTH, bp, LRU_CONV - 1, LRU_WIDTH), dt),
                   jnp.zeros((DEPTH, bp, FFN_CONV - 1, 2 * D_FF), dt))
    y_prompt, sp = _run_group(x_prompt, c_prompt, 0, None, zero_states, params)
    y_sample, ss = _run_group(x_sample, c_sample, PAST_LEN, (page_table, cache_k, cache_v),
                              (state_gla, state_s5_re, state_s5_im, state_lru, state_lru_conv, state_ffn_conv),
                              params)
    k_prompt, v_prompt, gla_prompt, s5_re_prompt, s5_im_prompt, lru_prompt, lru_conv_prompt, ffn_conv_prompt = sp
    k_sample, v_sample, gla_sample, s5_re_sample, s5_im_sample, lru_sample, lru_conv_sample, ffn_conv_sample = ss
    return (y_prompt, y_sample, k_prompt, k_sample, v_prompt, v_sample, gla_prompt, gla_sample,
            s5_re_prompt, s5_re_sample, s5_im_prompt, s5_im_sample, lru_prompt, lru_sample,
            lru_conv_prompt, lru_conv_sample, ffn_conv_prompt, ffn_conv_sample)
```

```python
import functools
import math

import numpy as np
import jax
import jax.numpy as jnp
from jax import lax
from jax.experimental import pallas as pl
from jax.experimental.pallas import tpu as pltpu

F32 = jnp.float32
BF16 = jnp.bfloat16

D_MODEL = 2048
DEPTH = 2
PAGE_SIZE = 128
N_BRANCH = 4
BRANCH_WIDTH = 512
GLA_HEADS = 4
GLA_DK = 64
GLA_DV = 128
GLA_LOWRANK = 16
GLA_TAU = 16.0
S5_GROUP = 16
S5_GROUPS = 32
S5_N = 64
S5_STATES = S5_GROUPS * S5_N
LRU_HEADS = 4
LRU_BLOCK = 128
LRU_CONV = 4
LRU_C = 8.0
DA_HEADS = 4
DA_DK = 64
DA_DV = 128
ROPE_DIM = 16
ROPE_THETA = 500000.0
D_FF = 5632
FFN_CONV = 3
DEEPNORM_ALPHA = (2.0 * DEPTH) ** 0.25
EPS = 1e-5

LANES = 128
SUBLANES = 8
VMEM_LIMIT = 56 * 1024 * 1024

Z_MAIN = 12800
COL_GQ, COL_GK, COL_GV, COL_OG = 8192, 8448, 8704, 9216
COL_SU, COL_LX, COL_LG = 9728, 10240, 10752
COL_DQ, COL_DK, COL_DV = 11264, 11776, 12288

_NT = (((1,), (1,)), ((), ()))
_TN = (((0,), (0,)), ((), ()))


def _cparams(sem):
    return pltpu.CompilerParams(dimension_semantics=sem, vmem_limit_bytes=VMEM_LIMIT)


def _dot(a, b):
    return jnp.dot(a, b, preferred_element_type=F32)


def _sigmoid(x):
    return 1.0 / (1.0 + jnp.exp(-x))


def _gelu(x):
    return 0.5 * x * (1.0 + jnp.tanh(math.sqrt(2.0 / math.pi) * (x + 0.044715 * (x * x * x))))


def _layer_norm(y, g, b):
    mu = jnp.mean(y, axis=-1, keepdims=True)
    d = y - mu
    var = jnp.mean(d * d, axis=-1, keepdims=True)
    return d * lax.rsqrt(var + EPS) * g + b


def _ada_kernel(c_ref, w_ref, b_ref, o_ref):
    c = c_ref[...]
    s = c * _sigmoid(c)
    o_ref[...] = _dot(s.astype(BF16), w_ref[...].astype(BF16)) + b_ref[...]


def _ada(c_all, w_ada, b_ada):
    rows = c_all.shape[0]
    n = w_ada.shape[-1]
    tn = 1536
    return pl.pallas_call(
        _ada_kernel,
        grid=(DEPTH, n // tn),
        in_specs=[pl.BlockSpec((rows, D_MODEL), lambda l, j: (0, 0)),
                  pl.BlockSpec((None, D_MODEL, tn), lambda l, j: (l, 0, j)),
                  pl.BlockSpec((None, 1, tn), lambda l, j: (l, 0, j))],
        out_specs=pl.BlockSpec((None, rows, tn), lambda l, j: (l, 0, j)),
        out_shape=jax.ShapeDtypeStruct((DEPTH, rows, n), F32),
        compiler_params=_cparams(("arbitrary", "arbitrary")),
        name="ada",
    )(c_all, w_ada, b_ada.reshape(DEPTH, 1, n))


def _inproj_kernel(x_ref, sc_ref, sh_ref, w_ref, wlr_ref, z_ref, zlr_ref, u_scr):
    @pl.when(pl.program_id(1) == 0)
    def _():
        u = (x_ref[...] * (1.0 + sc_ref[...]) + sh_ref[...]).astype(BF16)
        u_scr[...] = u
        zlr_ref[...] = _dot(u, wlr_ref[...])

    z_ref[...] = _dot(u_scr[...], w_ref[...])


def _inproj(x2d, sc, sh, gdiv, w_main, w_lr, tm, tn):
    m = x2d.shape[0]
    r = sc.shape[1]
    mod_spec = pl.BlockSpec((None, r, D_MODEL), lambda i, j: (i // gdiv, 0, 0))
    return pl.pallas_call(
        _inproj_kernel,
        grid=(m // tm, Z_MAIN // tn),
        in_specs=[pl.BlockSpec((tm, D_MODEL), lambda i, j: (i, 0)), mod_spec, mod_spec,
                  pl.BlockSpec((D_MODEL, tn), lambda i, j: (0, j)),
                  pl.BlockSpec((D_MODEL, LANES), lambda i, j: (0, 0))],
        out_specs=[pl.BlockSpec((tm, tn), lambda i, j: (i, j)),
                   pl.BlockSpec((tm, LANES), lambda i, j: (i, 0))],
        out_shape=[jax.ShapeDtypeStruct((m, Z_MAIN), F32), jax.ShapeDtypeStruct((m, LANES), F32)],
        scratch_shapes=[pltpu.VMEM((tm, D_MODEL), BF16)],
        compiler_params=_cparams(("arbitrary", "arbitrary")),
        name="inproj",
    )(x2d, sc, sh, w_main, w_lr)


def _rope_tables(pos0, length):
    half = ROPE_DIM // 2
    inv = ROPE_THETA ** (-jnp.arange(half, dtype=F32) * 2.0 / ROPE_DIM)
    ang = (pos0 + jnp.arange(length)).astype(F32)[:, None] * inv
    cos, sin = jnp.cos(ang), jnp.sin(ang)
    ones = jnp.ones((length, DA_DK - ROPE_DIM), F32)
    zeros = jnp.zeros((length, DA_DK - ROPE_DIM), F32)
    zh = jnp.zeros((length, half), F32)
    c = jnp.concatenate([cos, cos, ones], axis=1)
    s_next = jnp.concatenate([-sin, zh, zeros], axis=1)
    s_prev = jnp.concatenate([zh, sin, zeros], axis=1)
    rep = LANES // DA_DK
    return jnp.tile(c, (1, rep)), jnp.tile(s_next, (1, rep)), jnp.tile(s_prev, (1, rep))


def _rope_kernel(q_ref, k_ref, c_ref, sn_ref, sp_ref, qo_ref, ko_ref):
    half = ROPE_DIM // 2
    c, sn, sp = c_ref[...], sn_ref[...], sp_ref[...]
    for src, dst in ((q_ref, qo_ref), (k_ref, ko_ref)):
        for i in range(src.shape[1] // LANES):
            x = src[:, i * LANES:(i + 1) * LANES]
            nxt = pltpu.roll(x, LANES - half, axis=1)
            prv = pltpu.roll(x, half, axis=1)
            dst[:, i * LANES:(i + 1) * LANES] = x * c + nxt * sn + prv * sp


def _rope(z, tabs, nb, length, tm):
    m = z.shape[0]
    w = DA_HEADS * 2 * DA_DK
    nj = length // tm
    tab_spec = pl.BlockSpec((tm, LANES), lambda b, j: (j, 0))
    return pl.pallas_call(
        _rope_kernel,
        grid=(nb, nj),
        in_specs=[pl.BlockSpec((tm, w), lambda b, j: (b * nj + j, COL_DQ // w)),
                  pl.BlockSpec((tm, w), lambda b, j: (b * nj + j, COL_DK // w)),
                  tab_spec, tab_spec, tab_spec],
        out_specs=[pl.BlockSpec((tm, w), lambda b, j: (b * nj + j, 0))] * 2,
        out_shape=[jax.ShapeDtypeStruct((m, w), F32)] * 2,
        compiler_params=_cparams(("arbitrary", "arbitrary")),
        name="rope",
    )(z, z, *tabs)


def _gla_consts(c, seq):
    nlev = int(math.log2(seq))
    t = np.arange(c)
    tt, rr = t[:, None], t[None, :]
    same = (tt // seq) == (rr // seq)
    tril = same & (rr <= tt)
    seg_end = (t // seq) * seq + seq - 1
    dk = same & (rr > tt) & (rr <= seg_end[:, None])
    mats = [tril.astype(np.float32), dk.astype(np.float32)]
    lvl = np.full((c, c), -1, np.int32)
    lvl[t, t] = nlev
    for i in range(nlev):
        b = seq >> (i + 1)
        mid = (t // (2 * b)) * (2 * b) + b - 1
        mb = same & (rr <= mid[:, None])
        mats.append(tril.astype(np.float32) - mb.astype(np.float32))
        valid = ((tt // (2 * b)) == (rr // (2 * b))) & ((tt % (2 * b)) >= b) & ((rr % (2 * b)) < b)
        lvl[valid] = i
    return np.concatenate(mats, axis=0), lvl, nlev


def _gla_kernel(q_ref, k_ref, v_ref, og_ref, lr_ref, s0_ref, lmat_ref, lvl_ref, wa2_ref, ba_ref, gn_ref,
                o_ref, st_ref, *, c, seq, nlev):
    nb = c // seq
    j = pl.program_id(1)

    @pl.when(j == 0)
    def _():
        st_ref[...] = s0_ref[...]

    x = _dot(lr_ref[...].astype(BF16), wa2_ref[...]) + ba_ref[...]
    la = (jnp.minimum(x, 0.0) - jnp.log1p(jnp.exp(-jnp.abs(x)))) * (1.0 / GLA_TAU)
    e_all = jnp.dot(lmat_ref[...], la, precision=lax.Precision.HIGHEST, preferred_element_type=F32)
    cum, e_end = e_all[0:c], e_all[c:2 * c]
    q = q_ref[...] * (GLA_DK ** -0.5)
    k = k_ref[...]
    lvl = lvl_ref[...]
    lane = lax.broadcasted_iota(jnp.int32, (c, LANES), 1)
    if nb > 1:
        rowb = lax.broadcasted_iota(jnp.int32, (c, nb * LANES), 0) // seq
        colb = lax.broadcasted_iota(jnp.int32, (c, nb * LANES), 1) // LANES
        bmask = rowb == colb

    def spread(a):
        if nb == 1:
            return a
        return jnp.where(bmask, jnp.concatenate([a] * nb, axis=1), 0.0)

    for p in range(2):
        sl = slice(p * LANES, (p + 1) * LANES)
        qp, kp, cump = q[:, sl], k[:, sl], cum[:, sl]
        qts, kts = [], []
        for i in range(nlev):
            eb = e_all[(2 + i) * c:(3 + i) * c, sl]
            qts.append(qp * jnp.exp(eb))
            kts.append((kp * jnp.exp(-eb)).astype(BF16))
        qg = qp * jnp.exp(cump)
        kh = kp * jnp.exp(e_end[:, sl])
        kpb = kp.astype(BF16)
        decays = []
        for b in range(nb):
            cl = cump[b * seq + seq - 1:b * seq + seq, :]
            decays.append(jnp.broadcast_to(jnp.exp(cl), (LANES, LANES)).T)
        for hh in range(2):
            h = 2 * p + hh
            hm = (lane // GLA_DK) == hh
            att = lax.dot_general(jnp.where(hm, qp, 0.0).astype(BF16), kpb, _NT, preferred_element_type=F32)
            att = jnp.where(lvl == nlev, att, 0.0)
            for i in range(nlev):
                a = lax.dot_general(jnp.where(hm, qts[i], 0.0).astype(BF16), kts[i], _NT,
                                    preferred_element_type=F32)
                att = jnp.where(lvl == i, a, att)
            vh = v_ref[:, h * GLA_DV:(h + 1) * GLA_DV].astype(BF16)
            o = _dot(att.astype(BF16), vh)
            s_all = st_ref[:, h].reshape(nb * LANES, GLA_DV)
            o = o + _dot(spread(jnp.where(hm, qg, 0.0)).astype(BF16), s_all.astype(BF16))
            upd = lax.dot_general(spread(jnp.where(hm, kh, 0.0)).astype(BF16), vh, _TN,
                                  preferred_element_type=F32)
            for b in range(nb):
                st_ref[b, h] = st_ref[b, h] * decays[b] + upd[b * LANES:(b + 1) * LANES]
            ms = jnp.mean(o * o, axis=-1, keepdims=True)
            og = og_ref[:, h * GLA_DV:(h + 1) * GLA_DV]
            o_ref[:, h * GLA_DV:(h + 1) * GLA_DV] = o * lax.rsqrt(ms + EPS) * gn_ref[...] * (og * _sigmoid(og))


def _gla(z, zlr, s0_pad, wa2p, ba, gn, ng, length, c, seq):
    m = z.shape[0]
    nb = c // seq
    nj = length // c
    lmat, lvl, nlev = _gla_consts(c, seq)
    w = GLA_HEADS * GLA_DK
    wv = GLA_HEADS * GLA_DV
    row = lambda g, j: g * nj + j
    const2 = lambda g, j: (0, 0)
    st_spec = pl.BlockSpec((nb, GLA_HEADS, LANES, GLA_DV), lambda g, j: (g, 0, 0, 0))
    kern = functools.partial(_gla_kernel, c=c, seq=seq, nlev=nlev)
    return pl.pallas_call(
        kern,
        grid=(ng, nj),
        in_specs=[pl.BlockSpec((c, w), lambda g, j: (row(g, j), COL_GQ // w)),
                  pl.BlockSpec((c, w), lambda g, j: (row(g, j), COL_GK // w)),
                  pl.BlockSpec((c, wv), lambda g, j: (row(g, j), COL_GV // wv)),
                  pl.BlockSpec((c, wv), lambda g, j: (row(g, j), COL_OG // wv)),
                  pl.BlockSpec((c, LANES), lambda g, j: (row(g, j), 0)),
                  st_spec,
                  pl.BlockSpec(lmat.shape, const2),
                  pl.BlockSpec(lvl.shape, const2),
                  pl.BlockSpec((LANES, w), const2),
                  pl.BlockSpec((1, w), const2),
                  pl.BlockSpec((1, GLA_DV), const2)],
        out_specs=[pl.BlockSpec((c, wv), lambda g, j: (row(g, j), 0)), st_spec],
        out_shape=[jax.ShapeDtypeStruct((m, wv), F32),
                   jax.ShapeDtypeStruct(s0_pad.shape, F32)],
        compiler_params=_cparams(("arbitrary", "arbitrary")),
        name="gla",
    )(z, z, z, z, zlr, s0_pad, jnp.asarray(lmat), jnp.asarray(lvl), wa2p, ba, gn)


def _gla_state_pad(s):
    b = s.shape[0]
    s6 = s.reshape(b, 2, 2, 1, GLA_DK, GLA_DV) * jnp.eye(2, dtype=F32)[None, None, :, :, None, None]
    return s6.reshape(b, GLA_HEADS, LANES, GLA_DV)


def _gla_state_unpad(sp):
    b = sp.shape[0]
    s6 = sp.reshape(b, 2, 2, 2, GLA_DK, GLA_DV)
    return jnp.stack([s6[:, :, 0, 0], s6[:, :, 1, 1]], axis=2).reshape(b, GLA_HEADS, GLA_DK, GLA_DV)


_S5_CH = S5_STATES // LANES


def _s5_kernel(u_ref, h0r_ref, h0i_ref, are_ref, aim_ref, ldt_ref, bdr_ref, bdi_ref, cdr_ref, cdi_ref,
               d_ref, wg_ref, bg_ref, y_ref, hfr_ref, hfi_ref, sre, sim, hcr, hci, *, nseq, sl):
    tm = nseq * sl
    j = pl.program_id(1)

    @pl.when(j == 0)
    def _():
        hcr[...] = h0r_ref[...]
        hci[...] = h0i_ref[...]

    lr, li = are_ref[...], aim_ref[...]
    dt = jnp.exp(ldt_ref[...])
    mag = jnp.exp(lr * dt)
    abr, abi = mag * jnp.cos(li * dt), mag * jnp.sin(li * dt)
    den = lr * lr + li * li
    fr = ((abr - 1.0) * lr + abi * li) / den
    fi = (abi * lr - (abr - 1.0) * li) / den

    u = u_ref[...]
    ub = u.astype(BF16)
    for c2 in range(_S5_CH // 2):
        tr = _dot(ub, bdr_ref[:, c2 * 256:(c2 + 1) * 256])
        ti = _dot(ub, bdi_ref[:, c2 * 256:(c2 + 1) * 256])
        for e in range(2):
            ch = 2 * c2 + e
            trc, tic = tr[:, e * LANES:(e + 1) * LANES], ti[:, e * LANES:(e + 1) * LANES]
            frc, fic = fr[ch:ch + 1, :], fi[ch:ch + 1, :]
            sre[ch * tm:(ch + 1) * tm, :] = frc * trc - fic * tic
            sim[ch * tm:(ch + 1) * tm, :] = frc * tic + fic * trc

    ar0, ai0, ar1, ai1 = abr[0:8], abi[0:8], abr[8:16], abi[8:16]
    for s in range(nseq):
        def body(t, carry, s=s):
            r0, i0, r1, i1 = carry
            lo = pl.ds(s * sl + t, SUBLANES, stride=tm)
            hi = pl.ds(SUBLANES * tm + s * sl + t, SUBLANES, stride=tm)
            n_r0 = ar0 * r0 - ai0 * i0 + sre[lo, :]
            n_i0 = ar0 * i0 + ai0 * r0 + sim[lo, :]
            n_r1 = ar1 * r1 - ai1 * i1 + sre[hi, :]
            n_i1 = ar1 * i1 + ai1 * r1 + sim[hi, :]
            sre[lo, :] = n_r0
            sim[lo, :] = n_i0
            sre[hi, :] = n_r1
            sim[hi, :] = n_i1
            return n_r0, n_i0, n_r1, n_i1

        init = (hcr[s, 0:8], hci[s, 0:8], hcr[s, 8:16], hci[s, 8:16])
        r0, i0, r1, i1 = lax.fori_loop(0, sl, body, init, unroll=8)
        hcr[s, 0:8] = r0
        hci[s, 0:8] = i0
        hcr[s, 8:16] = r1
        hci[s, 8:16] = i1

    y = d_ref[...] * u
    for c2 in range(_S5_CH // 2):
        a, b = 2 * c2, 2 * c2 + 1
        hre = jnp.concatenate([sre[a * tm:(a + 1) * tm, :], sre[b * tm:(b + 1) * tm, :]], axis=1).astype(BF16)
        him = jnp.concatenate([sim[a * tm:(a + 1) * tm, :], sim[b * tm:(b + 1) * tm, :]], axis=1).astype(BF16)
        y = y + _dot(hre, cdr_ref[c2 * 256:(c2 + 1) * 256, :]) - _dot(him, cdi_ref[c2 * 256:(c2 + 1) * 256, :])
    y = _gelu(y)
    y_ref[...] = y * _sigmoid(_dot(y.astype(BF16), wg_ref[...]) + bg_ref[...])

    @pl.when(j == pl.num_programs(1) - 1)
    def _():
        hfr_ref[...] = hcr[...]
        hfi_ref[...] = hci[...]


def _s5(z, h0r, h0i, p, ng, length, nseq, sl):
    m = z.shape[0]
    tm = nseq * sl
    nj = length // tm
    w = BRANCH_WIDTH
    const2 = lambda g, j: (0, 0)
    st_spec = pl.BlockSpec((None, nseq, _S5_CH, LANES), lambda g, j: (g, 0, 0, 0))
    vec = pl.BlockSpec((_S5_CH, LANES), const2)
    kern = functools.partial(_s5_kernel, nseq=nseq, sl=sl)
    return pl.pallas_call(
        kern,
        grid=(ng, nj),
        in_specs=[pl.BlockSpec((tm, w), lambda g, j: (g * nj + j, COL_SU // w)),
                  st_spec, st_spec, vec, vec, vec,
                  pl.BlockSpec((w, S5_STATES), const2), pl.BlockSpec((w, S5_STATES), const2),
                  pl.BlockSpec((S5_STATES, w), const2), pl.BlockSpec((S5_STATES, w), const2),
                  pl.BlockSpec((1, w), const2), pl.BlockSpec((w, w), const2), pl.BlockSpec((1, w), const2)],
        out_specs=[pl.BlockSpec((tm, w), lambda g, j: (g * nj + j, 0)), st_spec, st_spec],
        out_shape=[jax.ShapeDtypeStruct((m, w), F32),
                   jax.ShapeDtypeStruct(h0r.shape, F32), jax.ShapeDtypeStruct(h0i.shape, F32)],
        scratch_shapes=[pltpu.VMEM((_S5_CH * tm, LANES), F32), pltpu.VMEM((_S5_CH * tm, LANES), F32),
                        pltpu.VMEM((nseq, _S5_CH, LANES), F32), pltpu.VMEM((nseq, _S5_CH, LANES), F32)],
        compiler_params=_cparams(("arbitrary", "arbitrary")),
        name="s5",
    )(z, h0r, h0i, p["are"], p["aim"], p["ldt"], p["bdr"], p["bdi"], p["cdr"], p["cdi"],
      p["d"], p["wg"], p["bg"])


def _s5_params(a_re, a_im, log_dt, b_re, b_im, c_re, c_im, d, w_glu, b_glu):
    eye = jnp.eye(S5_GROUPS, dtype=F32)

    def bd_in(b):
        return (eye[:, None, :, None] * jnp.swapaxes(b, 1, 2)[:, :, None, :]).reshape(
            S5_GROUPS * S5_GROUP, S5_STATES).astype(BF16)

    def bd_out(c):
        return (eye[:, None, :, None] * jnp.swapaxes(c, 1, 2)[:, :, None, :]).reshape(
            S5_STATES, S5_GROUPS * S5_GROUP).astype(BF16)

    return dict(are=a_re.reshape(_S5_CH, LANES), aim=a_im.reshape(_S5_CH, LANES),
                ldt=jnp.broadcast_to(log_dt[:, None], (S5_GROUPS, S5_N)).reshape(_S5_CH, LANES),
                bdr=bd_in(b_re), bdi=bd_in(b_im), cdr=bd_out(c_re), cdi=bd_out(c_im),
                d=d.reshape(1, -1), wg=w_glu.astype(BF16), bg=b_glu.reshape(1, -1))


_HALO = SUBLANES


def _lru_kernel(lx_ref, lg_ref, zprev_ref, cst_ref, h0_ref, cw_ref, cb_ref, wa_ref, ba_ref, wx_ref, bx_ref,
                lam_ref, y_ref, hfin_ref, cs_scr, xc_scr, a_scr, b_scr, h_c, *, nseq, sl):
    j = pl.program_id(1)

    @pl.when(j == 0)
    def _():
        h_c[...] = h0_ref[...]

    k = LRU_CONV
    for s in range(nseq):
        if nseq == 1:
            cs_scr[0:_HALO, :] = jnp.where(j == 0, cst_ref[0], zprev_ref[...])
        else:
            cs_scr[0:_HALO, :] = cst_ref[s]
        cs_scr[_HALO:_HALO + sl, :] = lx_ref[s * sl:(s + 1) * sl, :]
        xc = cb_ref[...]
        for t in range(k):
            off = _HALO - (k - 1) + t
            xc = xc + cw_ref[t:t + 1, :] * cs_scr[off:off + sl, :]
        xc_scr[s * sl:(s + 1) * sl, :] = xc

    lam = lam_ref[...]
    sp = jnp.maximum(-lam, 0.0) + jnp.log1p(jnp.exp(-jnp.abs(lam)))
    for h in range(LRU_HEADS):
        hs = slice(h * LRU_BLOCK, (h + 1) * LRU_BLOCK)
        xc = xc_scr[:, hs]
        xb = xc.astype(BF16)
        r = _sigmoid(_dot(xb, wa_ref[h]) + ba_ref[:, hs])
        i = _sigmoid(_dot(xb, wx_ref[h]) + bx_ref[:, hs])
        la = -LRU_C * r * sp[:, hs]
        a_scr[:, hs] = jnp.exp(la)
        b_scr[:, hs] = jnp.sqrt(-jnp.tanh(la) * (jnp.exp(2.0 * la) + 1.0)) * (i * xc)

    for s in range(nseq):
        def body(t, h, s=s):
            idx = pl.ds(s * sl + t, 1)
            h = a_scr[idx, :] * h + b_scr[idx, :]
            b_scr[idx, :] = h
            return h

        h_c[s:s + 1, :] = lax.fori_loop(0, sl, body, h_c[s:s + 1, :], unroll=8)

    y_ref[...] = b_scr[...] * _gelu(lg_ref[...])

    @pl.when(j == pl.num_programs(1) - 1)
    def _():
        hfin_ref[...] = h_c[...]


def _lru(z, cst, h0, p, ng, length, nseq, sl):
    m = z.shape[0]
    tm = nseq * sl
    nj = length // tm
    w = BRANCH_WIDTH
    const2 = lambda g, j: (0, 0)
    const3 = lambda g, j: (0, 0, 0)
    st_spec = pl.BlockSpec((None, nseq, w), lambda g, j: (g, 0, 0))
    vec = pl.BlockSpec((1, w), const2)
    wsp = pl.BlockSpec((LRU_HEADS, LRU_BLOCK, LRU_BLOCK), const3)
    rb = tm // _HALO
    kern = functools.partial(_lru_kernel, nseq=nseq, sl=sl)
    return pl.pallas_call(
        kern,
        grid=(ng, nj),
        in_specs=[pl.BlockSpec((tm, w), lambda g, j: (g * nj + j, COL_LX // w)),
                  pl.BlockSpec((tm, w), lambda g, j: (g * nj + j, COL_LG // w)),
                  pl.BlockSpec((_HALO, w), lambda g, j: (jnp.maximum((g * nj + j) * rb - 1, 0), COL_LX // w)),
                  pl.BlockSpec((None, nseq, _HALO, w), lambda g, j: (g, 0, 0, 0)),
                  st_spec,
                  pl.BlockSpec((LRU_CONV, w), const2), vec, wsp, vec, wsp, vec, vec],
        out_specs=[pl.BlockSpec((tm, w), lambda g, j: (g * nj + j, 0)), st_spec],
        out_shape=[jax.ShapeDtypeStruct((m, w), F32), jax.ShapeDtypeStruct(h0.shape, F32)],
        scratch_shapes=[pltpu.VMEM((_HALO + sl, w), F32), pltpu.VMEM((tm, w), F32),
                        pltpu.VMEM((tm, w), F32), pltpu.VMEM((tm, w), F32),
                        pltpu.VMEM((nseq, w), F32)],
        compiler_params=_cparams(("arbitrary", "arbitrary")),
        name="lru",
    )(z, z, z, cst, h0, p["cw"], p["cb"], p["wa"], p["ba"], p["wx"], p["bx"], p["lam"])


def _da_lambda(lq1_ref, lk1_ref, lq2_ref, lk2_ref, lam_init):
    return (jnp.exp(jnp.sum(lq1_ref[...] * lk1_ref[...])) - jnp.exp(jnp.sum(lq2_ref[...] * lk2_ref[...]))
            + lam_init)


def _attn_kernel(q_ref, k_ref, v_ref, lq1_ref, lk1_ref, lq2_ref, lk2_ref, dn_ref, o_ref,
                 m_scr, l_scr, acc_scr, *, tq, tk, lam_init):
    qi, ki = pl.program_id(2), pl.program_id(3)

    @pl.when(ki == 0)
    def _():
        m_scr[...] = jnp.full(m_scr.shape, -jnp.inf, F32)
        l_scr[...] = jnp.zeros(l_scr.shape, F32)
        acc_scr[...] = jnp.zeros(acc_scr.shape, F32)

    @pl.when(ki * tk <= qi * tq + tq - 1)
    def _():
        q = q_ref[...] * (DA_DK ** -0.5)
        kb = k_ref[...].astype(BF16)
        vb = v_ref[...].astype(BF16)
        lane = lax.broadcasted_iota(jnp.int32, (tq, LANES), 1)
        row = qi * tq + lax.broadcasted_iota(jnp.int32, (tq, tk), 0)
        col = ki * tk + lax.broadcasted_iota(jnp.int32, (tq, tk), 1)
        mask = col <= row
        for c in range(2):
            qc = jnp.where((lane // DA_DK) == c, q, 0.0).astype(BF16)
            s = lax.dot_general(qc, kb, _NT, preferred_element_type=F32)
            s = jnp.where(mask, s, -jnp.inf)
            m_prev = m_scr[c]
            m_new = jnp.maximum(m_prev, jnp.max(s, axis=-1, keepdims=True))
            alpha = jnp.exp(m_prev - m_new)
            pexp = jnp.exp(s - m_new)
            l_scr[c] = alpha * l_scr[c] + jnp.sum(pexp, axis=-1, keepdims=True)
            acc_scr[c] = alpha * acc_scr[c] + _dot(pexp.astype(BF16), vb)
            m_scr[c] = m_new

    @pl.when(ki == pl.num_programs(3) - 1)
    def _():
        lam = _da_lambda(lq1_ref, lk1_ref, lq2_ref, lk2_ref, lam_init)
        o = acc_scr[0] / l_scr[0] - lam * (acc_scr[1] / l_scr[1])
        ms = jnp.mean(o * o, axis=-1, keepdims=True)
        o_ref[...] = o * lax.rsqrt(ms + EPS) * dn_ref[...] * (1.0 - lam_init)


def _attn_prompt(q_rot, k_rot, z, lam_p, dn, nb, length, tq, tk, lam_init):
    m = q_rot.shape[0]
    nq, nk = length // tq, length // tk
    const2 = lambda b, h, qi, ki: (0, 0)
    kv_row = lambda b, qi, ki: b * nk + jnp.minimum(ki, (qi * tq + tq - 1) // tk)
    vec = pl.BlockSpec((1, DA_DK), const2)
    kern = functools.partial(_attn_kernel, tq=tq, tk=tk, lam_init=lam_init)
    return pl.pallas_call(
        kern,
        grid=(nb, DA_HEADS, nq, nk),
        in_specs=[pl.BlockSpec((tq, LANES), lambda b, h, qi, ki: (b * nq + qi, h)),
                  pl.BlockSpec((tk, LANES), lambda b, h, qi, ki: (kv_row(b, qi, ki), h)),
                  pl.BlockSpec((tk, DA_DV), lambda b, h, qi, ki: (kv_row(b, qi, ki), COL_DV // DA_DV + h)),
                  vec, vec, vec, vec, pl.BlockSpec((1, DA_DV), const2)],
        out_specs=pl.BlockSpec((tq, DA_DV), lambda b, h, qi, ki: (b * nq + qi, h)),
        out_shape=jax.ShapeDtypeStruct((m, DA_HEADS * DA_DV), F32),
        scratch_shapes=[pltpu.VMEM((2, tq, 1), F32), pltpu.VMEM((2, tq, 1), F32),
                        pltpu.VMEM((2, tq, DA_DV), F32)],
        compiler_params=_cparams(("arbitrary", "arbitrary", "arbitrary", "arbitrary")),
        name="attn_prompt",
    )(q_rot, k_rot, z, *lam_p, dn)


_PAGES_PER_STEP = 8
_QROWS = DA_HEADS * 2 * SUBLANES


def _attn_sample_kernel(pt_ref, q_ref, *refs, lq, lam_init):
    npg = _PAGES_PER_STEP
    k_refs, v_refs = refs[0:npg], refs[npg:2 * npg]
    kn_ref, vn_ref, lq1_ref, lk1_ref, lq2_ref, lk2_ref, dn_ref = refs[2 * npg:2 * npg + 7]
    o_ref = refs[2 * npg + 7]
    qbd_scr, m_scr, l_scr, acc_scr = refs[2 * npg + 8:]
    w = DA_HEADS * 2 * DA_DK
    j = pl.program_id(1)

    @pl.when(j == 0)
    def _():
        q = q_ref[...] * (DA_DK ** -0.5)
        lane = lax.broadcasted_iota(jnp.int32, (lq, w), 1)
        tiles = [jnp.where((lane // DA_DK) == hc, q, 0.0) for hc in range(DA_HEADS * 2)]
        qbd_scr[...] = jnp.concatenate(tiles, axis=0)
        m_scr[...] = jnp.full(m_scr.shape, -jnp.inf, F32)
        l_scr[...] = jnp.zeros(l_scr.shape, F32)
        acc_scr[...] = jnp.zeros(acc_scr.shape, F32)

    qbd = qbd_scr[...].astype(BF16)

    def update(kb, vb, mask):
        s = lax.dot_general(qbd, kb, _NT, preferred_element_type=F32)
        if mask is not None:
            s = jnp.where(mask, s, -jnp.inf)
        m_prev = m_scr[...]
        m_new = jnp.maximum(m_prev, jnp.max(s, axis=-1, keepdims=True))
        alpha = jnp.exp(m_prev - m_new)
        pexp = jnp.exp(s - m_new)
        l_scr[...] = alpha * l_scr[...] + jnp.sum(pexp, axis=-1, keepdims=True)
        acc_scr[...] = alpha * acc_scr[...] + _dot(pexp.astype(BF16), vb)
        m_scr[...] = m_new

    for i in range(npg):
        update(k_refs[i][...].astype(BF16), v_refs[i][...].astype(BF16), None)

    @pl.when(j == pl.num_programs(1) - 1)
    def _():
        nk = kn_ref.shape[0]
        t = lax.broadcasted_iota(jnp.int32, (_QROWS, nk), 0) % lq
        col = lax.broadcasted_iota(jnp.int32, (_QROWS, nk), 1)
        update(kn_ref[...].astype(BF16), vn_ref[...].astype(BF16), col <= t)
        lam = _da_lambda(lq1_ref, lk1_ref, lq2_ref, lk2_ref, lam_init)
        for h in range(DA_HEADS):
            r0 = h * 2 * lq
            cs = slice(h * DA_DV, (h + 1) * DA_DV)
            o0 = acc_scr[r0:r0 + lq, cs] / l_scr[r0:r0 + lq]
            o1 = acc_scr[r0 + lq:r0 + 2 * lq, cs] / l_scr[r0 + lq:r0 + 2 * lq]
            o = o0 - lam * o1
            ms = jnp.mean(o * o, axis=-1, keepdims=True)
            o_ref[:, cs] = o * lax.rsqrt(ms + EPS) * dn_ref[...] * (1.0 - lam_init)


def _attn_sample(q_rot, cache_k, cache_v, layer, page_table, k_new, v_new, lam_p, dn, lam_init):
    nb, n_pages = page_table.shape
    lq = q_rot.shape[0] // nb
    w = DA_HEADS * 2 * DA_DK
    wv = DA_HEADS * DA_DV
    npg = _PAGES_PER_STEP
    nsteps = n_pages // npg
    ck = cache_k.reshape(DEPTH, -1, PAGE_SIZE, w)
    cv = cache_v.reshape(DEPTH, -1, PAGE_SIZE, wv)
    const2 = lambda b, j, pt: (0, 0)

    def page_spec(i, width):
        return pl.BlockSpec((None, None, PAGE_SIZE, width),
                            lambda b, j, pt: (layer, pt[b * n_pages + j * npg + i], 0, 0))

    vec = pl.BlockSpec((1, DA_DK), const2)
    nk = k_new.shape[1]
    grid_spec = pltpu.PrefetchScalarGridSpec(
        num_scalar_prefetch=1,
        grid=(nb, nsteps),
        in_specs=([pl.BlockSpec((lq, w), lambda b, j, pt: (b, 0))]
                  + [page_spec(i, w) for i in range(npg)] + [page_spec(i, wv) for i in range(npg)]
                  + [pl.BlockSpec((None, nk, w), lambda b, j, pt: (b, 0, 0)),
                     pl.BlockSpec((None, nk, wv), lambda b, j, pt: (b, 0, 0)),
                     vec, vec, vec, vec, pl.BlockSpec((1, DA_DV), const2)]),
        out_specs=pl.BlockSpec((lq, wv), lambda b, j, pt: (b, 0)),
        scratch_shapes=[pltpu.VMEM((_QROWS, w), F32), pltpu.VMEM((_QROWS, 1), F32),
                        pltpu.VMEM((_QROWS, 1), F32), pltpu.VMEM((_QROWS, wv), F32)],
    )
    kern = functools.partial(_attn_sample_kernel, lq=lq, lam_init=lam_init)
    return pl.pallas_call(
        kern,
        grid_spec=grid_spec,
        out_shape=jax.ShapeDtypeStruct((nb * lq, wv), F32),
        compiler_params=_cparams(("arbitrary", "arbitrary")),
        name="attn_sample",
    )(page_table.reshape(-1), q_rot, *([ck] * npg), *([cv] * npg), k_new, v_new, *lam_p, dn)


def _merge_kernel(o0_ref, o1_ref, o2_ref, o3_ref, g_ref, wb_ref, wo_ref, x_ref, g1_ref, lng_ref, lnb_ref,
                  out_ref, acc_scr):
    b = pl.program_id(1)

    @pl.when(b == 0)
    def _():
        acc_scr[...] = jnp.zeros(acc_scr.shape, F32)

    o = jnp.where(b == 0, o0_ref[...], jnp.where(b == 1, o1_ref[...], jnp.where(b == 2, o2_ref[...], o3_ref[...])))
    acc_scr[...] += _sigmoid(g_ref[...]) * _dot(o.astype(BF16), wb_ref[...])

    @pl.when(b == N_BRANCH - 1)
    def _():
        mix = _dot(acc_scr[...].astype(BF16), wo_ref[...])
        y = DEEPNORM_ALPHA * x_ref[...] + g1_ref[...] * mix
        out_ref[...] = _layer_norm(y, lng_ref[...], lnb_ref[...])


def _merge(branches, z, w_branch, w_out, x2d, g1, gdiv, ln_g, ln_b, tm):
    m = x2d.shape[0]
    d = D_MODEL
    r = g1.shape[1]
    const2 = lambda i, b: (0, 0)
    osp = pl.BlockSpec((tm, BRANCH_WIDTH), lambda i, b: (i, 0))
    return pl.pallas_call(
        _merge_kernel,
        grid=(m // tm, N_BRANCH),
        in_specs=[osp, osp, osp, osp,
                  pl.BlockSpec((tm, d), lambda i, b: (i, b)),
                  pl.BlockSpec((None, BRANCH_WIDTH, d), lambda i, b: (b, 0, 0)),
                  pl.BlockSpec((d, d), const2),
                  pl.BlockSpec((tm, d), lambda i, b: (i, 0)),
                  pl.BlockSpec((None, r, d), lambda i, b: (i // gdiv, 0, 0)),
                  pl.BlockSpec((1, d), const2), pl.BlockSpec((1, d), const2)],
        out_specs=pl.BlockSpec((tm, d), lambda i, b: (i, 0)),
        out_shape=jax.ShapeDtypeStruct((m, d), F32),
        scratch_shapes=[pltpu.VMEM((tm, d), F32)],
        compiler_params=_cparams(("arbitrary", "arbitrary")),
        name="merge",
    )(*branches, z, w_branch, w_out, x2d, g1, ln_g, ln_b)


_FHALO = 2 * SUBLANES


def _ffn_kernel(x_ref, xprev_ref, sc_ref, sh_ref, g2_ref, csta_ref, cstv_ref, wa_ref, wv_ref, cw_ref, cb_ref,
                wd_ref, lng_ref, lnb_ref, out_ref, sta_ref, stv_ref, u_scr, hsa, hsv, ya, yv, acc_scr,
                *, nseq, sl, first_div):
    tm = nseq * sl
    halo = _FHALO if nseq == 1 else 0
    i, j = pl.program_id(0), pl.program_id(1)
    nf = pl.num_programs(1)

    @pl.when(j == 0)
    def _():
        sc, sh = 1.0 + sc_ref[...], sh_ref[...]
        if halo:
            u_scr[0:halo, :] = (xprev_ref[...] * sc + sh).astype(BF16)
        u_scr[halo:halo + tm, :] = (x_ref[...] * sc + sh).astype(BF16)
        acc_scr[...] = jnp.zeros(acc_scr.shape, F32)

    u = u_scr[...]
    k = FFN_CONV
    base = _FHALO - (k - 1)
    for w_ref, cst_ref, hs, yo, st_ref, coff in ((wa_ref, csta_ref, hsa, ya, sta_ref, 0),
                                                 (wv_ref, cstv_ref, hsv, yv, stv_ref, 1)):
        hup = _dot(u, w_ref[...])
        cw = cw_ref[coff]
        cb = cb_ref[coff]
        for s in range(nseq):
            if nseq == 1:
                hs[0:_FHALO, :] = hup[0:_FHALO]

                @pl.when(i % first_div == 0)
                def _():
                    hs[_FHALO - _HALO:_FHALO, :] = cst_ref[0]
                hs[_FHALO:_FHALO + sl, :] = hup[_FHALO:_FHALO + sl]
            else:
                hs[_FHALO - _HALO:_FHALO, :] = cst_ref[s]
                hs[_FHALO:_FHALO + sl, :] = hup[s * sl:(s + 1) * sl]
            y = cb
            for t in range(k):
                y = y + cw[t:t + 1, :] * hs[base + t:base + t + sl, :]
            yo[s * sl:(s + 1) * sl, :] = y
            st_ref[s] = hs[_FHALO + sl - _HALO:_FHALO + sl, :]

    act = (_gelu(ya[...]) * yv[...]).astype(BF16)
    acc_scr[...] += _dot(act, wd_ref[...])

    @pl.when(j == nf - 1)
    def _():
        y = DEEPNORM_ALPHA * x_ref[...] + g2_ref[...] * acc_scr[...]
        out_ref[...] = _layer_norm(y, lng_ref[...], lnb_ref[...])


def _ffn(x2d, sc, sh, g2, gdiv, cst, w_up, conv_w, conv_b, w_down, ln_g, ln_b, nseq, sl, first_div, tf):
    m, d = x2d.shape
    tm = nseq * sl
    nblk = m // tm
    nf = D_FF // tf
    r = sc.shape[1]
    halo = _FHALO if nseq == 1 else 0
    const2 = lambda i, j: (0, 0)
    mod_spec = pl.BlockSpec((None, r, d), lambda i, j: (i // gdiv, 0, 0))
    rb = tm // _FHALO
    cw2 = conv_w.reshape(FFN_CONV, 2, D_FF).transpose(1, 0, 2)
    cb2 = conv_b.reshape(2, 1, D_FF)
    csta, cstv = cst[..., :D_FF], cst[..., D_FF:]
    cst_spec = pl.BlockSpec((None, nseq, _HALO, tf), lambda i, j: (i // first_div, 0, 0, j))
    st_spec = pl.BlockSpec((None, nseq, _HALO, tf), lambda i, j: (i, 0, 0, j))
    st_shape = jax.ShapeDtypeStruct((nblk, nseq, _HALO, D_FF), F32)
    kern = functools.partial(_ffn_kernel, nseq=nseq, sl=sl, first_div=first_div)
    return pl.pallas_call(
        kern,
        grid=(nblk, nf),
        in_specs=[pl.BlockSpec((tm, d), lambda i, j: (i, 0)),
                  pl.BlockSpec((_FHALO, d), lambda i, j: (jnp.maximum(i * rb - 1, 0), 0)),
                  mod_spec, mod_spec, mod_spec, cst_spec, cst_spec,
                  pl.BlockSpec((d, tf), lambda i, j: (0, j)),
                  pl.BlockSpec((d, tf), lambda i, j: (0, nf + j)),
                  pl.BlockSpec((2, FFN_CONV, tf), lambda i, j: (0, 0, j)),
                  pl.BlockSpec((2, 1, tf), lambda i, j: (0, 0, j)),
                  pl.BlockSpec((tf, d), lambda i, j: (j, 0)),
                  pl.BlockSpec((1, d), const2), pl.BlockSpec((1, d), const2)],
        out_specs=[pl.BlockSpec((tm, d), lambda i, j: (i, 0)), st_spec, st_spec],
        out_shape=[jax.ShapeDtypeStruct((m, d), F32), st_shape, st_shape],
        scratch_shapes=[pltpu.VMEM((halo + tm, d), BF16),
                        pltpu.VMEM((_FHALO + sl, tf), F32), pltpu.VMEM((_FHALO + sl, tf), F32),
                        pltpu.VMEM((tm, tf), F32), pltpu.VMEM((tm, tf), F32),
                        pltpu.VMEM((tm, d), F32)],
        compiler_params=_cparams(("arbitrary", "arbitrary")),
        name="ffn",
    )(x2d, x2d, sc, sh, g2, csta, cstv, w_up, w_up, cw2, cb2, w_down, ln_g, ln_b)


def _pad_state_rows(buf, rows):
    pad = [(0, 0)] * buf.ndim
    pad[-2] = (rows - buf.shape[-2], 0)
    return jnp.pad(buf, pad)


def _run_layer(x2d, mod, cfg, wts, l, states, paged):
    nb, length = cfg["nb"], cfg["len"]
    m = nb * length
    sh1, sc1, g1, sh2, sc2, g2 = mod
    gdiv = cfg["gdiv"]
    st_gla, st_s5re, st_s5im, st_lru, st_lconv, st_fconv = states

    z, zlr = _inproj(x2d, sc1, sh1, gdiv(cfg["tm_in"]), wts["w_main"], wts["w_lr"], cfg["tm_in"], cfg["tn_in"])

    ng_seq, nseq = cfg["ng"], cfg["nseq"]
    o_gla, gla_new = _gla(z, zlr, _gla_state_pad(st_gla), wts["wa2p"], wts["gla_ba"], wts["gla_norm"],
                          ng_seq, m // ng_seq, cfg["gla_c"], cfg["gla_seq"])
    gla_new = _gla_state_unpad(gla_new)

    sl = cfg["sl"]
    o_s5, s5re_new, s5im_new = _s5(z, st_s5re.reshape(ng_seq, nseq, _S5_CH, LANES),
                                   st_s5im.reshape(ng_seq, nseq, _S5_CH, LANES),
                                   wts["s5"], ng_seq, m // ng_seq, nseq, sl)
    s5re_new = s5re_new.reshape(nb, S5_GROUPS, S5_N)
    s5im_new = s5im_new.reshape(nb, S5_GROUPS, S5_N)

    cst = _pad_state_rows(st_lconv, _HALO).reshape(ng_seq, nseq, _HALO, BRANCH_WIDTH)
    o_lru, lru_new = _lru(z, cst, st_lru.reshape(ng_seq, nseq, BRANCH_WIDTH), wts["lru"],
                          ng_seq, m // ng_seq, nseq, sl)
    lru_new = lru_new.reshape(nb, BRANCH_WIDTH)
    lconv_new = z[:, COL_LX:COL_LX + BRANCH_WIDTH].reshape(nb, length, BRANCH_WIDTH)[:, length - (LRU_CONV - 1):]

    lam_init = 0.8 - 0.6 * math.exp(-0.3 * l)
    q_rot, k_rot = _rope(z, cfg["rope"], nb, length, cfg["tm_rope"])
    v_new = z[:, COL_DV:COL_DV + DA_HEADS * DA_DV]
    if paged is None:
        o_da = _attn_prompt(q_rot, k_rot, z, wts["lam_p"], wts["da_norm"], nb, length,
                            cfg["tq"], cfg["tk"], lam_init)
    else:
        page_table, cache_k, cache_v = paged
        padk = ((0, 0), (0, PAGE_SIZE - length), (0, 0))
        k_pad = jnp.pad(k_rot.reshape(nb, length, -1), padk)
        v_pad = jnp.pad(v_new.reshape(nb, length, -1), padk)
        o_da = _attn_sample(q_rot, cache_k, cache_v, l, page_table, k_pad, v_pad, wts["lam_p"],
                            wts["da_norm"], lam_init)

    x1 = _merge((o_gla, o_s5, o_lru, o_da), z, wts["w_branch"], wts["w_out"], x2d, g1, gdiv(cfg["tm_mg"]),
                wts["ln1_g"], wts["ln1_b"], cfg["tm_mg"])

    fcst = _pad_state_rows(st_fconv, _HALO).reshape(ng_seq, nseq, _HALO, 2 * D_FF)
    fsl = cfg["ffn_sl"]
    x2, sta, stv = _ffn(x1, sc2, sh2, g2, gdiv(nseq * fsl), fcst, wts["w_up"], wts["ffn_conv_w"],
                        wts["ffn_conv_b"], wts["w_down"], wts["ln2_g"], wts["ln2_b"], nseq, fsl,
                        length // fsl if nseq == 1 else 1, cfg["tf"])
    fst = jnp.concatenate([sta, stv], axis=-1)
    if nseq == 1:
        per_seq = length // fsl
        fst = fst.reshape(nb, per_seq, _HALO, 2 * D_FF)[:, per_seq - 1]
    else:
        fst = fst.reshape(nb, _HALO, 2 * D_FF)
    fconv_new = fst[:, _HALO - (FFN_CONV - 1):]

    new = (k_rot.reshape(nb, length, DA_HEADS, 2, DA_DK), v_new.reshape(nb, length, DA_HEADS, DA_DV),
           gla_new, s5re_new, s5im_new, lru_new, lconv_new, fconv_new)
    return x2, new


def _layer_weights(l, w_in, gla_wa2, gla_ba, gla_norm, s5_raw, lru_raw, lam_raw, da_norm, w_branch, w_out,
                   ln1_g, ln1_b, ffn_w_up, ffn_conv_w, ffn_conv_b, ffn_w_down, ln2_g, ln2_b):
    wl = w_in[l]
    n_mix = 1024
    lr0 = n_mix
    rest0 = n_mix + GLA_LOWRANK
    gate0 = wl.shape[1] - N_BRANCH * D_MODEL
    w_main = jnp.concatenate([wl[:, gate0:], wl[:, :n_mix], wl[:, rest0:gate0]], axis=1).astype(BF16)
    w_lr = jnp.pad(wl[:, lr0:rest0], ((0, 0), (0, LANES - GLA_LOWRANK))).astype(BF16)
    wa2p = jnp.pad(gla_wa2[l], ((0, LANES - GLA_LOWRANK), (0, 0))).astype(BF16)
    row = lambda a: a[l].reshape(1, -1)
    cw, cb, wa, ba, wx, bx, lam = [a[l] for a in lru_raw]
    return dict(
        w_main=w_main, w_lr=w_lr, wa2p=wa2p, gla_ba=row(gla_ba), gla_norm=row(gla_norm),
        s5=_s5_params(*[a[l] for a in s5_raw]),
        lru=dict(cw=cw, cb=cb.reshape(1, -1), wa=wa.astype(BF16), ba=ba.reshape(1, -1),
                 wx=wx.astype(BF16), bx=bx.reshape(1, -1), lam=lam.reshape(1, -1)),
        lam_p=tuple(row(a) for a in lam_raw), da_norm=row(da_norm),
        w_branch=w_branch[l].astype(BF16), w_out=w_out[l].astype(BF16),
        ln1_g=row(ln1_g), ln1_b=row(ln1_b),
        w_up=ffn_w_up[l].astype(BF16), ffn_conv_w=ffn_conv_w[l], ffn_conv_b=ffn_conv_b[l],
        w_down=ffn_w_down[l].astype(BF16), ln2_g=row(ln2_g), ln2_b=row(ln2_b))


def kernel(x_prompt, x_sample, cache_k, cache_v, state_gla, state_s5_re, state_s5_im, state_lru, state_lru_conv, state_ffn_conv, page_table, c_prompt, c_sample, w_ada, b_ada, w_in, gla_wa2, gla_ba, gla_norm, s5_a_re, s5_a_im, s5_log_dt, s5_b_re, s5_b_im, s5_c_re, s5_c_im, s5_d, s5_w_glu, s5_b_glu, lru_conv_w, lru_conv_b, lru_w_a, lru_b_a, lru_w_x, lru_b_x, lru_lambda, da_lq1, da_lk1, da_lq2, da_lk2, da_norm, w_branch, w_out, ln1_g, ln1_b, ffn_w_up, ffn_conv_w, ffn_conv_b, ffn_w_down, ln2_g, ln2_b):
    bp, lp, d = x_prompt.shape
    bs, ls, _ = x_sample.shape
    past_len = page_table.shape[1] * PAGE_SIZE

    rows = bp + bs
    rows_pad = -(-rows // SUBLANES) * SUBLANES
    c_all = jnp.pad(jnp.concatenate([c_prompt, c_sample], axis=0), ((0, rows_pad - rows), (0, 0)))
    mod_all = _ada(c_all, w_ada, b_ada)

    def mods(l, lo, n, per_row):
        parts = jnp.split(mod_all[l, lo:lo + n], 6, axis=-1)
        if per_row:
            return [jnp.repeat(p, per_row, axis=0)[None] for p in parts]
        return [p[:, None, :] for p in parts]

    cfg_p = dict(nb=bp, len=lp, ng=bp, nseq=1, sl=256, ffn_sl=512, tm_in=512, tn_in=1280, gla_c=128, gla_seq=128,
                 tm_rope=512, tq=512, tk=512, tm_mg=256, tf=512,
                 gdiv=lambda tm: lp // tm, rope=_rope_tables(0, lp))
    cfg_s = dict(nb=bs, len=ls, ng=1, nseq=bs, sl=ls, ffn_sl=ls, tm_in=bs * ls, tn_in=1280, gla_c=bs * ls,
                 gla_seq=ls, tm_rope=ls, tm_mg=bs * ls, tf=512,
                 gdiv=lambda tm: 1, rope=_rope_tables(past_len, ls))

    zeros_p = (jnp.zeros((bp, GLA_HEADS, GLA_DK, GLA_DV), F32), jnp.zeros((bp, S5_GROUPS, S5_N), F32),
               jnp.zeros((bp, S5_GROUPS, S5_N), F32), jnp.zeros((bp, BRANCH_WIDTH), F32),
               jnp.zeros((bp, LRU_CONV - 1, BRANCH_WIDTH), F32), jnp.zeros((bp, FFN_CONV - 1, 2 * D_FF), F32))

    xp = x_prompt.reshape(bp * lp, d)
    xs = x_sample.reshape(bs * ls, d)
    col_p = [[] for _ in range(8)]
    col_s = [[] for _ in range(8)]
    s5_raw = (s5_a_re, s5_a_im, s5_log_dt, s5_b_re, s5_b_im, s5_c_re, s5_c_im, s5_d, s5_w_glu, s5_b_glu)
    lru_raw = (lru_conv_w, lru_conv_b, lru_w_a, lru_b_a, lru_w_x, lru_b_x, lru_lambda)
    lam_raw = (da_lq1, da_lk1, da_lq2, da_lk2)
    for l in range(DEPTH):
        wts = _layer_weights(l, w_in, gla_wa2, gla_ba, gla_norm, s5_raw, lru_raw, lam_raw, da_norm, w_branch,
                             w_out, ln1_g, ln1_b, ffn_w_up, ffn_conv_w, ffn_conv_b, ffn_w_down, ln2_g, ln2_b)
        xp, new_p = _run_layer(xp, mods(l, 0, bp, 0), cfg_p, wts, l, zeros_p, None)
        st_s = (state_gla[l], state_s5_re[l], state_s5_im[l], state_lru[l], state_lru_conv[l], state_ffn_conv[l])
        xs, new_s = _run_layer(xs, mods(l, bp, bs, ls), cfg_s, wts, l, st_s, (page_table, cache_k, cache_v))
        for lst, s in zip(col_p, new_p):
            lst.append(s)
        for lst, s in zip(col_s, new_s):
            lst.append(s)

    sp = [jnp.stack(s) for s in col_p]
    ss = [jnp.stack(s) for s in col_s]
    out = [xp.reshape(bp, lp, d), xs.reshape(bs, ls, d)]
    for a, b in zip(sp, ss):
        out += [a, b]
    return tuple(out)
```

```python
import functools
import math

import numpy as np
import jax
import jax.numpy as jnp
from jax import lax
from jax.experimental import pallas as pl
from jax.experimental.pallas import tpu as pltpu

F32 = jnp.float32
BF16 = jnp.bfloat16

D_MODEL = 2048
DEPTH = 2
PAGE_SIZE = 128
N_BRANCH = 4
BRANCH_WIDTH = 512
GLA_HEADS = 4
GLA_DK = 64
GLA_DV = 128
GLA_LOWRANK = 16
GLA_TAU = 16.0
S5_GROUP = 16
S5_GROUPS = 32
S5_N = 64
S5_STATES = S5_GROUPS * S5_N
LRU_HEADS = 4
LRU_BLOCK = 128
LRU_CONV = 4
LRU_C = 8.0
DA_HEADS = 4
DA_DK = 64
DA_DV = 128
ROPE_DIM = 16
ROPE_THETA = 500000.0
D_FF = 5632
FFN_CONV = 3
DEEPNORM_ALPHA = (2.0 * DEPTH) ** 0.25
EPS = 1e-5

LANES = 128
SUBLANES = 8
VMEM_LIMIT = 56 * 1024 * 1024

Z_MAIN = 12800
COL_GQ, COL_GK, COL_GV, COL_OG = 8192, 8448, 8704, 9216
COL_SU, COL_LX, COL_LG = 9728, 10240, 10752
COL_DQ, COL_DK, COL_DV = 11264, 11776, 12288

_NT = (((1,), (1,)), ((), ()))
_TN = (((0,), (0,)), ((), ()))


def _cparams(sem):
    return pltpu.CompilerParams(dimension_semantics=sem, vmem_limit_bytes=VMEM_LIMIT)


def _dot(a, b):
    return jnp.dot(a, b, preferred_element_type=F32)


def _sigmoid(x):
    return 0.5 * jnp.tanh(0.5 * x) + 0.5


def _gelu(x):
    return 0.5 * x * (1.0 + jnp.tanh(math.sqrt(2.0 / math.pi) * (x + 0.044715 * (x * x * x))))


def _layer_norm(y, g, b):
    mu = jnp.mean(y, axis=-1, keepdims=True)
    d = y - mu
    var = jnp.mean(d * d, axis=-1, keepdims=True)
    return d * lax.rsqrt(var + EPS) * g + b


def _ada_kernel(c_ref, w_ref, b_ref, o_ref):
    c = c_ref[...]
    s = c * _sigmoid(c)
    o_ref[...] = _dot(s.astype(BF16), w_ref[...].astype(BF16)) + b_ref[...]


def _ada(c_all, w_ada, b_ada):
    rows = c_all.shape[0]
    n = w_ada.shape[-1]
    tn = 1536
    return pl.pallas_call(
        _ada_kernel,
        grid=(DEPTH, n // tn),
        in_specs=[pl.BlockSpec((rows, D_MODEL), lambda l, j: (0, 0)),
                  pl.BlockSpec((None, D_MODEL, tn), lambda l, j: (l, 0, j)),
                  pl.BlockSpec((None, 1, tn), lambda l, j: (l, 0, j))],
        out_specs=pl.BlockSpec((None, rows, tn), lambda l, j: (l, 0, j)),
        out_shape=jax.ShapeDtypeStruct((DEPTH, rows, n), F32),
        compiler_params=_cparams(("arbitrary", "arbitrary")),
        name="ada",
    )(c_all, w_ada, b_ada.reshape(DEPTH, 1, n))


def _inproj_kernel(x_ref, sc_ref, sh_ref, w_ref, wlr_ref, z_ref, zlr_ref, u_scr):
    @pl.when(pl.program_id(1) == 0)
    def _():
        u = (x_ref[...] * (1.0 + sc_ref[...]) + sh_ref[...]).astype(BF16)
        u_scr[...] = u
        zlr_ref[...] = _dot(u, wlr_ref[...])

    z_ref[...] = _dot(u_scr[...], w_ref[...])


def _inproj(x2d, sc, sh, gdiv, w_main, w_lr, tm, tn):
    m = x2d.shape[0]
    r = sc.shape[1]
    mod_spec = pl.BlockSpec((None, r, D_MODEL), lambda i, j: (i // gdiv, 0, 0))
    return pl.pallas_call(
        _inproj_kernel,
        grid=(m // tm, Z_MAIN // tn),
        in_specs=[pl.BlockSpec((tm, D_MODEL), lambda i, j: (i, 0)), mod_spec, mod_spec,
                  pl.BlockSpec((D_MODEL, tn), lambda i, j: (0, j)),
                  pl.BlockSpec((D_MODEL, LANES), lambda i, j: (0, 0))],
        out_specs=[pl.BlockSpec((tm, tn), lambda i, j: (i, j)),
                   pl.BlockSpec((tm, LANES), lambda i, j: (i, 0))],
        out_shape=[jax.ShapeDtypeStruct((m, Z_MAIN), F32), jax.ShapeDtypeStruct((m, LANES), F32)],
        scratch_shapes=[pltpu.VMEM((tm, D_MODEL), BF16)],
        compiler_params=_cparams(("arbitrary", "arbitrary")),
        name="inproj",
    )(x2d, sc, sh, w_main, w_lr)


def _rope_tables(pos0, length):
    half = ROPE_DIM // 2
    inv = ROPE_THETA ** (-jnp.arange(half, dtype=F32) * 2.0 / ROPE_DIM)
    ang = (pos0 + jnp.arange(length)).astype(F32)[:, None] * inv
    cos, sin = jnp.cos(ang), jnp.sin(ang)
    ones = jnp.ones((length, DA_DK - ROPE_DIM), F32)
    zeros = jnp.zeros((length, DA_DK - ROPE_DIM), F32)
    zh = jnp.zeros((length, half), F32)
    c = jnp.concatenate([cos, cos, ones], axis=1)
    s_next = jnp.concatenate([-sin, zh, zeros], axis=1)
    s_prev = jnp.concatenate([zh, sin, zeros], axis=1)
    rep = LANES // DA_DK
    return jnp.tile(c, (1, rep)), jnp.tile(s_next, (1, rep)), jnp.tile(s_prev, (1, rep))


def _rope_kernel(q_ref, k_ref, c_ref, sn_ref, sp_ref, qo_ref, ko_ref):
    half = ROPE_DIM // 2
    c, sn, sp = c_ref[...], sn_ref[...], sp_ref[...]
    for src, dst in ((q_ref, qo_ref), (k_ref, ko_ref)):
        for i in range(src.shape[1] // LANES):
            x = src[:, i * LANES:(i + 1) * LANES]
            nxt = pltpu.roll(x, LANES - half, axis=1)
            prv = pltpu.roll(x, half, axis=1)
            dst[:, i * LANES:(i + 1) * LANES] = x * c + nxt * sn + prv * sp


def _rope(z, tabs, nb, length, tm):
    m = z.shape[0]
    w = DA_HEADS * 2 * DA_DK
    nj = length // tm
    tab_spec = pl.BlockSpec((tm, LANES), lambda b, j: (j, 0))
    return pl.pallas_call(
        _rope_kernel,
        grid=(nb, nj),
        in_specs=[pl.BlockSpec((tm, w), lambda b, j: (b * nj + j, COL_DQ // w)),
                  pl.BlockSpec((tm, w), lambda b, j: (b * nj + j, COL_DK // w)),
                  tab_spec, tab_spec, tab_spec],
        out_specs=[pl.BlockSpec((tm, w), lambda b, j: (b * nj + j, 0))] * 2,
        out_shape=[jax.ShapeDtypeStruct((m, w), F32)] * 2,
        compiler_params=_cparams(("arbitrary", "arbitrary")),
        name="rope",
    )(z, z, *tabs)


def _gla_consts(c, seq):
    nlev = int(math.log2(seq))
    t = np.arange(c)
    tt, rr = t[:, None], t[None, :]
    same = (tt // seq) == (rr // seq)
    tril = same & (rr <= tt)
    seg_end = (t // seq) * seq + seq - 1
    dk = same & (rr > tt) & (rr <= seg_end[:, None])
    mats = [tril.astype(np.float32), dk.astype(np.float32)]
    lvl = np.full((c, c), -1, np.int32)
    lvl[t, t] = nlev
    for i in range(nlev):
        b = seq >> (i + 1)
        mid = (t // (2 * b)) * (2 * b) + b - 1
        mb = same & (rr <= mid[:, None])
        mats.append(tril.astype(np.float32) - mb.astype(np.float32))
        valid = ((tt // (2 * b)) == (rr // (2 * b))) & ((tt % (2 * b)) >= b) & ((rr % (2 * b)) < b)
        lvl[valid] = i
    return np.concatenate(mats, axis=0), lvl, nlev


def _gla_kernel(q_ref, k_ref, v_ref, og_ref, lr_ref, s0_ref, lmat_ref, lvl_ref, wa2_ref, ba_ref, gn_ref,
                o_ref, st_ref, *, c, seq, nlev):
    nb = c // seq
    j = pl.program_id(1)

    @pl.when(j == 0)
    def _():
        st_ref[...] = s0_ref[...]

    x = _dot(lr_ref[...].astype(BF16), wa2_ref[...]) + ba_ref[...]
    la = (jnp.minimum(x, 0.0) - jnp.log1p(jnp.exp(-jnp.abs(x)))) * (1.0 / GLA_TAU)
    e_all = jnp.dot(lmat_ref[...], la, precision=lax.Precision.HIGHEST, preferred_element_type=F32)
    cum, e_end = e_all[0:c], e_all[c:2 * c]
    q = q_ref[...] * (GLA_DK ** -0.5)
    k = k_ref[...]
    lvl = lvl_ref[...]
    lane = lax.broadcasted_iota(jnp.int32, (c, LANES), 1)
    if nb > 1:
        rowb = lax.broadcasted_iota(jnp.int32, (c, nb * LANES), 0) // seq
        colb = lax.broadcasted_iota(jnp.int32, (c, nb * LANES), 1) // LANES
        bmask = rowb == colb

    def spread(a):
        if nb == 1:
            return a
        return jnp.where(bmask, jnp.concatenate([a] * nb, axis=1), 0.0)

    for p in range(2):
        sl = slice(p * LANES, (p + 1) * LANES)
        qp, kp, cump = q[:, sl], k[:, sl], cum[:, sl]
        qts, kts = [], []
        for i in range(nlev):
            eb = e_all[(2 + i) * c:(3 + i) * c, sl]
            qts.append(qp * jnp.exp(eb))
            kts.append((kp * jnp.exp(-eb)).astype(BF16))
        qg = qp * jnp.exp(cump)
        kh = kp * jnp.exp(e_end[:, sl])
        kpb = kp.astype(BF16)
        decays = []
        for b in range(nb):
            cl = cump[b * seq + seq - 1:b * seq + seq, :]
            decays.append(jnp.broadcast_to(jnp.exp(cl), (LANES, LANES)).T)
        for hh in range(2):
            h = 2 * p + hh
            hm = (lane // GLA_DK) == hh
            att = lax.dot_general(jnp.where(hm, qp, 0.0).astype(BF16), kpb, _NT, preferred_element_type=F32)
            att = jnp.where(lvl == nlev, att, 0.0)
            for i in range(nlev):
                a = lax.dot_general(jnp.where(hm, qts[i], 0.0).astype(BF16), kts[i], _NT,
                                    preferred_element_type=F32)
                att = jnp.where(lvl == i, a, att)
            vh = v_ref[:, h * GLA_DV:(h + 1) * GLA_DV].astype(BF16)
            o = _dot(att.astype(BF16), vh)
            s_all = st_ref[:, h].reshape(nb * LANES, GLA_DV)
            o = o + _dot(spread(jnp.where(hm, qg, 0.0)).astype(BF16), s_all.astype(BF16))
            upd = lax.dot_general(spread(jnp.where(hm, kh, 0.0)).astype(BF16), vh, _TN,
                                  preferred_element_type=F32)
            for b in range(nb):
                st_ref[b, h] = st_ref[b, h] * decays[b] + upd[b * LANES:(b + 1) * LANES]
            ms = jnp.mean(o * o, axis=-1, keepdims=True)
            og = og_ref[:, h * GLA_DV:(h + 1) * GLA_DV]
            o_ref[:, h * GLA_DV:(h + 1) * GLA_DV] = o * lax.rsqrt(ms + EPS) * gn_ref[...] * (og * _sigmoid(og))


def _gla(z, zlr, s0_pad, wa2p, ba, gn, ng, length, c, seq):
    m = z.shape[0]
    nb = c // seq
    nj = length // c
    lmat, lvl, nlev = _gla_consts(c, seq)
    w = GLA_HEADS * GLA_DK
    wv = GLA_HEADS * GLA_DV
    row = lambda g, j: g * nj + j
    const2 = lambda g, j: (0, 0)
    st_spec = pl.BlockSpec((nb, GLA_HEADS, LANES, GLA_DV), lambda g, j: (g, 0, 0, 0))
    kern = functools.partial(_gla_kernel, c=c, seq=seq, nlev=nlev)
    return pl.pallas_call(
        kern,
        grid=(ng, nj),
        in_specs=[pl.BlockSpec((c, w), lambda g, j: (row(g, j), COL_GQ // w)),
                  pl.BlockSpec((c, w), lambda g, j: (row(g, j), COL_GK // w)),
                  pl.BlockSpec((c, wv), lambda g, j: (row(g, j), COL_GV // wv)),
                  pl.BlockSpec((c, wv), lambda g, j: (row(g, j), COL_OG // wv)),
                  pl.BlockSpec((c, LANES), lambda g, j: (row(g, j), 0)),
                  st_spec,
                  pl.BlockSpec(lmat.shape, const2),
                  pl.BlockSpec(lvl.shape, const2),
                  pl.BlockSpec((LANES, w), const2),
                  pl.BlockSpec((1, w), const2),
                  pl.BlockSpec((1, GLA_DV), const2)],
        out_specs=[pl.BlockSpec((c, wv), lambda g, j: (row(g, j), 0)), st_spec],
        out_shape=[jax.ShapeDtypeStruct((m, wv), F32),
                   jax.ShapeDtypeStruct(s0_pad.shape, F32)],
        compiler_params=_cparams(("arbitrary", "arbitrary")),
        name="gla",
    )(z, z, z, z, zlr, s0_pad, jnp.asarray(lmat), jnp.asarray(lvl), wa2p, ba, gn)


def _gla_state_pad(s):
    b = s.shape[0]
    s6 = s.reshape(b, 2, 2, 1, GLA_DK, GLA_DV) * jnp.eye(2, dtype=F32)[None, None, :, :, None, None]
    return s6.reshape(b, GLA_HEADS, LANES, GLA_DV)


def _gla_state_unpad(sp):
    b = sp.shape[0]
    s6 = sp.reshape(b, 2, 2, 2, GLA_DK, GLA_DV)
    return jnp.stack([s6[:, :, 0, 0], s6[:, :, 1, 1]], axis=2).reshape(b, GLA_HEADS, GLA_DK, GLA_DV)


_S5_CH = S5_STATES // LANES


def _s5_kernel(u_ref, h0r_ref, h0i_ref, are_ref, aim_ref, ldt_ref, bdr_ref, bdi_ref, cdr_ref, cdi_ref,
               d_ref, wg_ref, bg_ref, y_ref, hfr_ref, hfi_ref, sre, sim, hcr, hci, *, nseq, sl):
    tm = nseq * sl
    j = pl.program_id(1)

    @pl.when(j == 0)
    def _():
        hcr[...] = h0r_ref[...]
        hci[...] = h0i_ref[...]

    lr, li = are_ref[...], aim_ref[...]
    dt = jnp.exp(ldt_ref[...])
    mag = jnp.exp(lr * dt)
    abr, abi = mag * jnp.cos(li * dt), mag * jnp.sin(li * dt)
    den = lr * lr + li * li
    fr = ((abr - 1.0) * lr + abi * li) / den
    fi = (abi * lr - (abr - 1.0) * li) / den

    u = u_ref[...]
    ub = u.astype(BF16)
    for c2 in range(_S5_CH // 2):
        tr = _dot(ub, bdr_ref[:, c2 * 256:(c2 + 1) * 256])
        ti = _dot(ub, bdi_ref[:, c2 * 256:(c2 + 1) * 256])
        for e in range(2):
            ch = 2 * c2 + e
            trc, tic = tr[:, e * LANES:(e + 1) * LANES], ti[:, e * LANES:(e + 1) * LANES]
            frc, fic = fr[ch:ch + 1, :], fi[ch:ch + 1, :]
            sre[ch * tm:(ch + 1) * tm, :] = frc * trc - fic * tic
            sim[ch * tm:(ch + 1) * tm, :] = frc * tic + fic * trc

    ar0, ai0, ar1, ai1 = abr[0:8], abi[0:8], abr[8:16], abi[8:16]
    for s in range(nseq):
        def body(t, carry, s=s):
            r0, i0, r1, i1 = carry
            lo = pl.ds(s * sl + t, SUBLANES, stride=tm)
            hi = pl.ds(SUBLANES * tm + s * sl + t, SUBLANES, stride=tm)
            n_r0 = ar0 * r0 - ai0 * i0 + sre[lo, :]
            n_i0 = ar0 * i0 + ai0 * r0 + sim[lo, :]
            n_r1 = ar1 * r1 - ai1 * i1 + sre[hi, :]
            n_i1 = ar1 * i1 + ai1 * r1 + sim[hi, :]
            sre[lo, :] = n_r0
            sim[lo, :] = n_i0
            sre[hi, :] = n_r1
            sim[hi, :] = n_i1
            return n_r0, n_i0, n_r1, n_i1

        init = (hcr[s, 0:8], hci[s, 0:8], hcr[s, 8:16], hci[s, 8:16])
        r0, i0, r1, i1 = lax.fori_loop(0, sl, body, init, unroll=8)
        hcr[s, 0:8] = r0
        hci[s, 0:8] = i0
        hcr[s, 8:16] = r1
        hci[s, 8:16] = i1

    y = d_ref[...] * u
    for c2 in range(_S5_CH // 2):
        a, b = 2 * c2, 2 * c2 + 1
        hre = jnp.concatenate([sre[a * tm:(a + 1) * tm, :], sre[b * tm:(b + 1) * tm, :]], axis=1).astype(BF16)
        him = jnp.concatenate([sim[a * tm:(a + 1) * tm, :], sim[b * tm:(b + 1) * tm, :]], axis=1).astype(BF16)
        y = y + _dot(hre, cdr_ref[c2 * 256:(c2 + 1) * 256, :]) - _dot(him, cdi_ref[c2 * 256:(c2 + 1) * 256, :])
    y = _gelu(y)
    y_ref[...] = y * _sigmoid(_dot(y.astype(BF16), wg_ref[...]) + bg_ref[...])

    @pl.when(j == pl.num_programs(1) - 1)
    def _():
        hfr_ref[...] = hcr[...]
        hfi_ref[...] = hci[...]


def _s5(z, h0r, h0i, p, ng, length, nseq, sl):
    m = z.shape[0]
    tm = nseq * sl
    nj = length // tm
    w = BRANCH_WIDTH
    const2 = lambda g, j: (0, 0)
    st_spec = pl.BlockSpec((None, nseq, _S5_CH, LANES), lambda g, j: (g, 0, 0, 0))
    vec = pl.BlockSpec((_S5_CH, LANES), const2)
    kern = functools.partial(_s5_kernel, nseq=nseq, sl=sl)
    return pl.pallas_call(
        kern,
        grid=(ng, nj),
        in_specs=[pl.BlockSpec((tm, w), lambda g, j: (g * nj + j, COL_SU // w)),
                  st_spec, st_spec, vec, vec, vec,
                  pl.BlockSpec((w, S5_STATES), const2), pl.BlockSpec((w, S5_STATES), const2),
                  pl.BlockSpec((S5_STATES, w), const2), pl.BlockSpec((S5_STATES, w), const2),
                  pl.BlockSpec((1, w), const2), pl.BlockSpec((w, w), const2), pl.BlockSpec((1, w), const2)],
        out_specs=[pl.BlockSpec((tm, w), lambda g, j: (g * nj + j, 0)), st_spec, st_spec],
        out_shape=[jax.ShapeDtypeStruct((m, w), F32),
                   jax.ShapeDtypeStruct(h0r.shape, F32), jax.ShapeDtypeStruct(h0i.shape, F32)],
        scratch_shapes=[pltpu.VMEM((_S5_CH * tm, LANES), F32), pltpu.VMEM((_S5_CH * tm, LANES), F32),
                        pltpu.VMEM((nseq, _S5_CH, LANES), F32), pltpu.VMEM((nseq, _S5_CH, LANES), F32)],
        compiler_params=_cparams(("arbitrary", "arbitrary")),
        name="s5",
    )(z, h0r, h0i, p["are"], p["aim"], p["ldt"], p["bdr"], p["bdi"], p["cdr"], p["cdi"],
      p["d"], p["wg"], p["bg"])


def _s5_params(a_re, a_im, log_dt, b_re, b_im, c_re, c_im, d, w_glu, b_glu):
    eye = jnp.eye(S5_GROUPS, dtype=F32)

    def bd_in(b):
        return (eye[:, None, :, None] * jnp.swapaxes(b, 1, 2)[:, :, None, :]).reshape(
            S5_GROUPS * S5_GROUP, S5_STATES).astype(BF16)

    def bd_out(c):
        return (eye[:, None, :, None] * jnp.swapaxes(c, 1, 2)[:, :, None, :]).reshape(
            S5_STATES, S5_GROUPS * S5_GROUP).astype(BF16)

    return dict(are=a_re.reshape(_S5_CH, LANES), aim=a_im.reshape(_S5_CH, LANES),
                ldt=jnp.broadcast_to(log_dt[:, None], (S5_GROUPS, S5_N)).reshape(_S5_CH, LANES),
                bdr=bd_in(b_re), bdi=bd_in(b_im), cdr=bd_out(c_re), cdi=bd_out(c_im),
                d=d.reshape(1, -1), wg=w_glu.astype(BF16), bg=b_glu.reshape(1, -1))


_HALO = SUBLANES


def _lru_kernel(lx_ref, lg_ref, zprev_ref, cst_ref, h0_ref, cw_ref, cb_ref, wa_ref, ba_ref, wx_ref, bx_ref,
                lam_ref, y_ref, hfin_ref, cs_scr, xc_scr, a_scr, b_scr, h_c, *, nseq, sl):
    j = pl.program_id(1)

    @pl.when(j == 0)
    def _():
        h_c[...] = h0_ref[...]

    k = LRU_CONV
    for s in range(nseq):
        if nseq == 1:
            cs_scr[0:_HALO, :] = jnp.where(j == 0, cst_ref[0], zprev_ref[...])
        else:
            cs_scr[0:_HALO, :] = cst_ref[s]
        cs_scr[_HALO:_HALO + sl, :] = lx_ref[s * sl:(s + 1) * sl, :]
        xc = cb_ref[...]
        for t in range(k):
            off = _HALO - (k - 1) + t
            xc = xc + cw_ref[t:t + 1, :] * cs_scr[off:off + sl, :]
        xc_scr[s * sl:(s + 1) * sl, :] = xc

    lam = lam_ref[...]
    sp = jnp.maximum(-lam, 0.0) + jnp.log1p(jnp.exp(-jnp.abs(lam)))
    for h in range(LRU_HEADS):
        hs = slice(h * LRU_BLOCK, (h + 1) * LRU_BLOCK)
        xc = xc_scr[:, hs]
        xb = xc.astype(BF16)
        r = _sigmoid(_dot(xb, wa_ref[h]) + ba_ref[:, hs])
        i = _sigmoid(_dot(xb, wx_ref[h]) + bx_ref[:, hs])
        la = -LRU_C * r * sp[:, hs]
        a_scr[:, hs] = jnp.exp(la)
        b_scr[:, hs] = jnp.sqrt(-jnp.tanh(la) * (jnp.exp(2.0 * la) + 1.0)) * (i * xc)

    for s in range(nseq):
        def body(t, h, s=s):
            idx = pl.ds(s * sl + t, 1)
            h = a_scr[idx, :] * h + b_scr[idx, :]
            b_scr[idx, :] = h
            return h

        h_c[s:s + 1, :] = lax.fori_loop(0, sl, body, h_c[s:s + 1, :], unroll=8)

    y_ref[...] = b_scr[...] * _gelu(lg_ref[...])

    @pl.when(j == pl.num_programs(1) - 1)
    def _():
        hfin_ref[...] = h_c[...]


def _lru(z, cst, h0, p, ng, length, nseq, sl):
    m = z.shape[0]
    tm = nseq * sl
    nj = length // tm
    w = BRANCH_WIDTH
    const2 = lambda g, j: (0, 0)
    const3 = lambda g, j: (0, 0, 0)
    st_spec = pl.BlockSpec((None, nseq, w), lambda g, j: (g, 0, 0))
    vec = pl.BlockSpec((1, w), const2)
    wsp = pl.BlockSpec((LRU_HEADS, LRU_BLOCK, LRU_BLOCK), const3)
    rb = tm // _HALO
    kern = functools.partial(_lru_kernel, nseq=nseq, sl=sl)
    return pl.pallas_call(
        kern,
        grid=(ng, nj),
        in_specs=[pl.BlockSpec((tm, w), lambda g, j: (g * nj + j, COL_LX // w)),
                  pl.BlockSpec((tm, w), lambda g, j: (g * nj + j, COL_LG // w)),
                  pl.BlockSpec((_HALO, w), lambda g, j: (jnp.maximum((g * nj + j) * rb - 1, 0), COL_LX // w)),
                  pl.BlockSpec((None, nseq, _HALO, w), lambda g, j: (g, 0, 0, 0)),
                  st_spec,
                  pl.BlockSpec((LRU_CONV, w), const2), vec, wsp, vec, wsp, vec, vec],
        out_specs=[pl.BlockSpec((tm, w), lambda g, j: (g * nj + j, 0)), st_spec],
        out_shape=[jax.ShapeDtypeStruct((m, w), F32), jax.ShapeDtypeStruct(h0.shape, F32)],
        scratch_shapes=[pltpu.VMEM((_HALO + sl, w), F32), pltpu.VMEM((tm, w), F32),
                        pltpu.VMEM((tm, w), F32), pltpu.VMEM((tm, w), F32),
                        pltpu.VMEM((nseq, w), F32)],
        compiler_params=_cparams(("arbitrary", "arbitrary")),
        name="lru",
    )(z, z, z, cst, h0, p["cw"], p["cb"], p["wa"], p["ba"], p["wx"], p["bx"], p["lam"])


def _da_lambda(lq1_ref, lk1_ref, lq2_ref, lk2_ref, lam_init):
    return (jnp.exp(jnp.sum(lq1_ref[...] * lk1_ref[...])) - jnp.exp(jnp.sum(lq2_ref[...] * lk2_ref[...]))
            + lam_init)


def _attn_kernel(q_ref, k_ref, v_ref, lq1_ref, lk1_ref, lq2_ref, lk2_ref, dn_ref, o_ref,
                 m_scr, l_scr, acc_scr, *, tq, tk, lam_init):
    qi, ki = pl.program_id(2), pl.program_id(3)

    @pl.when(ki == 0)
    def _():
        m_scr[...] = jnp.full(m_scr.shape, -jnp.inf, F32)
        l_scr[...] = jnp.zeros(l_scr.shape, F32)
        acc_scr[...] = jnp.zeros(acc_scr.shape, F32)

    def step(masked):
        q = q_ref[...] * (DA_DK ** -0.5)
        kb = k_ref[...].astype(BF16)
        vb = v_ref[...].astype(BF16)
        lane = lax.broadcasted_iota(jnp.int32, (tq, LANES), 1)
        if masked:
            row = qi * tq + lax.broadcasted_iota(jnp.int32, (tq, tk), 0)
            col = ki * tk + lax.broadcasted_iota(jnp.int32, (tq, tk), 1)
            mask = col <= row
        comps = range(2)
        ss = [lax.dot_general(jnp.where((lane // DA_DK) == c, q, 0.0).astype(BF16), kb, _NT,
                              preferred_element_type=F32) for c in comps]
        if masked:
            ss = [jnp.where(mask, s, -jnp.inf) for s in ss]
        m_prev = [m_scr[c] for c in comps]
        l_prev = [l_scr[c] for c in comps]
        a_prev = [acc_scr[c] for c in comps]
        m_new = [jnp.maximum(m_prev[c], jnp.max(ss[c], axis=-1, keepdims=True)) for c in comps]
        alpha = [jnp.exp(m_prev[c] - m_new[c]) for c in comps]
        pexp = [jnp.exp(ss[c] - m_new[c]) for c in comps]
        pv = [_dot(pexp[c].astype(BF16), vb) for c in comps]
        for c in comps:
            l_scr[c] = alpha[c] * l_prev[c] + jnp.sum(pexp[c], axis=-1, keepdims=True)
            acc_scr[c] = alpha[c] * a_prev[c] + pv[c]
            m_scr[c] = m_new[c]

    last_col, first_col = ki * tk + tk - 1, ki * tk
    first_row, last_row = qi * tq, qi * tq + tq - 1

    @pl.when(last_col <= first_row)
    def _():
        step(False)

    @pl.when((last_col > first_row) & (first_col <= last_row))
    def _():
        step(True)

    @pl.when(ki == pl.num_programs(3) - 1)
    def _():
        lam = _da_lambda(lq1_ref, lk1_ref, lq2_ref, lk2_ref, lam_init)
        o = acc_scr[0] / l_scr[0] - lam * (acc_scr[1] / l_scr[1])
        ms = jnp.mean(o * o, axis=-1, keepdims=True)
        o_ref[...] = o * lax.rsqrt(ms + EPS) * dn_ref[...] * (1.0 - lam_init)


def _attn_prompt(q_rot, k_rot, z, lam_p, dn, nb, length, tq, tk, lam_init):
    m = q_rot.shape[0]
    nq, nk = length // tq, length // tk
    const2 = lambda b, h, qi, ki: (0, 0)
    kv_row = lambda b, qi, ki: b * nk + jnp.minimum(ki, (qi * tq + tq - 1) // tk)
    vec = pl.BlockSpec((1, DA_DK), const2)
    kern = functools.partial(_attn_kernel, tq=tq, tk=tk, lam_init=lam_init)
    return pl.pallas_call(
        kern,
        grid=(nb, DA_HEADS, nq, nk),
        in_specs=[pl.BlockSpec((tq, LANES), lambda b, h, qi, ki: (b * nq + qi, h)),
                  pl.BlockSpec((tk, LANES), lambda b, h, qi, ki: (kv_row(b, qi, ki), h)),
                  pl.BlockSpec((tk, DA_DV), lambda b, h, qi, ki: (kv_row(b, qi, ki), COL_DV // DA_DV + h)),
                  vec, vec, vec, vec, pl.BlockSpec((1, DA_DV), const2)],
        out_specs=pl.BlockSpec((tq, DA_DV), lambda b, h, qi, ki: (b * nq + qi, h)),
        out_shape=jax.ShapeDtypeStruct((m, DA_HEADS * DA_DV), F32),
        scratch_shapes=[pltpu.VMEM((2, tq, 1), F32), pltpu.VMEM((2, tq, 1), F32),
                        pltpu.VMEM((2, tq, DA_DV), F32)],
        compiler_params=_cparams(("arbitrary", "arbitrary", "arbitrary", "arbitrary")),
        name="attn_prompt",
    )(q_rot, k_rot, z, *lam_p, dn)


_PAGES_PER_STEP = 8
_QROWS = DA_HEADS * 2 * SUBLANES


def _attn_sample_kernel(pt_ref, q_ref, *refs, lq, lam_init):
    npg = _PAGES_PER_STEP
    kt_refs, v_refs = refs[0:npg], refs[npg:2 * npg]
    kn_ref, vn_ref, lq1_ref, lk1_ref, lq2_ref, lk2_ref, dn_ref = refs[2 * npg:2 * npg + 7]
    o_ref = refs[2 * npg + 7]
    qbd_scr, m_scr, l_scr, acc_scr = refs[2 * npg + 8:]
    w = DA_HEADS * 2 * DA_DK
    hrows = 2 * lq
    j = pl.program_id(1)

    @pl.when(j == 0)
    def _():
        q = q_ref[...] * (DA_DK ** -0.5)
        lane = lax.broadcasted_iota(jnp.int32, (lq, w), 1)
        tiles = [jnp.where((lane // DA_DK) == hc, q, 0.0) for hc in range(DA_HEADS * 2)]
        qbd_scr[...] = jnp.concatenate(tiles, axis=0)
        m_scr[...] = jnp.full(m_scr.shape, -jnp.inf, F32)
        l_scr[...] = jnp.zeros(l_scr.shape, F32)
        acc_scr[...] = jnp.zeros(acc_scr.shape, F32)

    qbd = qbd_scr[...].astype(BF16)

    def update(s, nchunk, vget):
        m_prev = m_scr[...]
        m_new = jnp.maximum(m_prev, jnp.max(s, axis=-1, keepdims=True))
        alpha = jnp.exp(m_prev - m_new)
        pexp = jnp.exp(s - m_new)
        l_scr[...] = alpha * l_scr[...] + jnp.sum(pexp, axis=-1, keepdims=True)
        pb = pexp.astype(BF16)
        for h in range(DA_HEADS):
            rs = slice(h * hrows, (h + 1) * hrows)
            pv = _dot(pb[rs, 0:PAGE_SIZE], vget(0, h))
            for i in range(1, nchunk):
                pv = pv + _dot(pb[rs, i * PAGE_SIZE:(i + 1) * PAGE_SIZE], vget(i, h))
            acc_scr[rs, :] = alpha[rs] * acc_scr[rs, :] + pv
        m_scr[...] = m_new

    s_pages = jnp.concatenate([_dot(qbd, kt_refs[i][...].astype(BF16)) for i in range(npg)], axis=1)
    update(s_pages, npg, lambda i, h: v_refs[i][pl.ds(h, PAGE_SIZE, stride=DA_HEADS), :].astype(BF16))

    @pl.when(j == pl.num_programs(1) - 1)
    def _():
        nk = kn_ref.shape[0]
        t = lax.broadcasted_iota(jnp.int32, (_QROWS, nk), 0) % lq
        col = lax.broadcasted_iota(jnp.int32, (_QROWS, nk), 1)
        s_new = lax.dot_general(qbd, kn_ref[...].astype(BF16), _NT, preferred_element_type=F32)
        update(jnp.where(col <= t, s_new, -jnp.inf), 1,
               lambda i, h: vn_ref[:, h * DA_DV:(h + 1) * DA_DV].astype(BF16))
        lam = _da_lambda(lq1_ref, lk1_ref, lq2_ref, lk2_ref, lam_init)
        for h in range(DA_HEADS):
            r0 = h * hrows
            o0 = acc_scr[r0:r0 + lq, :] / l_scr[r0:r0 + lq]
            o1 = acc_scr[r0 + lq:r0 + 2 * lq, :] / l_scr[r0 + lq:r0 + 2 * lq]
            o = o0 - lam * o1
            ms = jnp.mean(o * o, axis=-1, keepdims=True)
            o_ref[:, h * DA_DV:(h + 1) * DA_DV] = o * lax.rsqrt(ms + EPS) * dn_ref[...] * (1.0 - lam_init)


def _attn_sample(q_rot, cache_k, cache_v, layer, page_table, k_new, v_new, lam_p, dn, lam_init):
    nb, n_pages = page_table.shape
    lq = q_rot.shape[0] // nb
    w = DA_HEADS * 2 * DA_DK
    wv = DA_HEADS * DA_DV
    npg = _PAGES_PER_STEP
    nsteps = n_pages // npg
    ckt = jnp.transpose(cache_k, (0, 1, 3, 4, 5, 2)).reshape(DEPTH, -1, w, PAGE_SIZE)
    cv2 = cache_v.reshape(DEPTH, -1, PAGE_SIZE * DA_HEADS, DA_DV)
    const2 = lambda b, j, pt: (0, 0)
    page = lambda b, j, pt, i: pt[b * n_pages + j * npg + i]

    def k_spec(i):
        return pl.BlockSpec((None, None, w, PAGE_SIZE), lambda b, j, pt: (layer, page(b, j, pt, i), 0, 0))

    def v_spec(i):
        return pl.BlockSpec((None, None, PAGE_SIZE * DA_HEADS, DA_DV),
                            lambda b, j, pt: (layer, page(b, j, pt, i), 0, 0))

    vec = pl.BlockSpec((1, DA_DK), const2)
    nk = k_new.shape[1]
    grid_spec = pltpu.PrefetchScalarGridSpec(
        num_scalar_prefetch=1,
        grid=(nb, nsteps),
        in_specs=([pl.BlockSpec((lq, w), lambda b, j, pt: (b, 0))]
                  + [k_spec(i) for i in range(npg)] + [v_spec(i) for i in range(npg)]
                  + [pl.BlockSpec((None, nk, w), lambda b, j, pt: (b, 0, 0)),
                     pl.BlockSpec((None, nk, wv), lambda b, j, pt: (b, 0, 0)),
                     vec, vec, vec, vec, pl.BlockSpec((1, DA_DV), const2)]),
        out_specs=pl.BlockSpec((lq, wv), lambda b, j, pt: (b, 0)),
        scratch_shapes=[pltpu.VMEM((_QROWS, w), F32), pltpu.VMEM((_QROWS, 1), F32),
                        pltpu.VMEM((_QROWS, 1), F32), pltpu.VMEM((_QROWS, DA_DV), F32)],
    )
    kern = functools.partial(_attn_sample_kernel, lq=lq, lam_init=lam_init)
    return pl.pallas_call(
        kern,
        grid_spec=grid_spec,
        out_shape=jax.ShapeDtypeStruct((nb * lq, wv), F32),
        compiler_params=_cparams(("arbitrary", "arbitrary")),
        name="attn_sample",
    )(page_table.reshape(-1), q_rot, *([ckt] * npg), *([cv2] * npg), k_new, v_new, *lam_p, dn)


def _merge_kernel(o0_ref, o1_ref, o2_ref, o3_ref, g_ref, wb_ref, wo_ref, x_ref, g1_ref, lng_ref, lnb_ref,
                  out_ref, acc_scr):
    b = pl.program_id(1)

    @pl.when(b == 0)
    def _():
        acc_scr[...] = jnp.zeros(acc_scr.shape, F32)

    o = jnp.where(b == 0, o0_ref[...], jnp.where(b == 1, o1_ref[...], jnp.where(b == 2, o2_ref[...], o3_ref[...])))
    acc_scr[...] += _sigmoid(g_ref[...]) * _dot(o.astype(BF16), wb_ref[...])

    @pl.when(b == N_BRANCH - 1)
    def _():
        mix = _dot(acc_scr[...].astype(BF16), wo_ref[...])
        y = DEEPNORM_ALPHA * x_ref[...] + g1_ref[...] * mix
        out_ref[...] = _layer_norm(y, lng_ref[...], lnb_ref[...])


def _merge(branches, z, w_branch, w_out, x2d, g1, gdiv, ln_g, ln_b, tm):
    m = x2d.shape[0]
    d = D_MODEL
    r = g1.shape[1]
    const2 = lambda i, b: (0, 0)
    osp = pl.BlockSpec((tm, BRANCH_WIDTH), lambda i, b: (i, 0))
    return pl.pallas_call(
        _merge_kernel,
        grid=(m // tm, N_BRANCH),
        in_specs=[osp, osp, osp, osp,
                  pl.BlockSpec((tm, d), lambda i, b: (i, b)),
                  pl.BlockSpec((None, BRANCH_WIDTH, d), lambda i, b: (b, 0, 0)),
                  pl.BlockSpec((d, d), const2, pipeline_mode=pl.Buffered(1)),
                  pl.BlockSpec((tm, d), lambda i, b: (i, 0)),
                  pl.BlockSpec((None, r, d), lambda i, b: (i // gdiv, 0, 0)),
                  pl.BlockSpec((1, d), const2), pl.BlockSpec((1, d), const2)],
        out_specs=pl.BlockSpec((tm, d), lambda i, b: (i, 0)),
        out_shape=jax.ShapeDtypeStruct((m, d), F32),
        scratch_shapes=[pltpu.VMEM((tm, d), F32)],
        compiler_params=_cparams(("arbitrary", "arbitrary")),
        name="merge",
    )(*branches, z, w_branch, w_out, x2d, g1, ln_g, ln_b)


_FHALO = 2 * SUBLANES


def _ffn_kernel(x_ref, xprev_ref, sc_ref, sh_ref, g2_ref, csta_ref, cstv_ref, wa_ref, wv_ref, cw_ref, cb_ref,
                wd_ref, lng_ref, lnb_ref, out_ref, sta_ref, stv_ref, u_scr, hsa, hsv, act_scr, acc_scr,
                *, nseq, sl, first_div, rc):
    tm = nseq * sl
    halo = _FHALO if nseq == 1 else 0
    seg = _FHALO + sl
    i, j = pl.program_id(0), pl.program_id(1)
    nf = pl.num_programs(1)

    @pl.when(j == 0)
    def _():
        sc, sh = 1.0 + sc_ref[...], sh_ref[...]
        if halo:
            u_scr[0:halo, :] = (xprev_ref[...] * sc + sh).astype(BF16)
        u_scr[halo:halo + tm, :] = (x_ref[...] * sc + sh).astype(BF16)
        acc_scr[...] = jnp.zeros(acc_scr.shape, F32)

    parts = ((wa_ref, csta_ref, hsa, sta_ref), (wv_ref, cstv_ref, hsv, stv_ref))
    if nseq == 1:
        half = tm // 2
        spans = ((0, half), (half, tm))
        first = i % first_div == 0
        for lo, hi in ((0, _FHALO + half), (_FHALO + half, _FHALO + tm)):
            for w_ref, cst_ref, hs, _ in parts:
                hs[lo:hi, :] = _dot(u_scr[lo:hi, :], w_ref[...])
                if lo == 0:
                    hs[_FHALO - _HALO:_FHALO, :] = jnp.where(first, cst_ref[0], hs[_FHALO - _HALO:_FHALO, :])
    else:
        spans = ((0, tm),)
        u = u_scr[...]
        for w_ref, cst_ref, hs, _ in parts:
            hup = _dot(u, w_ref[...])
            for s in range(nseq):
                hs[s * seg + _FHALO - _HALO:s * seg + _FHALO, :] = cst_ref[s]
                hs[s * seg + _FHALO:(s + 1) * seg, :] = hup[s * sl:(s + 1) * sl]
    for _, _, hs, st_ref in parts:
        for s in range(nseq):
            st_ref[s] = hs[(s + 1) * seg - _HALO:(s + 1) * seg, :]

    k = FFN_CONV
    base = _FHALO - (k - 1)
    for lo, hi in spans:
        for r0 in range(lo, hi, rc):
            s, t0 = divmod(r0, sl)
            o = s * seg + base + t0
            ya, yv = cb_ref[0], cb_ref[1]
            for t in range(k):
                ya = ya + cw_ref[0, t:t + 1, :] * hsa[o + t:o + t + rc, :]
                yv = yv + cw_ref[1, t:t + 1, :] * hsv[o + t:o + t + rc, :]
            act_scr[r0:r0 + rc, :] = (_gelu(ya) * yv).astype(act_scr.dtype)
        acc_scr[lo:hi, :] += _dot(act_scr[lo:hi, :].astype(BF16), wd_ref[...])

    @pl.when(j == nf - 1)
    def _():
        y = DEEPNORM_ALPHA * x_ref[...] + g2_ref[...] * acc_scr[...]
        out_ref[...] = _layer_norm(y, lng_ref[...], lnb_ref[...])


def _ffn(x2d, sc, sh, g2, gdiv, cst, w_up, conv_w, conv_b, w_down, ln_g, ln_b, nseq, sl, first_div, tf):
    m, d = x2d.shape
    tm = nseq * sl
    nblk = m // tm
    nf = D_FF // tf
    r = sc.shape[1]
    halo = _FHALO if nseq == 1 else 0
    const2 = lambda i, j: (0, 0)
    mod_spec = pl.BlockSpec((None, r, d), lambda i, j: (i // gdiv, 0, 0))
    rb = tm // _FHALO
    cw2 = conv_w.reshape(FFN_CONV, 2, D_FF).transpose(1, 0, 2)
    cb2 = conv_b.reshape(2, 1, D_FF)
    csta, cstv = cst[..., :D_FF], cst[..., D_FF:]
    cst_spec = pl.BlockSpec((None, nseq, _HALO, tf), lambda i, j: (i // first_div, 0, 0, j))
    st_spec = pl.BlockSpec((None, nseq, _HALO, tf), lambda i, j: (i, 0, 0, j))
    st_shape = jax.ShapeDtypeStruct((nblk, nseq, _HALO, D_FF), F32)
    rc = 32 if sl % 32 == 0 else sl
    act_dtype = BF16 if rc % (2 * SUBLANES) == 0 else F32
    kern = functools.partial(_ffn_kernel, nseq=nseq, sl=sl, first_div=first_div, rc=rc)
    return pl.pallas_call(
        kern,
        grid=(nblk, nf),
        in_specs=[pl.BlockSpec((tm, d), lambda i, j: (i, 0)),
                  pl.BlockSpec((_FHALO, d), lambda i, j: (jnp.maximum(i * rb - 1, 0), 0)),
                  mod_spec, mod_spec, mod_spec, cst_spec, cst_spec,
                  pl.BlockSpec((d, tf), lambda i, j: (0, j)),
                  pl.BlockSpec((d, tf), lambda i, j: (0, nf + j)),
                  pl.BlockSpec((2, FFN_CONV, tf), lambda i, j: (0, 0, j)),
                  pl.BlockSpec((2, 1, tf), lambda i, j: (0, 0, j)),
                  pl.BlockSpec((tf, d), lambda i, j: (j, 0)),
                  pl.BlockSpec((1, d), const2), pl.BlockSpec((1, d), const2)],
        out_specs=[pl.BlockSpec((tm, d), lambda i, j: (i, 0)), st_spec, st_spec],
        out_shape=[jax.ShapeDtypeStruct((m, d), F32), st_shape, st_shape],
        scratch_shapes=[pltpu.VMEM((halo + tm, d), BF16),
                        pltpu.VMEM((nseq * (_FHALO + sl), tf), F32), pltpu.VMEM((nseq * (_FHALO + sl), tf), F32),
                        pltpu.VMEM((tm, tf), act_dtype),
                        pltpu.VMEM((tm, d), F32)],
        compiler_params=_cparams(("arbitrary", "arbitrary")),
        name="ffn",
    )(x2d, x2d, sc, sh, g2, csta, cstv, w_up, w_up, cw2, cb2, w_down, ln_g, ln_b)


def _pad_state_rows(buf, rows):
    pad = [(0, 0)] * buf.ndim
    pad[-2] = (rows - buf.shape[-2], 0)
    return jnp.pad(buf, pad)


def _run_layer(x2d, mod, cfg, wts, l, states, paged):
    nb, length = cfg["nb"], cfg["len"]
    m = nb * length
    sh1, sc1, g1, sh2, sc2, g2 = mod
    gdiv = cfg["gdiv"]
    st_gla, st_s5re, st_s5im, st_lru, st_lconv, st_fconv = states

    z, zlr = _inproj(x2d, sc1, sh1, gdiv(cfg["tm_in"]), wts["w_main"], wts["w_lr"], cfg["tm_in"], cfg["tn_in"])

    ng_seq, nseq = cfg["ng"], cfg["nseq"]
    o_gla, gla_new = _gla(z, zlr, _gla_state_pad(st_gla), wts["wa2p"], wts["gla_ba"], wts["gla_norm"],
                          ng_seq, m // ng_seq, cfg["gla_c"], cfg["gla_seq"])
    gla_new = _gla_state_unpad(gla_new)

    sl = cfg["sl"]
    o_s5, s5re_new, s5im_new = _s5(z, st_s5re.reshape(ng_seq, nseq, _S5_CH, LANES),
                                   st_s5im.reshape(ng_seq, nseq, _S5_CH, LANES),
                                   wts["s5"], ng_seq, m // ng_seq, nseq, sl)
    s5re_new = s5re_new.reshape(nb, S5_GROUPS, S5_N)
    s5im_new = s5im_new.reshape(nb, S5_GROUPS, S5_N)

    cst = _pad_state_rows(st_lconv, _HALO).reshape(ng_seq, nseq, _HALO, BRANCH_WIDTH)
    o_lru, lru_new = _lru(z, cst, st_lru.reshape(ng_seq, nseq, BRANCH_WIDTH), wts["lru"],
                          ng_seq, m // ng_seq, nseq, sl)
    lru_new = lru_new.reshape(nb, BRANCH_WIDTH)
    lconv_new = z[:, COL_LX:COL_LX + BRANCH_WIDTH].reshape(nb, length, BRANCH_WIDTH)[:, length - (LRU_CONV - 1):]

    lam_init = 0.8 - 0.6 * math.exp(-0.3 * l)
    q_rot, k_rot = _rope(z, cfg["rope"], nb, length, cfg["tm_rope"])
    v_new = z[:, COL_DV:COL_DV + DA_HEADS * DA_DV]
    if paged is None:
        o_da = _attn_prompt(q_rot, k_rot, z, wts["lam_p"], wts["da_norm"], nb, length,
                            cfg["tq"], cfg["tk"], lam_init)
    else:
        page_table, cache_k, cache_v = paged
        padk = ((0, 0), (0, PAGE_SIZE - length), (0, 0))
        k_pad = jnp.pad(k_rot.reshape(nb, length, -1), padk)
        v_pad = jnp.pad(v_new.reshape(nb, length, -1), padk)
        o_da = _attn_sample(q_rot, cache_k, cache_v, l, page_table, k_pad, v_pad, wts["lam_p"],
                            wts["da_norm"], lam_init)

    x1 = _merge((o_gla, o_s5, o_lru, o_da), z, wts["w_branch"], wts["w_out"], x2d, g1, gdiv(cfg["tm_mg"]),
                wts["ln1_g"], wts["ln1_b"], cfg["tm_mg"])

    fcst = _pad_state_rows(st_fconv, _HALO).reshape(ng_seq, nseq, _HALO, 2 * D_FF)
    fsl = cfg["ffn_sl"]
    x2, sta, stv = _ffn(x1, sc2, sh2, g2, gdiv(nseq * fsl), fcst, wts["w_up"], wts["ffn_conv_w"],
                        wts["ffn_conv_b"], wts["w_down"], wts["ln2_g"], wts["ln2_b"], nseq, fsl,
                        length // fsl if nseq == 1 else 1, cfg["tf"])
    fst = jnp.concatenate([sta, stv], axis=-1)
    if nseq == 1:
        per_seq = length // fsl
        fst = fst.reshape(nb, per_seq, _HALO, 2 * D_FF)[:, per_seq - 1]
    else:
        fst = fst.reshape(nb, _HALO, 2 * D_FF)
    fconv_new = fst[:, _HALO - (FFN_CONV - 1):]

    new = (k_rot.reshape(nb, length, DA_HEADS, 2, DA_DK), v_new.reshape(nb, length, DA_HEADS, DA_DV),
           gla_new, s5re_new, s5im_new, lru_new, lconv_new, fconv_new)
    return x2, new


def _layer_weights(l, w_in, gla_wa2, gla_ba, gla_norm, s5_raw, lru_raw, lam_raw, da_norm, w_branch, w_out,
                   ln1_g, ln1_b, ffn_w_up, ffn_conv_w, ffn_conv_b, ffn_w_down, ln2_g, ln2_b):
    wl = w_in[l]
    n_mix = 1024
    lr0 = n_mix
    rest0 = n_mix + GLA_LOWRANK
    gate0 = wl.shape[1] - N_BRANCH * D_MODEL
    w_main = jnp.concatenate([wl[:, gate0:], wl[:, :n_mix], wl[:, rest0:gate0]], axis=1).astype(BF16)
    w_lr = jnp.pad(wl[:, lr0:rest0], ((0, 0), (0, LANES - GLA_LOWRANK))).astype(BF16)
    wa2p = jnp.pad(gla_wa2[l], ((0, LANES - GLA_LOWRANK), (0, 0))).astype(BF16)
    row = lambda a: a[l].reshape(1, -1)
    cw, cb, wa, ba, wx, bx, lam = [a[l] for a in lru_raw]
    return dict(
        w_main=w_main, w_lr=w_lr, wa2p=wa2p, gla_ba=row(gla_ba), gla_norm=row(gla_norm),
        s5=_s5_params(*[a[l] for a in s5_raw]),
        lru=dict(cw=cw, cb=cb.reshape(1, -1), wa=wa.astype(BF16), ba=ba.reshape(1, -1),
                 wx=wx.astype(BF16), bx=bx.reshape(1, -1), lam=lam.reshape(1, -1)),
        lam_p=tuple(row(a) for a in lam_raw), da_norm=row(da_norm),
        w_branch=w_branch[l].astype(BF16), w_out=w_out[l].astype(BF16),
        ln1_g=row(ln1_g), ln1_b=row(ln1_b),
        w_up=ffn_w_up[l].astype(BF16), ffn_conv_w=ffn_conv_w[l], ffn_conv_b=ffn_conv_b[l],
        w_down=ffn_w_down[l].astype(BF16), ln2_g=row(ln2_g), ln2_b=row(ln2_b))


def kernel(x_prompt, x_sample, cache_k, cache_v, state_gla, state_s5_re, state_s5_im, state_lru, state_lru_conv, state_ffn_conv, page_table, c_prompt, c_sample, w_ada, b_ada, w_in, gla_wa2, gla_ba, gla_norm, s5_a_re, s5_a_im, s5_log_dt, s5_b_re, s5_b_im, s5_c_re, s5_c_im, s5_d, s5_w_glu, s5_b_glu, lru_conv_w, lru_conv_b, lru_w_a, lru_b_a, lru_w_x, lru_b_x, lru_lambda, da_lq1, da_lk1, da_lq2, da_lk2, da_norm, w_branch, w_out, ln1_g, ln1_b, ffn_w_up, ffn_conv_w, ffn_conv_b, ffn_w_down, ln2_g, ln2_b):
    bp, lp, d = x_prompt.shape
    bs, ls, _ = x_sample.shape
    past_len = page_table.shape[1] * PAGE_SIZE

    rows = bp + bs
    rows_pad = -(-rows // SUBLANES) * SUBLANES
    c_all = jnp.pad(jnp.concatenate([c_prompt, c_sample], axis=0), ((0, rows_pad - rows), (0, 0)))
    mod_all = _ada(c_all, w_ada, b_ada)

    def mods(l, lo, n, per_row):
        parts = jnp.split(mod_all[l, lo:lo + n], 6, axis=-1)
        if per_row:
            return [jnp.repeat(p, per_row, axis=0)[None] for p in parts]
        return [p[:, None, :] for p in parts]

    cfg_p = dict(nb=bp, len=lp, ng=bp, nseq=1, sl=256, ffn_sl=512, tm_in=1024, tn_in=640, gla_c=128, gla_seq=128,
                 tm_rope=512, tq=1024, tk=512, tm_mg=512, tf=512,
                 gdiv=lambda tm: lp // tm, rope=_rope_tables(0, lp))
    cfg_s = dict(nb=bs, len=ls, ng=1, nseq=bs, sl=ls, ffn_sl=ls, tm_in=bs * ls, tn_in=1280, gla_c=bs * ls,
                 gla_seq=ls, tm_rope=ls, tm_mg=bs * ls, tf=512,
                 gdiv=lambda tm: 1, rope=_rope_tables(past_len, ls))

    zeros_p = (jnp.zeros((bp, GLA_HEADS, GLA_DK, GLA_DV), F32), jnp.zeros((bp, S5_GROUPS, S5_N), F32),
               jnp.zeros((bp, S5_GROUPS, S5_N), F32), jnp.zeros((bp, BRANCH_WIDTH), F32),
               jnp.zeros((bp, LRU_CONV - 1, BRANCH_WIDTH), F32), jnp.zeros((bp, FFN_CONV - 1, 2 * D_FF), F32))

    xp = x_prompt.reshape(bp * lp, d)
    xs = x_sample.reshape(bs * ls, d)
    col_p = [[] for _ in range(8)]
    col_s = [[] for _ in range(8)]
    s5_raw = (s5_a_re, s5_a_im, s5_log_dt, s5_b_re, s5_b_im, s5_c_re, s5_c_im, s5_d, s5_w_glu, s5_b_glu)
    lru_raw = (lru_conv_w, lru_conv_b, lru_w_a, lru_b_a, lru_w_x, lru_b_x, lru_lambda)
    lam_raw = (da_lq1, da_lk1, da_lq2, da_lk2)
    for l in range(DEPTH):
        wts = _layer_weights(l, w_in, gla_wa2, gla_ba, gla_norm, s5_raw, lru_raw, lam_raw, da_norm, w_branch,
                             w_out, ln1_g, ln1_b, ffn_w_up, ffn_conv_w, ffn_conv_b, ffn_w_down, ln2_g, ln2_b)
        xp, new_p = _run_layer(xp, mods(l, 0, bp, 0), cfg_p, wts, l, zeros_p, None)
        st_s = (state_gla[l], state_s5_re[l], state_s5_im[l], state_lru[l], state_lru_conv[l], state_ffn_conv[l])
        xs, new_s = _run_layer(xs, mods(l, bp, bs, ls), cfg_s, wts, l, st_s, (page_table, cache_k, cache_v))
        for lst, s in zip(col_p, new_p):
            lst.append(s)
        for lst, s in zip(col_s, new_s):
            lst.append(s)

    sp = [jnp.stack(s) for s in col_p]
    ss = [jnp.stack(s) for s in col_s]
    out = [xp.reshape(bp, lp, d), xs.reshape(bs, ls, d)]
    for a, b in zip(sp, ss):
        out += [a, b]
    return tuple(out)
```

```python
import functools
import math

import numpy as np
import jax
import jax.numpy as jnp
from jax import lax
from jax.experimental import pallas as pl
from jax.experimental.pallas import tpu as pltpu

F32 = jnp.float32
BF16 = jnp.bfloat16

D_MODEL = 2048
DEPTH = 2
PAGE_SIZE = 128
N_BRANCH = 4
BRANCH_WIDTH = 512
GLA_HEADS = 4
GLA_DK = 64
GLA_DV = 128
GLA_LOWRANK = 16
GLA_TAU = 16.0
S5_GROUP = 16
S5_GROUPS = 32
S5_N = 64
S5_STATES = S5_GROUPS * S5_N
LRU_HEADS = 4
LRU_BLOCK = 128
LRU_CONV = 4
LRU_C = 8.0
DA_HEADS = 4
DA_DK = 64
DA_DV = 128
ROPE_DIM = 16
ROPE_THETA = 500000.0
D_FF = 5632
FFN_CONV = 3
DEEPNORM_ALPHA = (2.0 * DEPTH) ** 0.25
EPS = 1e-5

LANES = 128
SUBLANES = 8
VMEM_LIMIT = 56 * 1024 * 1024

Z_MAIN = 12800
COL_GQ, COL_GK, COL_GV, COL_OG = 8192, 8448, 8704, 9216
COL_SU, COL_LX, COL_LG = 9728, 10240, 10752
COL_DQ, COL_DK, COL_DV = 11264, 11776, 12288

_NT = (((1,), (1,)), ((), ()))
_TN = (((0,), (0,)), ((), ()))


def _cparams(sem):
    return pltpu.CompilerParams(dimension_semantics=sem, vmem_limit_bytes=VMEM_LIMIT)


def _dot(a, b):
    return jnp.dot(a, b, preferred_element_type=F32)


def _sigmoid(x):
    return 0.5 * jnp.tanh(0.5 * x) + 0.5


def _gelu(x):
    return 0.5 * x * (1.0 + jnp.tanh(math.sqrt(2.0 / math.pi) * (x + 0.044715 * (x * x * x))))


def _layer_norm(y, g, b):
    mu = jnp.mean(y, axis=-1, keepdims=True)
    d = y - mu
    var = jnp.mean(d * d, axis=-1, keepdims=True)
    return d * lax.rsqrt(var + EPS) * g + b


def _ada_kernel(c_ref, w_ref, b_ref, o_ref):
    c = c_ref[...]
    s = c * _sigmoid(c)
    o_ref[...] = _dot(s.astype(BF16), w_ref[...].astype(BF16)) + b_ref[...]


def _ada(c_all, w_ada, b_ada):
    rows = c_all.shape[0]
    n = w_ada.shape[-1]
    tn = 1536
    return pl.pallas_call(
        _ada_kernel,
        grid=(DEPTH, n // tn),
        in_specs=[pl.BlockSpec((rows, D_MODEL), lambda l, j: (0, 0)),
                  pl.BlockSpec((None, D_MODEL, tn), lambda l, j: (l, 0, j)),
                  pl.BlockSpec((None, 1, tn), lambda l, j: (l, 0, j))],
        out_specs=pl.BlockSpec((None, rows, tn), lambda l, j: (l, 0, j)),
        out_shape=jax.ShapeDtypeStruct((DEPTH, rows, n), F32),
        compiler_params=_cparams(("arbitrary", "arbitrary")),
        name="ada",
    )(c_all, w_ada, b_ada.reshape(DEPTH, 1, n))


def _inproj_kernel(x_ref, sc_ref, sh_ref, w_ref, wlr_ref, z_ref, zlr_ref, u_scr):
    @pl.when(pl.program_id(1) == 0)
    def _():
        u = (x_ref[...] * (1.0 + sc_ref[...]) + sh_ref[...]).astype(BF16)
        u_scr[...] = u
        zlr_ref[...] = _dot(u, wlr_ref[...])

    z_ref[...] = _dot(u_scr[...], w_ref[...])


def _inproj(x2d, sc, sh, gdiv, w_main, w_lr, layer, tm, tn):
    m = x2d.shape[0]
    r = sc.shape[1]
    mod_spec = pl.BlockSpec((None, r, D_MODEL), lambda i, j: (i // gdiv, 0, 0))
    return pl.pallas_call(
        _inproj_kernel,
        grid=(m // tm, Z_MAIN // tn),
        in_specs=[pl.BlockSpec((tm, D_MODEL), lambda i, j: (i, 0)), mod_spec, mod_spec,
                  pl.BlockSpec((None, D_MODEL, tn), lambda i, j: (layer, 0, j)),
                  pl.BlockSpec((None, D_MODEL, LANES), lambda i, j: (layer, 0, 0))],
        out_specs=[pl.BlockSpec((tm, tn), lambda i, j: (i, j)),
                   pl.BlockSpec((tm, LANES), lambda i, j: (i, 0))],
        out_shape=[jax.ShapeDtypeStruct((m, Z_MAIN), F32), jax.ShapeDtypeStruct((m, LANES), F32)],
        scratch_shapes=[pltpu.VMEM((tm, D_MODEL), BF16)],
        compiler_params=_cparams(("arbitrary", "arbitrary")),
        name="inproj",
    )(x2d, sc, sh, w_main, w_lr)


def _rope_tables(pos0, length):
    half = ROPE_DIM // 2
    inv = ROPE_THETA ** (-jnp.arange(half, dtype=F32) * 2.0 / ROPE_DIM)
    ang = (pos0 + jnp.arange(length)).astype(F32)[:, None] * inv
    cos, sin = jnp.cos(ang), jnp.sin(ang)
    ones = jnp.ones((length, DA_DK - ROPE_DIM), F32)
    zeros = jnp.zeros((length, DA_DK - ROPE_DIM), F32)
    zh = jnp.zeros((length, half), F32)
    c = jnp.concatenate([cos, cos, ones], axis=1)
    s_next = jnp.concatenate([-sin, zh, zeros], axis=1)
    s_prev = jnp.concatenate([zh, sin, zeros], axis=1)
    rep = LANES // DA_DK
    return jnp.tile(c, (1, rep)), jnp.tile(s_next, (1, rep)), jnp.tile(s_prev, (1, rep))


def _rope_kernel(q_ref, k_ref, c_ref, sn_ref, sp_ref, qo_ref, ko_ref):
    half = ROPE_DIM // 2
    c, sn, sp = c_ref[...], sn_ref[...], sp_ref[...]
    for src, dst in ((q_ref, qo_ref), (k_ref, ko_ref)):
        for i in range(src.shape[1] // LANES):
            x = src[:, i * LANES:(i + 1) * LANES]
            nxt = pltpu.roll(x, LANES - half, axis=1)
            prv = pltpu.roll(x, half, axis=1)
            dst[:, i * LANES:(i + 1) * LANES] = x * c + nxt * sn + prv * sp


def _rope(z, tabs, nb, length, tm):
    m = z.shape[0]
    w = DA_HEADS * 2 * DA_DK
    nj = length // tm
    tab_spec = pl.BlockSpec((tm, LANES), lambda b, j: (j, 0))
    return pl.pallas_call(
        _rope_kernel,
        grid=(nb, nj),
        in_specs=[pl.BlockSpec((tm, w), lambda b, j: (b * nj + j, COL_DQ // w)),
                  pl.BlockSpec((tm, w), lambda b, j: (b * nj + j, COL_DK // w)),
                  tab_spec, tab_spec, tab_spec],
        out_specs=[pl.BlockSpec((tm, w), lambda b, j: (b * nj + j, 0))] * 2,
        out_shape=[jax.ShapeDtypeStruct((m, w), F32)] * 2,
        compiler_params=_cparams(("arbitrary", "arbitrary")),
        name="rope",
    )(z, z, *tabs)


def _gla_consts(c, seq):
    nlev = int(math.log2(seq))
    t = np.arange(c)
    tt, rr = t[:, None], t[None, :]
    same = (tt // seq) == (rr // seq)
    tril = same & (rr <= tt)
    lvl = np.full((c, c), -1, np.int32)
    lvl[t, t] = nlev
    for i in range(nlev):
        b = seq >> (i + 1)
        valid = ((tt // (2 * b)) == (rr // (2 * b))) & ((tt % (2 * b)) >= b) & ((rr % (2 * b)) < b)
        lvl[valid] = i
    return tril.astype(np.float32), lvl, nlev


def _gla_kernel(q_ref, k_ref, v_ref, og_ref, lr_ref, s0_ref, lmat_ref, lvl_ref, wa2_ref, ba_ref, gn_ref,
                o_ref, st_ref, cum_scr, *, c, seq, nlev):
    nb = c // seq
    j = pl.program_id(1)

    @pl.when(j == 0)
    def _():
        st_ref[...] = s0_ref[...]

    x = _dot(lr_ref[...].astype(BF16), wa2_ref[...]) + ba_ref[...]
    la = (jnp.minimum(x, 0.0) - jnp.log1p(jnp.exp(-jnp.abs(x)))) * (1.0 / GLA_TAU)
    cum = jnp.dot(lmat_ref[...], la, precision=lax.Precision.HIGHEST, preferred_element_type=F32)
    cum_scr[...] = cum
    wq = cum.shape[1]

    def cum_at(idx, group):
        sub = lax.broadcasted_iota(jnp.int32, (SUBLANES, wq), 0)
        bcast = lambda r: jnp.broadcast_to(cum_scr[r:r + 1, :], (SUBLANES, wq))
        tiles = []
        for t0 in range(0, c, SUBLANES):
            tile = bcast(idx(t0))
            for g in range(group, SUBLANES, group):
                tile = jnp.where(sub >= g, bcast(idx(t0 + g)), tile)
            tiles.append(tile)
        return jnp.concatenate(tiles, axis=0)

    e_end = cum_at(lambda t: (t // seq) * seq + seq - 1, seq) - cum
    e_lvl = [cum - cum_at(lambda t, b=seq >> (i + 1): (t // (2 * b)) * (2 * b) + b - 1, 2 * (seq >> (i + 1)))
             for i in range(nlev)]
    q = q_ref[...] * (GLA_DK ** -0.5)
    k = k_ref[...]
    lvl = lvl_ref[...]
    lane = lax.broadcasted_iota(jnp.int32, (c, LANES), 1)
    if nb > 1:
        rowb = lax.broadcasted_iota(jnp.int32, (c, nb * LANES), 0) // seq
        colb = lax.broadcasted_iota(jnp.int32, (c, nb * LANES), 1) // LANES
        bmask = rowb == colb

    def spread(a):
        if nb == 1:
            return a
        return jnp.where(bmask, jnp.concatenate([a] * nb, axis=1), 0.0)

    for p in range(2):
        sl = slice(p * LANES, (p + 1) * LANES)
        qp, kp, cump = q[:, sl], k[:, sl], cum[:, sl]
        qts, kts = [], []
        for i in range(nlev):
            eb = e_lvl[i][:, sl]
            qts.append(qp * jnp.exp(eb))
            kts.append((kp * jnp.exp(-eb)).astype(BF16))
        qg = qp * jnp.exp(cump)
        kh = kp * jnp.exp(e_end[:, sl])
        kpb = kp.astype(BF16)
        decays = []
        for b in range(nb):
            cl = cump[b * seq + seq - 1:b * seq + seq, :]
            decays.append(jnp.broadcast_to(jnp.exp(cl), (LANES, LANES)).T)
        for hh in range(2):
            h = 2 * p + hh
            hm = (lane // GLA_DK) == hh
            att = lax.dot_general(jnp.where(hm, qp, 0.0).astype(BF16), kpb, _NT, preferred_element_type=F32)
            att = jnp.where(lvl == nlev, att, 0.0)
            for i in range(nlev):
                a = lax.dot_general(jnp.where(hm, qts[i], 0.0).astype(BF16), kts[i], _NT,
                                    preferred_element_type=F32)
                att = jnp.where(lvl == i, a, att)
            vh = v_ref[:, h * GLA_DV:(h + 1) * GLA_DV].astype(BF16)
            o = _dot(att.astype(BF16), vh)
            s_all = st_ref[:, h].reshape(nb * LANES, GLA_DV)
            o = o + _dot(spread(jnp.where(hm, qg, 0.0)).astype(BF16), s_all.astype(BF16))
            upd = lax.dot_general(spread(jnp.where(hm, kh, 0.0)).astype(BF16), vh, _TN,
                                  preferred_element_type=F32)
            for b in range(nb):
                st_ref[b, h] = st_ref[b, h] * decays[b] + upd[b * LANES:(b + 1) * LANES]
            ms = jnp.mean(o * o, axis=-1, keepdims=True)
            og = og_ref[:, h * GLA_DV:(h + 1) * GLA_DV]
            o_ref[:, h * GLA_DV:(h + 1) * GLA_DV] = o * lax.rsqrt(ms + EPS) * gn_ref[...] * (og * _sigmoid(og))


def _gla(z, zlr, s0_pad, wa2p, ba, gn, ng, length, c, seq):
    m = z.shape[0]
    nb = c // seq
    nj = length // c
    lmat, lvl, nlev = _gla_consts(c, seq)
    w = GLA_HEADS * GLA_DK
    wv = GLA_HEADS * GLA_DV
    row = lambda g, j: g * nj + j
    const2 = lambda g, j: (0, 0)
    st_spec = pl.BlockSpec((nb, GLA_HEADS, LANES, GLA_DV), lambda g, j: (g, 0, 0, 0))
    kern = functools.partial(_gla_kernel, c=c, seq=seq, nlev=nlev)
    return pl.pallas_call(
        kern,
        grid=(ng, nj),
        in_specs=[pl.BlockSpec((c, w), lambda g, j: (row(g, j), COL_GQ // w)),
                  pl.BlockSpec((c, w), lambda g, j: (row(g, j), COL_GK // w)),
                  pl.BlockSpec((c, wv), lambda g, j: (row(g, j), COL_GV // wv)),
                  pl.BlockSpec((c, wv), lambda g, j: (row(g, j), COL_OG // wv)),
                  pl.BlockSpec((c, LANES), lambda g, j: (row(g, j), 0)),
                  st_spec,
                  pl.BlockSpec(lmat.shape, const2),
                  pl.BlockSpec(lvl.shape, const2),
                  pl.BlockSpec((LANES, w), const2),
                  pl.BlockSpec((1, w), const2),
                  pl.BlockSpec((1, GLA_DV), const2)],
        out_specs=[pl.BlockSpec((c, wv), lambda g, j: (row(g, j), 0)), st_spec],
        out_shape=[jax.ShapeDtypeStruct((m, wv), F32),
                   jax.ShapeDtypeStruct(s0_pad.shape, F32)],
        scratch_shapes=[pltpu.VMEM((c, w), F32)],
        compiler_params=_cparams(("arbitrary", "arbitrary")),
        name="gla",
    )(z, z, z, z, zlr, s0_pad, jnp.asarray(lmat), jnp.asarray(lvl), wa2p, ba, gn)


def _gla_state_pad(s):
    b = s.shape[0]
    s6 = s.reshape(b, 2, 2, 1, GLA_DK, GLA_DV) * jnp.eye(2, dtype=F32)[None, None, :, :, None, None]
    return s6.reshape(b, GLA_HEADS, LANES, GLA_DV)


def _gla_state_unpad(sp):
    b = sp.shape[0]
    s6 = sp.reshape(b, 2, 2, 2, GLA_DK, GLA_DV)
    return jnp.stack([s6[:, :, 0, 0], s6[:, :, 1, 1]], axis=2).reshape(b, GLA_HEADS, GLA_DK, GLA_DV)


_S5_CH = S5_STATES // LANES


def _s5_kernel(u_ref, h0r_ref, h0i_ref, are_ref, aim_ref, ldt_ref, bdr_ref, bdi_ref, cdr_ref, cdi_ref,
               d_ref, wg_ref, bg_ref, y_ref, hfr_ref, hfi_ref, sre, sim, hsc, hcr, hci, *, nseq, sl):
    tm = nseq * sl
    w = BRANCH_WIDTH
    j = pl.program_id(0)

    @pl.when(j == 0)
    def _():
        hcr[...] = h0r_ref[...]
        hci[...] = h0i_ref[...]

    lr, li = are_ref[...], aim_ref[...]
    dt = jnp.exp(ldt_ref[...])
    mag = jnp.exp(lr * dt)
    abr, abi = mag * jnp.cos(li * dt), mag * jnp.sin(li * dt)
    den = lr * lr + li * li
    fr = ((abr - 1.0) * lr + abi * li) / den
    fi = (abi * lr - (abr - 1.0) * li) / den

    u = u_ref[...].reshape(tm, w)
    ub = u.astype(BF16)
    for c2 in range(_S5_CH // 2):
        ws = slice((c2 // 2) * LANES, (c2 // 2 + 1) * LANES)
        tr = _dot(ub[:, ws], bdr_ref[ws, c2 * 256:(c2 + 1) * 256])
        ti = _dot(ub[:, ws], bdi_ref[ws, c2 * 256:(c2 + 1) * 256])
        for e in range(2):
            ch = 2 * c2 + e
            trc, tic = tr[:, e * LANES:(e + 1) * LANES], ti[:, e * LANES:(e + 1) * LANES]
            frc, fic = fr[ch:ch + 1, :], fi[ch:ch + 1, :]
            sre[ch * tm:(ch + 1) * tm, :] = frc * trc - fic * tic
            sim[ch * tm:(ch + 1) * tm, :] = frc * tic + fic * trc

    ar0, ai0, ar1, ai1 = abr[0:8], abi[0:8], abr[8:16], abi[8:16]
    grp = min(nseq, 4)
    for g0 in range(0, nseq, grp):
        def body(t, carry, g0=g0):
            out = []
            for q in range(grp):
                r0, i0, r1, i1 = carry[4 * q:4 * q + 4]
                row = (g0 + q) * sl + t
                lo = pl.ds(row, SUBLANES, stride=tm)
                hi = pl.ds(SUBLANES * tm + row, SUBLANES, stride=tm)
                n_r0 = ar0 * r0 - ai0 * i0 + sre[lo, :]
                n_i0 = ar0 * i0 + ai0 * r0 + sim[lo, :]
                n_r1 = ar1 * r1 - ai1 * i1 + sre[hi, :]
                n_i1 = ar1 * i1 + ai1 * r1 + sim[hi, :]
                dst = pl.ds(pl.multiple_of(row * SUBLANES, SUBLANES), SUBLANES)
                hsc[0, dst, :] = n_r0
                hsc[1, dst, :] = n_i0
                hsc[2, dst, :] = n_r1
                hsc[3, dst, :] = n_i1
                out += [n_r0, n_i0, n_r1, n_i1]
            return tuple(out)

        init = tuple(a for q in range(grp) for a in (hcr[g0 + q, 0:8], hci[g0 + q, 0:8],
                                                     hcr[g0 + q, 8:16], hci[g0 + q, 8:16]))
        fin = lax.fori_loop(0, sl, body, init, unroll=8)
        for q in range(grp):
            hcr[g0 + q, 0:8] = fin[4 * q]
            hci[g0 + q, 0:8] = fin[4 * q + 1]
            hcr[g0 + q, 8:16] = fin[4 * q + 2]
            hci[g0 + q, 8:16] = fin[4 * q + 3]

    ywin = []
    for wi in range(w // LANES):
        ws = slice(wi * LANES, (wi + 1) * LANES)
        yw = d_ref[:, ws] * u[:, ws]
        for c2 in (2 * wi, 2 * wi + 1):
            part = 2 * ((2 * c2) // SUBLANES)
            ca, cb = (2 * c2) % SUBLANES, (2 * c2 + 1) % SUBLANES
            rows = lambda cc: pl.ds(cc, tm, stride=SUBLANES)
            hr, hi = hsc.at[part], hsc.at[part + 1]
            hre = jnp.concatenate([hr[rows(ca), :], hr[rows(cb), :]], axis=1).astype(BF16)
            him = jnp.concatenate([hi[rows(ca), :], hi[rows(cb), :]], axis=1).astype(BF16)
            yw = yw + _dot(hre, cdr_ref[c2 * 256:(c2 + 1) * 256, ws]) - _dot(him, cdi_ref[c2 * 256:(c2 + 1) * 256, ws])
        ywin.append(yw)
    y = _gelu(jnp.concatenate(ywin, axis=1))
    y = y * _sigmoid(_dot(y.astype(BF16), wg_ref[...]) + bg_ref[...])
    y_ref[...] = y.reshape(nseq, sl, w)

    @pl.when(j == pl.num_programs(0) - 1)
    def _():
        hfr_ref[...] = hcr[...]
        hfi_ref[...] = hci[...]


def _s5(z3, h0r, h0i, p, sl):
    nseq, length, _ = z3.shape
    tm = nseq * sl
    w = BRANCH_WIDTH
    const2 = lambda j: (0, 0)
    st_spec = pl.BlockSpec((nseq, _S5_CH, LANES), lambda j: (0, 0, 0))
    vec = pl.BlockSpec((_S5_CH, LANES), const2)
    kern = functools.partial(_s5_kernel, nseq=nseq, sl=sl)
    return pl.pallas_call(
        kern,
        grid=(length // sl,),
        in_specs=[pl.BlockSpec((nseq, sl, w), lambda j: (0, j, COL_SU // w)),
                  st_spec, st_spec, vec, vec, vec,
                  pl.BlockSpec((w, S5_STATES), const2), pl.BlockSpec((w, S5_STATES), const2),
                  pl.BlockSpec((S5_STATES, w), const2), pl.BlockSpec((S5_STATES, w), const2),
                  pl.BlockSpec((1, w), const2), pl.BlockSpec((w, w), const2), pl.BlockSpec((1, w), const2)],
        out_specs=[pl.BlockSpec((nseq, sl, w), lambda j: (0, j, 0)), st_spec, st_spec],
        out_shape=[jax.ShapeDtypeStruct((nseq, length, w), F32),
                   jax.ShapeDtypeStruct(h0r.shape, F32), jax.ShapeDtypeStruct(h0i.shape, F32)],
        scratch_shapes=[pltpu.VMEM((_S5_CH * tm, LANES), F32), pltpu.VMEM((_S5_CH * tm, LANES), F32),
                        pltpu.VMEM((4, SUBLANES * tm, LANES), F32),
                        pltpu.VMEM((nseq, _S5_CH, LANES), F32), pltpu.VMEM((nseq, _S5_CH, LANES), F32)],
        compiler_params=_cparams(("arbitrary",)),
        name="s5",
    )(z3, h0r, h0i, p["are"], p["aim"], p["ldt"], p["bdr"], p["bdi"], p["cdr"], p["cdi"],
      p["d"], p["wg"], p["bg"])


def _s5_params(a_re, a_im, log_dt, b_re, b_im, c_re, c_im, d, w_glu, b_glu):
    eye = jnp.eye(S5_GROUPS, dtype=F32)

    def bd_in(b):
        return (eye[:, None, :, None] * jnp.swapaxes(b, 1, 2)[:, :, None, :]).reshape(
            S5_GROUPS * S5_GROUP, S5_STATES).astype(BF16)

    def bd_out(c):
        return (eye[:, None, :, None] * jnp.swapaxes(c, 1, 2)[:, :, None, :]).reshape(
            S5_STATES, S5_GROUPS * S5_GROUP).astype(BF16)

    return dict(are=a_re.reshape(_S5_CH, LANES), aim=a_im.reshape(_S5_CH, LANES),
                ldt=jnp.broadcast_to(log_dt[:, None], (S5_GROUPS, S5_N)).reshape(_S5_CH, LANES),
                bdr=bd_in(b_re), bdi=bd_in(b_im), cdr=bd_out(c_re), cdi=bd_out(c_im),
                d=d.reshape(1, -1), wg=w_glu.astype(BF16), bg=b_glu.reshape(1, -1))


_HALO = SUBLANES


def _lru_kernel(lx_ref, lg_ref, zprev_ref, cst_ref, h0_ref, cw_ref, cb_ref, wa_ref, ba_ref, wx_ref, bx_ref,
                lam_ref, y_ref, hfin_ref, cs_scr, xc_scr, a_scr, b_scr, h_c, *, nseq, sl):
    tm = nseq * sl
    j = pl.program_id(0)

    @pl.when(j == 0)
    def _():
        h_c[...] = h0_ref[...]

    k = LRU_CONV
    for s in range(nseq):
        cs_scr[0:_HALO, :] = jnp.where(j == 0, cst_ref[s], zprev_ref[s])
        cs_scr[_HALO:_HALO + sl, :] = lx_ref[s]
        xc = cb_ref[...]
        for t in range(k):
            off = _HALO - (k - 1) + t
            xc = xc + cw_ref[t:t + 1, :] * cs_scr[off:off + sl, :]
        xc_scr[s * sl:(s + 1) * sl, :] = xc

    lam = lam_ref[...]
    sp = jnp.maximum(-lam, 0.0) + jnp.log1p(jnp.exp(-jnp.abs(lam)))
    for h in range(LRU_HEADS):
        hs = slice(h * LRU_BLOCK, (h + 1) * LRU_BLOCK)
        xc = xc_scr[:, hs]
        xb = xc.astype(BF16)
        r = _sigmoid(_dot(xb, wa_ref[h]) + ba_ref[:, hs])
        i = _sigmoid(_dot(xb, wx_ref[h]) + bx_ref[:, hs])
        la = -LRU_C * r * sp[:, hs]
        a_scr[:, hs] = jnp.exp(la)
        b_scr[:, hs] = jnp.sqrt(-jnp.tanh(la) * (jnp.exp(2.0 * la) + 1.0)) * (i * xc)

    grp = min(nseq, 4)
    for g0 in range(0, nseq, grp):
        def body(t, hs, g0=g0):
            out = []
            for q in range(grp):
                idx = pl.ds((g0 + q) * sl + t, 1)
                h = a_scr[idx, :] * hs[q] + b_scr[idx, :]
                b_scr[idx, :] = h
                out.append(h)
            return tuple(out)

        fin = lax.fori_loop(0, sl, body, tuple(h_c[g0 + q:g0 + q + 1, :] for q in range(grp)), unroll=8)
        for q in range(grp):
            h_c[g0 + q:g0 + q + 1, :] = fin[q]

    y = b_scr[...] * _gelu(lg_ref[...].reshape(tm, BRANCH_WIDTH))
    y_ref[...] = y.reshape(nseq, sl, BRANCH_WIDTH)

    @pl.when(j == pl.num_programs(0) - 1)
    def _():
        hfin_ref[...] = h_c[...]


def _lru(z3, cst, h0, p, sl):
    nseq, length, _ = z3.shape
    tm = nseq * sl
    w = BRANCH_WIDTH
    const2 = lambda j: (0, 0)
    const3 = lambda j: (0, 0, 0)
    st_spec = pl.BlockSpec((nseq, w), const2)
    vec = pl.BlockSpec((1, w), const2)
    wsp = pl.BlockSpec((LRU_HEADS, LRU_BLOCK, LRU_BLOCK), const3)
    rb = sl // _HALO
    kern = functools.partial(_lru_kernel, nseq=nseq, sl=sl)
    return pl.pallas_call(
        kern,
        grid=(length // sl,),
        in_specs=[pl.BlockSpec((nseq, sl, w), lambda j: (0, j, COL_LX // w)),
                  pl.BlockSpec((nseq, sl, w), lambda j: (0, j, COL_LG // w)),
                  pl.BlockSpec((nseq, _HALO, w), lambda j: (0, jnp.maximum(j * rb - 1, 0), COL_LX // w)),
                  pl.BlockSpec((nseq, _HALO, w), const3),
                  st_spec,
                  pl.BlockSpec((LRU_CONV, w), const2), vec, wsp, vec, wsp, vec, vec],
        out_specs=[pl.BlockSpec((nseq, sl, w), lambda j: (0, j, 0)), st_spec],
        out_shape=[jax.ShapeDtypeStruct((nseq, length, w), F32), jax.ShapeDtypeStruct(h0.shape, F32)],
        scratch_shapes=[pltpu.VMEM((_HALO + sl, w), F32), pltpu.VMEM((tm, w), F32),
                        pltpu.VMEM((tm, w), F32), pltpu.VMEM((tm, w), F32),
                        pltpu.VMEM((nseq, w), F32)],
        compiler_params=_cparams(("arbitrary",)),
        name="lru",
    )(z3, z3, z3, cst, h0, p["cw"], p["cb"], p["wa"], p["ba"], p["wx"], p["bx"], p["lam"])


def _da_lambda(lq1_ref, lk1_ref, lq2_ref, lk2_ref, lam_init):
    return (jnp.exp(jnp.sum(lq1_ref[...] * lk1_ref[...])) - jnp.exp(jnp.sum(lq2_ref[...] * lk2_ref[...]))
            + lam_init)


def _attn_kernel(q_ref, k_ref, v_ref, lq1_ref, lk1_ref, lq2_ref, lk2_ref, dn_ref, o_ref,
                 m_scr, l_scr, acc_scr, *, tq, tk, lam_init):
    qi, ki = pl.program_id(2), pl.program_id(3)

    @pl.when(ki == 0)
    def _():
        m_scr[...] = jnp.full(m_scr.shape, -jnp.inf, F32)
        l_scr[...] = jnp.zeros(l_scr.shape, F32)
        acc_scr[...] = jnp.zeros(acc_scr.shape, F32)

    def step(masked):
        q = q_ref[...] * (DA_DK ** -0.5)
        kb = k_ref[...].astype(BF16)
        vb = v_ref[...].astype(BF16)
        lane = lax.broadcasted_iota(jnp.int32, (tq, LANES), 1)
        if masked:
            row = qi * tq + lax.broadcasted_iota(jnp.int32, (tq, tk), 0)
            col = ki * tk + lax.broadcasted_iota(jnp.int32, (tq, tk), 1)
            mask = col <= row
        comps = range(2)
        ss = [lax.dot_general(jnp.where((lane // DA_DK) == c, q, 0.0).astype(BF16), kb, _NT,
                              preferred_element_type=F32) for c in comps]
        if masked:
            ss = [jnp.where(mask, s, -jnp.inf) for s in ss]
        m_prev = [m_scr[c] for c in comps]
        l_prev = [l_scr[c] for c in comps]
        a_prev = [acc_scr[c] for c in comps]
        m_new = [jnp.maximum(m_prev[c], jnp.max(ss[c], axis=-1, keepdims=True)) for c in comps]
        alpha = [jnp.exp(m_prev[c] - m_new[c]) for c in comps]
        pexp = [jnp.exp(ss[c] - m_new[c]) for c in comps]
        pv = [_dot(pexp[c].astype(BF16), vb) for c in comps]
        for c in comps:
            l_scr[c] = alpha[c] * l_prev[c] + jnp.sum(pexp[c], axis=-1, keepdims=True)
            acc_scr[c] = alpha[c] * a_prev[c] + pv[c]
            m_scr[c] = m_new[c]

    last_col, first_col = ki * tk + tk - 1, ki * tk
    first_row, last_row = qi * tq, qi * tq + tq - 1

    @pl.when(last_col <= first_row)
    def _():
        step(False)

    @pl.when((last_col > first_row) & (first_col <= last_row))
    def _():
        step(True)

    @pl.when(ki == pl.num_programs(3) - 1)
    def _():
        lam = _da_lambda(lq1_ref, lk1_ref, lq2_ref, lk2_ref, lam_init)
        o = acc_scr[0] / l_scr[0] - lam * (acc_scr[1] / l_scr[1])
        ms = jnp.mean(o * o, axis=-1, keepdims=True)
        o_ref[...] = o * lax.rsqrt(ms + EPS) * dn_ref[...] * (1.0 - lam_init)


def _attn_prompt(q_rot, k_rot, z, lam_p, dn, nb, length, tq, tk, lam_init):
    m = q_rot.shape[0]
    nq, nk = length // tq, length // tk
    const2 = lambda b, h, qi, ki: (0, 0)
    kv_row = lambda b, qi, ki: b * nk + jnp.minimum(ki, (qi * tq + tq - 1) // tk)
    vec = pl.BlockSpec((1, DA_DK), const2)
    kern = functools.partial(_attn_kernel, tq=tq, tk=tk, lam_init=lam_init)
    return pl.pallas_call(
        kern,
        grid=(nb, DA_HEADS, nq, nk),
        in_specs=[pl.BlockSpec((tq, LANES), lambda b, h, qi, ki: (b * nq + qi, h)),
                  pl.BlockSpec((tk, LANES), lambda b, h, qi, ki: (kv_row(b, qi, ki), h)),
                  pl.BlockSpec((tk, DA_DV), lambda b, h, qi, ki: (kv_row(b, qi, ki), COL_DV // DA_DV + h)),
                  vec, vec, vec, vec, pl.BlockSpec((1, DA_DV), const2)],
        out_specs=pl.BlockSpec((tq, DA_DV), lambda b, h, qi, ki: (b * nq + qi, h)),
        out_shape=jax.ShapeDtypeStruct((m, DA_HEADS * DA_DV), F32),
        scratch_shapes=[pltpu.VMEM((2, tq, 1), F32), pltpu.VMEM((2, tq, 1), F32),
                        pltpu.VMEM((2, tq, DA_DV), F32)],
        compiler_params=_cparams(("arbitrary", "arbitrary", "arbitrary", "arbitrary")),
        name="attn_prompt",
    )(q_rot, k_rot, z, *lam_p, dn)


_PAGES_PER_STEP = 16
_QROWS = DA_HEADS * 2 * SUBLANES


def _attn_sample_kernel(pt_ref, q_ref, *refs, lq, lam_init):
    npg = _PAGES_PER_STEP
    kt_refs, v_refs = refs[0:npg], refs[npg:2 * npg]
    kn_ref, vn_ref, lq1_ref, lk1_ref, lq2_ref, lk2_ref, dn_ref = refs[2 * npg:2 * npg + 7]
    o_ref = refs[2 * npg + 7]
    qbd_scr, m_scr, l_scr, acc_scr = refs[2 * npg + 8:]
    w = DA_HEADS * 2 * DA_DK
    hrows = 2 * lq
    j = pl.program_id(1)

    @pl.when(j == 0)
    def _():
        q = q_ref[...] * (DA_DK ** -0.5)
        lane = lax.broadcasted_iota(jnp.int32, (lq, w), 1)
        tiles = [jnp.where((lane // DA_DK) == hc, q, 0.0) for hc in range(DA_HEADS * 2)]
        qbd_scr[...] = jnp.concatenate(tiles, axis=0)
        m_scr[...] = jnp.full(m_scr.shape, -jnp.inf, F32)
        l_scr[...] = jnp.zeros(l_scr.shape, F32)
        acc_scr[...] = jnp.zeros(acc_scr.shape, F32)

    qbd = qbd_scr[...].astype(BF16)

    def update(s, nchunk, vget):
        m_prev = m_scr[...]
        m_new = jnp.maximum(m_prev, jnp.max(s, axis=-1, keepdims=True))
        alpha = jnp.exp(m_prev - m_new)
        pexp = jnp.exp(s - m_new)
        l_scr[...] = alpha * l_scr[...] + jnp.sum(pexp, axis=-1, keepdims=True)
        pb = pexp.astype(BF16)
        for h in range(DA_HEADS):
            rs = slice(h * hrows, (h + 1) * hrows)
            pv = _dot(pb[rs, 0:PAGE_SIZE], vget(0, h))
            for i in range(1, nchunk):
                pv = pv + _dot(pb[rs, i * PAGE_SIZE:(i + 1) * PAGE_SIZE], vget(i, h))
            acc_scr[rs, :] = alpha[rs] * acc_scr[rs, :] + pv
        m_scr[...] = m_new

    s_pages = jnp.concatenate([_dot(qbd, kt_refs[i][...].astype(BF16)) for i in range(npg)], axis=1)
    update(s_pages, npg, lambda i, h: v_refs[i][pl.ds(h, PAGE_SIZE, stride=DA_HEADS), :].astype(BF16))

    @pl.when(j == pl.num_programs(1) - 1)
    def _():
        nk = kn_ref.shape[0]
        t = lax.broadcasted_iota(jnp.int32, (_QROWS, nk), 0) % lq
        col = lax.broadcasted_iota(jnp.int32, (_QROWS, nk), 1)
        s_new = lax.dot_general(qbd, kn_ref[...].astype(BF16), _NT, preferred_element_type=F32)
        update(jnp.where(col <= t, s_new, -jnp.inf), 1,
               lambda i, h: vn_ref[:, h * DA_DV:(h + 1) * DA_DV].astype(BF16))
        lam = _da_lambda(lq1_ref, lk1_ref, lq2_ref, lk2_ref, lam_init)
        for h in range(DA_HEADS):
            r0 = h * hrows
            o0 = acc_scr[r0:r0 + lq, :] / l_scr[r0:r0 + lq]
            o1 = acc_scr[r0 + lq:r0 + 2 * lq, :] / l_scr[r0 + lq:r0 + 2 * lq]
            o = o0 - lam * o1
            ms = jnp.mean(o * o, axis=-1, keepdims=True)
            o_ref[:, h * DA_DV:(h + 1) * DA_DV] = o * lax.rsqrt(ms + EPS) * dn_ref[...] * (1.0 - lam_init)


def _attn_sample(q_rot, cache_k, cache_v, layer, page_table, k_new, v_new, lam_p, dn, lam_init):
    nb, n_pages = page_table.shape
    lq = q_rot.shape[0] // nb
    w = DA_HEADS * 2 * DA_DK
    wv = DA_HEADS * DA_DV
    npg = _PAGES_PER_STEP
    nsteps = n_pages // npg
    ckt = jnp.transpose(cache_k, (0, 1, 3, 4, 5, 2)).reshape(DEPTH, -1, w, PAGE_SIZE)
    cv2 = cache_v.reshape(DEPTH, -1, PAGE_SIZE * DA_HEADS, DA_DV)
    const2 = lambda b, j, pt: (0, 0)
    page = lambda b, j, pt, i: pt[b * n_pages + j * npg + i]

    def k_spec(i):
        return pl.BlockSpec((None, None, w, PAGE_SIZE), lambda b, j, pt: (layer, page(b, j, pt, i), 0, 0))

    def v_spec(i):
        return pl.BlockSpec((None, None, PAGE_SIZE * DA_HEADS, DA_DV),
                            lambda b, j, pt: (layer, page(b, j, pt, i), 0, 0))

    vec = pl.BlockSpec((1, DA_DK), const2)
    nk = k_new.shape[1]
    grid_spec = pltpu.PrefetchScalarGridSpec(
        num_scalar_prefetch=1,
        grid=(nb, nsteps),
        in_specs=([pl.BlockSpec((lq, w), lambda b, j, pt: (b, 0))]
                  + [k_spec(i) for i in range(npg)] + [v_spec(i) for i in range(npg)]
                  + [pl.BlockSpec((None, nk, w), lambda b, j, pt: (b, 0, 0)),
                     pl.BlockSpec((None, nk, wv), lambda b, j, pt: (b, 0, 0)),
                     vec, vec, vec, vec, pl.BlockSpec((1, DA_DV), const2)]),
        out_specs=pl.BlockSpec((lq, wv), lambda b, j, pt: (b, 0)),
        scratch_shapes=[pltpu.VMEM((_QROWS, w), F32), pltpu.VMEM((_QROWS, 1), F32),
                        pltpu.VMEM((_QROWS, 1), F32), pltpu.VMEM((_QROWS, DA_DV), F32)],
    )
    kern = functools.partial(_attn_sample_kernel, lq=lq, lam_init=lam_init)
    return pl.pallas_call(
        kern,
        grid_spec=grid_spec,
        out_shape=jax.ShapeDtypeStruct((nb * lq, wv), F32),
        compiler_params=_cparams(("arbitrary", "arbitrary")),
        name="attn_sample",
    )(page_table.reshape(-1), q_rot, *([ckt] * npg), *([cv2] * npg), k_new, v_new, *lam_p, dn)


def _merge_kernel(o0_ref, o1_ref, o2_ref, o3_ref, g_ref, wb_ref, wo_ref, x_ref, g1_ref, lng_ref, lnb_ref,
                  out_ref, acc_scr):
    b = pl.program_id(1)

    @pl.when(b == 0)
    def _():
        acc_scr[...] = jnp.zeros(acc_scr.shape, F32)

    o = jnp.where(b == 0, o0_ref[...], jnp.where(b == 1, o1_ref[...], jnp.where(b == 2, o2_ref[...], o3_ref[...])))
    acc_scr[...] += _sigmoid(g_ref[...]) * _dot(o.astype(BF16), wb_ref[...])

    @pl.when(b == N_BRANCH - 1)
    def _():
        mix = _dot(acc_scr[...].astype(BF16), wo_ref[...])
        y = DEEPNORM_ALPHA * x_ref[...] + g1_ref[...] * mix
        out_ref[...] = _layer_norm(y, lng_ref[...], lnb_ref[...])


def _merge(branches, z, w_branch, w_out, layer, x2d, g1, gdiv, ln_g, ln_b, tm):
    m = x2d.shape[0]
    d = D_MODEL
    r = g1.shape[1]
    const2 = lambda i, b: (0, 0)
    osp = pl.BlockSpec((tm, BRANCH_WIDTH), lambda i, b: (i, 0))
    return pl.pallas_call(
        _merge_kernel,
        grid=(m // tm, N_BRANCH),
        in_specs=[osp, osp, osp, osp,
                  pl.BlockSpec((tm, d), lambda i, b: (i, b)),
                  pl.BlockSpec((None, None, BRANCH_WIDTH, d), lambda i, b: (layer, b, 0, 0)),
                  pl.BlockSpec((None, d, d), lambda i, b: (layer, 0, 0), pipeline_mode=pl.Buffered(1)),
                  pl.BlockSpec((tm, d), lambda i, b: (i, 0)),
                  pl.BlockSpec((None, r, d), lambda i, b: (i // gdiv, 0, 0)),
                  pl.BlockSpec((1, d), const2), pl.BlockSpec((1, d), const2)],
        out_specs=pl.BlockSpec((tm, d), lambda i, b: (i, 0)),
        out_shape=jax.ShapeDtypeStruct((m, d), F32),
        scratch_shapes=[pltpu.VMEM((tm, d), F32)],
        compiler_params=_cparams(("arbitrary", "arbitrary")),
        name="merge",
    )(*branches, z, w_branch, w_out, x2d, g1, ln_g, ln_b)


_FHALO = 2 * SUBLANES


def _ffn_kernel(x_ref, xprev_ref, sc_ref, sh_ref, g2_ref, csta_ref, cstv_ref, wa_ref, wv_ref, cw_ref, cb_ref,
                wd_ref, lng_ref, lnb_ref, out_ref, sta_ref, stv_ref, u_scr, hsa, hsv, act_scr, acc_scr,
                *, nseq, sl, first_div, rc):
    tm = nseq * sl
    halo = _FHALO if nseq == 1 else 0
    seg = _FHALO + sl
    i, j = pl.program_id(0), pl.program_id(1)
    nf = pl.num_programs(1)

    @pl.when(j == 0)
    def _():
        sc, sh = 1.0 + sc_ref[...], sh_ref[...]
        if halo:
            u_scr[0:halo, :] = (xprev_ref[...] * sc + sh).astype(BF16)
        u_scr[halo:halo + tm, :] = (x_ref[...] * sc + sh).astype(BF16)
        acc_scr[...] = jnp.zeros(acc_scr.shape, F32)

    parts = ((wa_ref, csta_ref, hsa, sta_ref), (wv_ref, cstv_ref, hsv, stv_ref))
    if nseq == 1:
        half = tm // 2
        spans = ((0, half), (half, tm))
        first = i % first_div == 0
        for lo, hi in ((0, _FHALO + half), (_FHALO + half, _FHALO + tm)):
            for w_ref, cst_ref, hs, _ in parts:
                hs[lo:hi, :] = _dot(u_scr[lo:hi, :], w_ref[...])
                if lo == 0:
                    hs[_FHALO - _HALO:_FHALO, :] = jnp.where(first, cst_ref[0], hs[_FHALO - _HALO:_FHALO, :])
    else:
        spans = ((0, tm),)
        u = u_scr[...]
        for w_ref, cst_ref, hs, _ in parts:
            hup = _dot(u, w_ref[...])
            for s in range(nseq):
                hs[s * seg + _FHALO - _HALO:s * seg + _FHALO, :] = cst_ref[s]
                hs[s * seg + _FHALO:(s + 1) * seg, :] = hup[s * sl:(s + 1) * sl]
    for _, _, hs, st_ref in parts:
        for s in range(nseq):
            st_ref[s] = hs[(s + 1) * seg - _HALO:(s + 1) * seg, :]

    k = FFN_CONV
    base = _FHALO - (k - 1)
    for lo, hi in spans:
        for r0 in range(lo, hi, rc):
            s, t0 = divmod(r0, sl)
            o = s * seg + base + t0
            ya, yv = cb_ref[0], cb_ref[1]
            for t in range(k):
                ya = ya + cw_ref[0, t:t + 1, :] * hsa[o + t:o + t + rc, :]
                yv = yv + cw_ref[1, t:t + 1, :] * hsv[o + t:o + t + rc, :]
            act_scr[r0:r0 + rc, :] = (_gelu(ya) * yv).astype(act_scr.dtype)
        acc_scr[lo:hi, :] += _dot(act_scr[lo:hi, :].astype(BF16), wd_ref[...])

    @pl.when(j == nf - 1)
    def _():
        y = DEEPNORM_ALPHA * x_ref[...] + g2_ref[...] * acc_scr[...]
        out_ref[...] = _layer_norm(y, lng_ref[...], lnb_ref[...])


def _ffn(x2d, sc, sh, g2, gdiv, cst, w_up, conv_w, conv_b, w_down, layer, ln_g, ln_b, nseq, sl, first_div, tf):
    m, d = x2d.shape
    tm = nseq * sl
    nblk = m // tm
    nf = D_FF // tf
    r = sc.shape[1]
    halo = _FHALO if nseq == 1 else 0
    const2 = lambda i, j: (0, 0)
    mod_spec = pl.BlockSpec((None, r, d), lambda i, j: (i // gdiv, 0, 0))
    rb = tm // _FHALO
    cw2 = conv_w.reshape(FFN_CONV, 2, D_FF).transpose(1, 0, 2)
    cb2 = conv_b.reshape(2, 1, D_FF)
    csta, cstv = cst[..., :D_FF], cst[..., D_FF:]
    cst_spec = pl.BlockSpec((None, nseq, _HALO, tf), lambda i, j: (i // first_div, 0, 0, j))
    st_spec = pl.BlockSpec((None, nseq, _HALO, tf), lambda i, j: (i, 0, 0, j))
    st_shape = jax.ShapeDtypeStruct((nblk, nseq, _HALO, D_FF), F32)
    rc = 32 if sl % 32 == 0 else sl
    act_dtype = BF16 if rc % (2 * SUBLANES) == 0 else F32
    kern = functools.partial(_ffn_kernel, nseq=nseq, sl=sl, first_div=first_div, rc=rc)
    return pl.pallas_call(
        kern,
        grid=(nblk, nf),
        in_specs=[pl.BlockSpec((tm, d), lambda i, j: (i, 0)),
                  pl.BlockSpec((_FHALO, d), lambda i, j: (jnp.maximum(i * rb - 1, 0), 0)),
                  mod_spec, mod_spec, mod_spec, cst_spec, cst_spec,
                  pl.BlockSpec((None, d, tf), lambda i, j: (layer, 0, j)),
                  pl.BlockSpec((None, d, tf), lambda i, j: (layer, 0, nf + j)),
                  pl.BlockSpec((2, FFN_CONV, tf), lambda i, j: (0, 0, j)),
                  pl.BlockSpec((2, 1, tf), lambda i, j: (0, 0, j)),
                  pl.BlockSpec((None, tf, d), lambda i, j: (layer, j, 0)),
                  pl.BlockSpec((1, d), const2), pl.BlockSpec((1, d), const2)],
        out_specs=[pl.BlockSpec((tm, d), lambda i, j: (i, 0)), st_spec, st_spec],
        out_shape=[jax.ShapeDtypeStruct((m, d), F32), st_shape, st_shape],
        scratch_shapes=[pltpu.VMEM((halo + tm, d), BF16),
                        pltpu.VMEM((nseq * (_FHALO + sl), tf), F32), pltpu.VMEM((nseq * (_FHALO + sl), tf), F32),
                        pltpu.VMEM((tm, tf), act_dtype),
                        pltpu.VMEM((tm, d), F32)],
        compiler_params=_cparams(("arbitrary", "arbitrary")),
        name="ffn",
    )(x2d, x2d, sc, sh, g2, csta, cstv, w_up, w_up, cw2, cb2, w_down, ln_g, ln_b)


def _pad_state_rows(buf, rows):
    pad = [(0, 0)] * buf.ndim
    pad[-2] = (rows - buf.shape[-2], 0)
    return jnp.pad(buf, pad)


def _run_layer(x2d, mod, cfg, wts, l, states, paged):
    nb, length = cfg["nb"], cfg["len"]
    m = nb * length
    sh1, sc1, g1, sh2, sc2, g2 = mod
    gdiv = cfg["gdiv"]
    st_gla, st_s5re, st_s5im, st_lru, st_lconv, st_fconv = states

    z, zlr = _inproj(x2d, sc1, sh1, gdiv(cfg["tm_in"]), wts["w_main"], wts["w_lr"], l, cfg["tm_in"], cfg["tn_in"])

    ng_seq, nseq = cfg["ng"], cfg["nseq"]
    o_gla, gla_new = _gla(z, zlr, _gla_state_pad(st_gla), wts["wa2p"], wts["gla_ba"], wts["gla_norm"],
                          ng_seq, m // ng_seq, cfg["gla_c"], cfg["gla_seq"])
    gla_new = _gla_state_unpad(gla_new)

    sl = cfg["sl"]
    o_s5, s5re_new, s5im_new = _s5(z.reshape(nb, length, Z_MAIN), st_s5re.reshape(nb, _S5_CH, LANES),
                                   st_s5im.reshape(nb, _S5_CH, LANES), wts["s5"], cfg["s5_sl"])
    o_s5 = o_s5.reshape(m, BRANCH_WIDTH)
    s5re_new = s5re_new.reshape(nb, S5_GROUPS, S5_N)
    s5im_new = s5im_new.reshape(nb, S5_GROUPS, S5_N)

    o_lru, lru_new = _lru(z.reshape(nb, length, Z_MAIN), _pad_state_rows(st_lconv, _HALO), st_lru, wts["lru"],
                          cfg["s5_sl"])
    o_lru = o_lru.reshape(m, BRANCH_WIDTH)
    lconv_new = z[:, COL_LX:COL_LX + BRANCH_WIDTH].reshape(nb, length, BRANCH_WIDTH)[:, length - (LRU_CONV - 1):]

    lam_init = 0.8 - 0.6 * math.exp(-0.3 * l)
    q_rot, k_rot = _rope(z, cfg["rope"], nb, length, cfg["tm_rope"])
    v_new = z[:, COL_DV:COL_DV + DA_HEADS * DA_DV]
    if paged is None:
        o_da = _attn_prompt(q_rot, k_rot, z, wts["lam_p"], wts["da_norm"], nb, length,
                            cfg["tq"], cfg["tk"], lam_init)
    else:
        page_table, cache_k, cache_v = paged
        padk = ((0, 0), (0, PAGE_SIZE - length), (0, 0))
        k_pad = jnp.pad(k_rot.reshape(nb, length, -1), padk)
        v_pad = jnp.pad(v_new.reshape(nb, length, -1), padk)
        o_da = _attn_sample(q_rot, cache_k, cache_v, l, page_table, k_pad, v_pad, wts["lam_p"],
                            wts["da_norm"], lam_init)

    x1 = _merge((o_gla, o_s5, o_lru, o_da), z, wts["w_branch"], wts["w_out"], l, x2d, g1, gdiv(cfg["tm_mg"]),
                wts["ln1_g"], wts["ln1_b"], cfg["tm_mg"])

    fcst = _pad_state_rows(st_fconv, _HALO).reshape(ng_seq, nseq, _HALO, 2 * D_FF)
    fsl = cfg["ffn_sl"]
    x2, sta, stv = _ffn(x1, sc2, sh2, g2, gdiv(nseq * fsl), fcst, wts["w_up"], wts["ffn_conv_w"],
                        wts["ffn_conv_b"], wts["w_down"], l, wts["ln2_g"], wts["ln2_b"], nseq, fsl,
                        length // fsl if nseq == 1 else 1, cfg["tf"])
    fst = jnp.concatenate([sta, stv], axis=-1)
    if nseq == 1:
        per_seq = length // fsl
        fst = fst.reshape(nb, per_seq, _HALO, 2 * D_FF)[:, per_seq - 1]
    else:
        fst = fst.reshape(nb, _HALO, 2 * D_FF)
    fconv_new = fst[:, _HALO - (FFN_CONV - 1):]

    new = (k_rot.reshape(nb, length, DA_HEADS, 2, DA_DK), v_new.reshape(nb, length, DA_HEADS, DA_DV),
           gla_new, s5re_new, s5im_new, lru_new, lconv_new, fconv_new)
    return x2, new


def _layer_weights(l, w_in, gla_wa2, gla_ba, gla_norm, s5_raw, lru_raw, lam_raw, da_norm, w_branch, w_out,
                   ln1_g, ln1_b, ffn_w_up, ffn_conv_w, ffn_conv_b, ffn_w_down, ln2_g, ln2_b):
    n_mix = 1024
    lr0 = n_mix
    rest0 = n_mix + GLA_LOWRANK
    gate0 = w_in.shape[2] - N_BRANCH * D_MODEL
    w_main = jnp.concatenate([w_in[:, :, gate0:], w_in[:, :, :n_mix], w_in[:, :, rest0:gate0]], axis=2).astype(BF16)
    w_lr = jnp.pad(w_in[:, :, lr0:rest0], ((0, 0), (0, 0), (0, LANES - GLA_LOWRANK))).astype(BF16)
    wa2p = jnp.pad(gla_wa2[l], ((0, LANES - GLA_LOWRANK), (0, 0))).astype(BF16)
    row = lambda a: a[l].reshape(1, -1)
    cw, cb, wa, ba, wx, bx, lam = [a[l] for a in lru_raw]
    return dict(
        w_main=w_main, w_lr=w_lr, wa2p=wa2p, gla_ba=row(gla_ba), gla_norm=row(gla_norm),
        s5=_s5_params(*[a[l] for a in s5_raw]),
        lru=dict(cw=cw, cb=cb.reshape(1, -1), wa=wa.astype(BF16), ba=ba.reshape(1, -1),
                 wx=wx.astype(BF16), bx=bx.reshape(1, -1), lam=lam.reshape(1, -1)),
        lam_p=tuple(row(a) for a in lam_raw), da_norm=row(da_norm),
        w_branch=w_branch.astype(BF16), w_out=w_out.astype(BF16),
        ln1_g=row(ln1_g), ln1_b=row(ln1_b),
        w_up=ffn_w_up.astype(BF16), ffn_conv_w=ffn_conv_w[l], ffn_conv_b=ffn_conv_b[l],
        w_down=ffn_w_down.astype(BF16), ln2_g=row(ln2_g), ln2_b=row(ln2_b))


def kernel(x_prompt, x_sample, cache_k, cache_v, state_gla, state_s5_re, state_s5_im, state_lru, state_lru_conv, state_ffn_conv, page_table, c_prompt, c_sample, w_ada, b_ada, w_in, gla_wa2, gla_ba, gla_norm, s5_a_re, s5_a_im, s5_log_dt, s5_b_re, s5_b_im, s5_c_re, s5_c_im, s5_d, s5_w_glu, s5_b_glu, lru_conv_w, lru_conv_b, lru_w_a, lru_b_a, lru_w_x, lru_b_x, lru_lambda, da_lq1, da_lk1, da_lq2, da_lk2, da_norm, w_branch, w_out, ln1_g, ln1_b, ffn_w_up, ffn_conv_w, ffn_conv_b, ffn_w_down, ln2_g, ln2_b):
    bp, lp, d = x_prompt.shape
    bs, ls, _ = x_sample.shape
    past_len = page_table.shape[1] * PAGE_SIZE

    rows = bp + bs
    rows_pad = -(-rows // SUBLANES) * SUBLANES
    c_all = jnp.pad(jnp.concatenate([c_prompt, c_sample], axis=0), ((0, rows_pad - rows), (0, 0)))
    mod_all = _ada(c_all, w_ada, b_ada)

    def mods(l, lo, n, per_row):
        parts = jnp.split(mod_all[l, lo:lo + n], 6, axis=-1)
        if per_row:
            return [jnp.repeat(p, per_row, axis=0)[None] for p in parts]
        return [p[:, None, :] for p in parts]

    cfg_p = dict(nb=bp, len=lp, ng=bp, nseq=1, sl=256, s5_sl=128, ffn_sl=512, tm_in=1024, tn_in=1280, gla_c=128,
                 gla_seq=128, tm_rope=512, tq=1024, tk=512, tm_mg=512, tf=512,
                 gdiv=lambda tm: lp // tm, rope=_rope_tables(0, lp))
    cfg_s = dict(nb=bs, len=ls, ng=1, nseq=bs, sl=ls, s5_sl=ls, ffn_sl=ls, tm_in=bs * ls, tn_in=1280, gla_c=bs * ls,
                 gla_seq=ls, tm_rope=ls, tm_mg=bs * ls, tf=512,
                 gdiv=lambda tm: 1, rope=_rope_tables(past_len, ls))

    zeros_p = (jnp.zeros((bp, GLA_HEADS, GLA_DK, GLA_DV), F32), jnp.zeros((bp, S5_GROUPS, S5_N), F32),
               jnp.zeros((bp, S5_GROUPS, S5_N), F32), jnp.zeros((bp, BRANCH_WIDTH), F32),
               jnp.zeros((bp, LRU_CONV - 1, BRANCH_WIDTH), F32), jnp.zeros((bp, FFN_CONV - 1, 2 * D_FF), F32))

    xp = x_prompt.reshape(bp * lp, d)
    xs = x_sample.reshape(bs * ls, d)
    col_p = [[] for _ in range(8)]
    col_s = [[] for _ in range(8)]
    s5_raw = (s5_a_re, s5_a_im, s5_log_dt, s5_b_re, s5_b_im, s5_c_re, s5_c_im, s5_d, s5_w_glu, s5_b_glu)
    lru_raw = (lru_conv_w, lru_conv_b, lru_w_a, lru_b_a, lru_w_x, lru_b_x, lru_lambda)
    lam_raw = (da_lq1, da_lk1, da_lq2, da_lk2)
    for l in range(DEPTH):
        wts = _layer_weights(l, w_in, gla_wa2, gla_ba, gla_norm, s5_raw, lru_raw, lam_raw, da_norm, w_branch,
                             w_out, ln1_g, ln1_b, ffn_w_up, ffn_conv_w, ffn_conv_b, ffn_w_down, ln2_g, ln2_b)
        xp, new_p = _run_layer(xp, mods(l, 0, bp, 0), cfg_p, wts, l, zeros_p, None)
        st_s = (state_gla[l], state_s5_re[l], state_s5_im[l], state_lru[l], state_lru_conv[l], state_ffn_conv[l])
        xs, new_s = _run_layer(xs, mods(l, bp, bs, ls), cfg_s, wts, l, st_s, (page_table, cache_k, cache_v))
        for lst, s in zip(col_p, new_p):
            lst.append(s)
        for lst, s in zip(col_s, new_s):
            lst.append(s)

    sp = [jnp.stack(s) for s in col_p]
    ss = [jnp.stack(s) for s in col_s]
    out = [xp.reshape(bp, lp, d), xs.reshape(bs, ls, d)]
    for a, b in zip(sp, ss):
        out += [a, b]
    return tuple(out)
```

```python
import functools
import math

import numpy as np
import jax
import jax.numpy as jnp
from jax import lax
from jax.experimental import pallas as pl
from jax.experimental.pallas import tpu as pltpu

F32 = jnp.float32
BF16 = jnp.bfloat16

D_MODEL = 2048
DEPTH = 2
PAGE_SIZE = 128
N_BRANCH = 4
BRANCH_WIDTH = 512
GLA_HEADS = 4
GLA_DK = 64
GLA_DV = 128
GLA_LOWRANK = 16
GLA_TAU = 16.0
S5_GROUP = 16
S5_GROUPS = 32
S5_N = 64
S5_STATES = S5_GROUPS * S5_N
LRU_HEADS = 4
LRU_BLOCK = 128
LRU_CONV = 4
LRU_C = 8.0
DA_HEADS = 4
DA_DK = 64
DA_DV = 128
ROPE_DIM = 16
ROPE_THETA = 500000.0
D_FF = 5632
FFN_CONV = 3
DEEPNORM_ALPHA = (2.0 * DEPTH) ** 0.25
EPS = 1e-5

LANES = 128
SUBLANES = 8
VMEM_LIMIT = 56 * 1024 * 1024

Z_GATE = N_BRANCH * D_MODEL
Z_MIX = 4608
COL_GQ, COL_GK, COL_GV, COL_OG = 0, 256, 512, 1024
COL_SU, COL_LX, COL_LG = 1536, 2048, 2560
COL_DQ, COL_DK, COL_DV = 3072, 3584, 4096

_NT = (((1,), (1,)), ((), ()))
_TN = (((0,), (0,)), ((), ()))


def _cparams(sem):
    return pltpu.CompilerParams(dimension_semantics=sem, vmem_limit_bytes=VMEM_LIMIT)


def _dot(a, b):
    return jnp.dot(a, b, preferred_element_type=F32)


def _sigmoid(x):
    return 0.5 * jnp.tanh(0.5 * x) + 0.5


def _gelu(x):
    return 0.5 * x * (1.0 + jnp.tanh(math.sqrt(2.0 / math.pi) * (x + 0.044715 * (x * x * x))))


def _layer_norm(y, g, b):
    mu = jnp.mean(y, axis=-1, keepdims=True)
    d = y - mu
    var = jnp.mean(d * d, axis=-1, keepdims=True)
    return d * lax.rsqrt(var + EPS) * g + b


def _ada_kernel(c_ref, w_ref, b_ref, o_ref):
    c = c_ref[...]
    s = c * _sigmoid(c)
    o_ref[...] = _dot(s.astype(BF16), w_ref[...].astype(BF16)) + b_ref[...]


def _ada(c_all, w_ada, b_ada):
    rows = c_all.shape[0]
    n = w_ada.shape[-1]
    tn = 1536
    return pl.pallas_call(
        _ada_kernel,
        grid=(DEPTH, n // tn),
        in_specs=[pl.BlockSpec((rows, D_MODEL), lambda l, j: (0, 0)),
                  pl.BlockSpec((None, D_MODEL, tn), lambda l, j: (l, 0, j)),
                  pl.BlockSpec((None, 1, tn), lambda l, j: (l, 0, j))],
        out_specs=pl.BlockSpec((None, rows, tn), lambda l, j: (l, 0, j)),
        out_shape=jax.ShapeDtypeStruct((DEPTH, rows, n), F32),
        compiler_params=_cparams(("arbitrary", "arbitrary")),
        name="ada",
    )(c_all, w_ada, b_ada.reshape(DEPTH, 1, n))


_WP_TILE = 512


def _wprep_kernel(a_ref, b_ref, o_ref, *, n_aligned, shift):
    j = pl.program_id(1)
    nv = _WP_TILE // LANES

    @pl.when(j < n_aligned)
    def _():
        o_ref[...] = a_ref[...].astype(BF16)

    @pl.when(j >= n_aligned)
    def _():
        lane = lax.broadcasted_iota(jnp.int32, (a_ref.shape[0], LANES), 1)
        cols = [a_ref[:, v * LANES:(v + 1) * LANES] for v in range(nv)] + [b_ref[...]]
        rolled = [pltpu.roll(c, LANES - shift, axis=1) for c in cols]
        for v in range(nv):
            o_ref[:, v * LANES:(v + 1) * LANES] = jnp.where(lane < LANES - shift, rolled[v],
                                                            rolled[v + 1]).astype(BF16)


def _wprep(w, first_tile, n_tiles, n_aligned, shift):
    depth, k, n = w.shape
    per = _WP_TILE // LANES
    last_b = (n - 1) // LANES
    kern = functools.partial(_wprep_kernel, n_aligned=n_aligned, shift=shift)
    return pl.pallas_call(
        kern,
        grid=(depth, n_tiles),
        in_specs=[pl.BlockSpec((None, k, _WP_TILE), lambda l, j: (l, 0, first_tile + j)),
                  pl.BlockSpec((None, k, LANES),
                               lambda l, j: (l, 0, jnp.minimum((first_tile + j + 1) * per, last_b)))],
        out_specs=pl.BlockSpec((None, k, _WP_TILE), lambda l, j: (l, 0, j)),
        out_shape=jax.ShapeDtypeStruct((depth, k, n_tiles * _WP_TILE), BF16),
        compiler_params=_cparams(("arbitrary", "arbitrary")),
        name="wprep",
    )(w, w)


def _inproj_kernel(x_ref, sc_ref, sh_ref, wg_ref, wm_ref, wlr_ref, zg_ref, zm_ref, zlr_ref, u_scr, *, ngate):
    j = pl.program_id(1)

    @pl.when(j == 0)
    def _():
        u = (x_ref[...] * (1.0 + sc_ref[...]) + sh_ref[...]).astype(BF16)
        u_scr[...] = u
        zlr_ref[...] = _dot(u, wlr_ref[...])

    @pl.when(j < ngate)
    def _():
        zg_ref[...] = _sigmoid(_dot(u_scr[...], wg_ref[...])).astype(BF16)

    @pl.when(j >= ngate)
    def _():
        zm_ref[...] = _dot(u_scr[...], wm_ref[...])


def _inproj(x2d, sc, sh, gdiv, w_gate, w_mix, w_lr, layer, tm, tng, tnm):
    m = x2d.shape[0]
    r = sc.shape[1]
    ngate, nmix = Z_GATE // tng, Z_MIX // tnm
    mod_spec = pl.BlockSpec((None, r, D_MODEL), lambda i, j: (i // gdiv, 0, 0))
    kern = functools.partial(_inproj_kernel, ngate=ngate)
    return pl.pallas_call(
        kern,
        grid=(m // tm, ngate + nmix),
        in_specs=[pl.BlockSpec((tm, D_MODEL), lambda i, j: (i, 0)), mod_spec, mod_spec,
                  pl.BlockSpec((None, D_MODEL, tng), lambda i, j: (layer, 0, jnp.minimum(j, ngate - 1))),
                  pl.BlockSpec((None, D_MODEL, tnm), lambda i, j: (layer, 0, jnp.maximum(j - ngate, 0))),
                  pl.BlockSpec((None, D_MODEL, LANES), lambda i, j: (layer, 0, 0))],
        out_specs=[pl.BlockSpec((tm, tng), lambda i, j: (i, jnp.minimum(j, ngate - 1))),
                   pl.BlockSpec((tm, tnm), lambda i, j: (i, jnp.maximum(j - ngate, 0))),
                   pl.BlockSpec((tm, LANES), lambda i, j: (i, 0))],
        out_shape=[jax.ShapeDtypeStruct((m, Z_GATE), BF16), jax.ShapeDtypeStruct((m, Z_MIX), F32),
                   jax.ShapeDtypeStruct((m, LANES), F32)],
        scratch_shapes=[pltpu.VMEM((tm, D_MODEL), BF16)],
        compiler_params=_cparams(("arbitrary", "arbitrary")),
        name="inproj",
    )(x2d, sc, sh, w_gate, w_mix, w_lr)


def _rope_tables(pos0, length):
    half = ROPE_DIM // 2
    inv = ROPE_THETA ** (-jnp.arange(half, dtype=F32) * 2.0 / ROPE_DIM)
    ang = (pos0 + jnp.arange(length)).astype(F32)[:, None] * inv
    cos, sin = jnp.cos(ang), jnp.sin(ang)
    ones = jnp.ones((length, DA_DK - ROPE_DIM), F32)
    zeros = jnp.zeros((length, DA_DK - ROPE_DIM), F32)
    zh = jnp.zeros((length, half), F32)
    c = jnp.concatenate([cos, cos, ones], axis=1)
    s_next = jnp.concatenate([-sin, zh, zeros], axis=1)
    s_prev = jnp.concatenate([zh, sin, zeros], axis=1)
    rep = LANES // DA_DK
    return jnp.tile(c, (1, rep)), jnp.tile(s_next, (1, rep)), jnp.tile(s_prev, (1, rep))


def _rope_kernel(q_ref, k_ref, c_ref, sn_ref, sp_ref, qo_ref, ko_ref):
    half = ROPE_DIM // 2
    c, sn, sp = c_ref[...], sn_ref[...], sp_ref[...]
    for src, dst in ((q_ref, qo_ref), (k_ref, ko_ref)):
        for i in range(src.shape[1] // LANES):
            x = src[:, i * LANES:(i + 1) * LANES]
            nxt = pltpu.roll(x, LANES - half, axis=1)
            prv = pltpu.roll(x, half, axis=1)
            dst[:, i * LANES:(i + 1) * LANES] = x * c + nxt * sn + prv * sp


def _rope(z, tabs, nb, length, tm):
    m = z.shape[0]
    w = DA_HEADS * 2 * DA_DK
    nj = length // tm
    tab_spec = pl.BlockSpec((tm, LANES), lambda b, j: (j, 0))
    return pl.pallas_call(
        _rope_kernel,
        grid=(nb, nj),
        in_specs=[pl.BlockSpec((tm, w), lambda b, j: (b * nj + j, COL_DQ // w)),
                  pl.BlockSpec((tm, w), lambda b, j: (b * nj + j, COL_DK // w)),
                  tab_spec, tab_spec, tab_spec],
        out_specs=[pl.BlockSpec((tm, w), lambda b, j: (b * nj + j, 0))] * 2,
        out_shape=[jax.ShapeDtypeStruct((m, w), F32)] * 2,
        compiler_params=_cparams(("arbitrary", "arbitrary")),
        name="rope",
    )(z, z, *tabs)


def _gla_consts(c, seq):
    nlev = int(math.log2(seq))
    t = np.arange(c)
    tt, rr = t[:, None], t[None, :]
    same = (tt // seq) == (rr // seq)
    tril = same & (rr <= tt)
    lvl = np.full((c, c), -1, np.int32)
    lvl[t, t] = nlev
    for i in range(nlev):
        b = seq >> (i + 1)
        valid = ((tt // (2 * b)) == (rr // (2 * b))) & ((tt % (2 * b)) >= b) & ((rr % (2 * b)) < b)
        lvl[valid] = i
    return tril.astype(np.float32), lvl, nlev


def _gla_kernel(q_ref, k_ref, v_ref, og_ref, lr_ref, s0_ref, lmat_ref, lvl_ref, wa2_ref, ba_ref, gn_ref,
                o_ref, st_ref, cum_scr, *, c, seq, nlev):
    nb = c // seq
    j = pl.program_id(1)

    @pl.when(j == 0)
    def _():
        st_ref[...] = s0_ref[...]

    x = _dot(lr_ref[...].astype(BF16), wa2_ref[...]) + ba_ref[...]
    la = (jnp.minimum(x, 0.0) - jnp.log1p(jnp.exp(-jnp.abs(x)))) * (1.0 / GLA_TAU)
    cum = jnp.dot(lmat_ref[...], la, precision=lax.Precision.HIGHEST, preferred_element_type=F32)
    cum_scr[...] = cum
    wq = cum.shape[1]

    def cum_at(idx, group):
        sub = lax.broadcasted_iota(jnp.int32, (SUBLANES, wq), 0)
        bcast = lambda r: jnp.broadcast_to(cum_scr[r:r + 1, :], (SUBLANES, wq))
        tiles = []
        for t0 in range(0, c, SUBLANES):
            tile = bcast(idx(t0))
            for g in range(group, SUBLANES, group):
                tile = jnp.where(sub >= g, bcast(idx(t0 + g)), tile)
            tiles.append(tile)
        return jnp.concatenate(tiles, axis=0)

    e_end = cum_at(lambda t: (t // seq) * seq + seq - 1, seq) - cum
    e_lvl = [cum - cum_at(lambda t, b=seq >> (i + 1): (t // (2 * b)) * (2 * b) + b - 1, 2 * (seq >> (i + 1)))
             for i in range(nlev)]
    q = q_ref[...] * (GLA_DK ** -0.5)
    k = k_ref[...]
    lvl = lvl_ref[...]
    lane = lax.broadcasted_iota(jnp.int32, (c, LANES), 1)
    if nb > 1:
        rowb = lax.broadcasted_iota(jnp.int32, (c, nb * LANES), 0) // seq
        colb = lax.broadcasted_iota(jnp.int32, (c, nb * LANES), 1) // LANES
        bmask = rowb == colb

    def spread(a):
        if nb == 1:
            return a
        return jnp.where(bmask, jnp.concatenate([a] * nb, axis=1), 0.0)

    for p in range(2):
        sl = slice(p * LANES, (p + 1) * LANES)
        qp, kp, cump = q[:, sl], k[:, sl], cum[:, sl]
        qts, kts = [], []
        for i in range(nlev):
            eb = e_lvl[i][:, sl]
            qts.append(qp * jnp.exp(eb))
            kts.append((kp * jnp.exp(-eb)).astype(BF16))
        qg = qp * jnp.exp(cump)
        kh = kp * jnp.exp(e_end[:, sl])
        kpb = kp.astype(BF16)
        decays = []
        for b in range(nb):
            cl = cump[b * seq + seq - 1:b * seq + seq, :]
            decays.append(jnp.broadcast_to(jnp.exp(cl), (LANES, LANES)).T)
        for hh in range(2):
            h = 2 * p + hh
            hm = (lane // GLA_DK) == hh
            att = lax.dot_general(jnp.where(hm, qp, 0.0).astype(BF16), kpb, _NT, preferred_element_type=F32)
            att = jnp.where(lvl == nlev, att, 0.0)
            for i in range(nlev):
                a = lax.dot_general(jnp.where(hm, qts[i], 0.0).astype(BF16), kts[i], _NT,
                                    preferred_element_type=F32)
                att = jnp.where(lvl == i, a, att)
            vh = v_ref[:, h * GLA_DV:(h + 1) * GLA_DV].astype(BF16)
            o = _dot(att.astype(BF16), vh)
            s_all = st_ref[:, h].reshape(nb * LANES, GLA_DV)
            o = o + _dot(spread(jnp.where(hm, qg, 0.0)).astype(BF16), s_all.astype(BF16))
            upd = lax.dot_general(spread(jnp.where(hm, kh, 0.0)).astype(BF16), vh, _TN,
                                  preferred_element_type=F32)
            for b in range(nb):
                st_ref[b, h] = st_ref[b, h] * decays[b] + upd[b * LANES:(b + 1) * LANES]
            ms = jnp.mean(o * o, axis=-1, keepdims=True)
            og = og_ref[:, h * GLA_DV:(h + 1) * GLA_DV]
            o_ref[:, h * GLA_DV:(h + 1) * GLA_DV] = o * lax.rsqrt(ms + EPS) * gn_ref[...] * (og * _sigmoid(og))


def _gla(z, zlr, s0_pad, wa2p, ba, gn, ng, length, c, seq):
    m = z.shape[0]
    nb = c // seq
    nj = length // c
    lmat, lvl, nlev = _gla_consts(c, seq)
    w = GLA_HEADS * GLA_DK
    wv = GLA_HEADS * GLA_DV
    row = lambda g, j: g * nj + j
    const2 = lambda g, j: (0, 0)
    st_spec = pl.BlockSpec((nb, GLA_HEADS, LANES, GLA_DV), lambda g, j: (g, 0, 0, 0))
    kern = functools.partial(_gla_kernel, c=c, seq=seq, nlev=nlev)
    return pl.pallas_call(
        kern,
        grid=(ng, nj),
        in_specs=[pl.BlockSpec((c, w), lambda g, j: (row(g, j), COL_GQ // w)),
                  pl.BlockSpec((c, w), lambda g, j: (row(g, j), COL_GK // w)),
                  pl.BlockSpec((c, wv), lambda g, j: (row(g, j), COL_GV // wv)),
                  pl.BlockSpec((c, wv), lambda g, j: (row(g, j), COL_OG // wv)),
                  pl.BlockSpec((c, LANES), lambda g, j: (row(g, j), 0)),
                  st_spec,
                  pl.BlockSpec(lmat.shape, const2),
                  pl.BlockSpec(lvl.shape, const2),
                  pl.BlockSpec((LANES, w), const2),
                  pl.BlockSpec((1, w), const2),
                  pl.BlockSpec((1, GLA_DV), const2)],
        out_specs=[pl.BlockSpec((c, wv), lambda g, j: (row(g, j), 0)), st_spec],
        out_shape=[jax.ShapeDtypeStruct((m, wv), F32),
                   jax.ShapeDtypeStruct(s0_pad.shape, F32)],
        scratch_shapes=[pltpu.VMEM((c, w), F32)],
        compiler_params=_cparams(("arbitrary", "arbitrary")),
        name="gla",
    )(z, z, z, z, zlr, s0_pad, jnp.asarray(lmat), jnp.asarray(lvl), wa2p, ba, gn)


def _gla_state_pad(s):
    b = s.shape[0]
    s6 = s.reshape(b, 2, 2, 1, GLA_DK, GLA_DV) * jnp.eye(2, dtype=F32)[None, None, :, :, None, None]
    return s6.reshape(b, GLA_HEADS, LANES, GLA_DV)


def _gla_state_unpad(sp):
    b = sp.shape[0]
    s6 = sp.reshape(b, 2, 2, 2, GLA_DK, GLA_DV)
    return jnp.stack([s6[:, :, 0, 0], s6[:, :, 1, 1]], axis=2).reshape(b, GLA_HEADS, GLA_DK, GLA_DV)


_S5_CH = S5_STATES // LANES


def _s5_kernel(u_ref, h0r_ref, h0i_ref, are_ref, aim_ref, ldt_ref, bdr_ref, bdi_ref, cdr_ref, cdi_ref,
               d_ref, wg_ref, bg_ref, y_ref, hfr_ref, hfi_ref, sre, sim, hsc, hcr, hci, *, nseq, sl):
    tm = nseq * sl
    w = BRANCH_WIDTH
    j = pl.program_id(0)

    @pl.when(j == 0)
    def _():
        hcr[...] = h0r_ref[...]
        hci[...] = h0i_ref[...]

    lr, li = are_ref[...], aim_ref[...]
    dt = jnp.exp(ldt_ref[...])
    mag = jnp.exp(lr * dt)
    abr, abi = mag * jnp.cos(li * dt), mag * jnp.sin(li * dt)
    den = lr * lr + li * li
    fr = ((abr - 1.0) * lr + abi * li) / den
    fi = (abi * lr - (abr - 1.0) * li) / den

    u = u_ref[...].reshape(tm, w)
    ub = u.astype(BF16)
    for c2 in range(_S5_CH // 2):
        ws = slice((c2 // 2) * LANES, (c2 // 2 + 1) * LANES)
        tr = _dot(ub[:, ws], bdr_ref[ws, c2 * 256:(c2 + 1) * 256])
        ti = _dot(ub[:, ws], bdi_ref[ws, c2 * 256:(c2 + 1) * 256])
        for e in range(2):
            ch = 2 * c2 + e
            trc, tic = tr[:, e * LANES:(e + 1) * LANES], ti[:, e * LANES:(e + 1) * LANES]
            frc, fic = fr[ch:ch + 1, :], fi[ch:ch + 1, :]
            sre[ch * tm:(ch + 1) * tm, :] = frc * trc - fic * tic
            sim[ch * tm:(ch + 1) * tm, :] = frc * tic + fic * trc

    ar0, ai0, ar1, ai1 = abr[0:8], abi[0:8], abr[8:16], abi[8:16]
    grp = min(nseq, 4)
    for g0 in range(0, nseq, grp):
        def body(t, carry, g0=g0):
            out = []
            for q in range(grp):
                r0, i0, r1, i1 = carry[4 * q:4 * q + 4]
                row = (g0 + q) * sl + t
                lo = pl.ds(row, SUBLANES, stride=tm)
                hi = pl.ds(SUBLANES * tm + row, SUBLANES, stride=tm)
                n_r0 = ar0 * r0 - ai0 * i0 + sre[lo, :]
                n_i0 = ar0 * i0 + ai0 * r0 + sim[lo, :]
                n_r1 = ar1 * r1 - ai1 * i1 + sre[hi, :]
                n_i1 = ar1 * i1 + ai1 * r1 + sim[hi, :]
                dst = pl.ds(pl.multiple_of(row * SUBLANES, SUBLANES), SUBLANES)
                hsc[0, dst, :] = n_r0
                hsc[1, dst, :] = n_i0
                hsc[2, dst, :] = n_r1
                hsc[3, dst, :] = n_i1
                out += [n_r0, n_i0, n_r1, n_i1]
            return tuple(out)

        init = tuple(a for q in range(grp) for a in (hcr[g0 + q, 0:8], hci[g0 + q, 0:8],
                                                     hcr[g0 + q, 8:16], hci[g0 + q, 8:16]))
        fin = lax.fori_loop(0, sl, body, init, unroll=8)
        for q in range(grp):
            hcr[g0 + q, 0:8] = fin[4 * q]
            hci[g0 + q, 0:8] = fin[4 * q + 1]
            hcr[g0 + q, 8:16] = fin[4 * q + 2]
            hci[g0 + q, 8:16] = fin[4 * q + 3]

    ywin = []
    for wi in range(w // LANES):
        ws = slice(wi * LANES, (wi + 1) * LANES)
        yw = d_ref[:, ws] * u[:, ws]
        for c2 in (2 * wi, 2 * wi + 1):
            part = 2 * ((2 * c2) // SUBLANES)
            ca, cb = (2 * c2) % SUBLANES, (2 * c2 + 1) % SUBLANES
            rows = lambda cc: pl.ds(cc, tm, stride=SUBLANES)
            hr, hi = hsc.at[part], hsc.at[part + 1]
            hre = jnp.concatenate([hr[rows(ca), :], hr[rows(cb), :]], axis=1).astype(BF16)
            him = jnp.concatenate([hi[rows(ca), :], hi[rows(cb), :]], axis=1).astype(BF16)
            yw = yw + _dot(hre, cdr_ref[c2 * 256:(c2 + 1) * 256, ws]) - _dot(him, cdi_ref[c2 * 256:(c2 + 1) * 256, ws])
        ywin.append(yw)
    y = _gelu(jnp.concatenate(ywin, axis=1))
    y = y * _sigmoid(_dot(y.astype(BF16), wg_ref[...]) + bg_ref[...])
    y_ref[...] = y.reshape(nseq, sl, w)

    @pl.when(j == pl.num_programs(0) - 1)
    def _():
        hfr_ref[...] = hcr[...]
        hfi_ref[...] = hci[...]


def _s5(z3, h0r, h0i, p, sl):
    nseq, length, _ = z3.shape
    tm = nseq * sl
    w = BRANCH_WIDTH
    const2 = lambda j: (0, 0)
    st_spec = pl.BlockSpec((nseq, _S5_CH, LANES), lambda j: (0, 0, 0))
    vec = pl.BlockSpec((_S5_CH, LANES), const2)
    kern = functools.partial(_s5_kernel, nseq=nseq, sl=sl)
    return pl.pallas_call(
        kern,
        grid=(length // sl,),
        in_specs=[pl.BlockSpec((nseq, sl, w), lambda j: (0, j, COL_SU // w)),
                  st_spec, st_spec, vec, vec, vec,
                  pl.BlockSpec((w, S5_STATES), const2), pl.BlockSpec((w, S5_STATES), const2),
                  pl.BlockSpec((S5_STATES, w), const2), pl.BlockSpec((S5_STATES, w), const2),
                  pl.BlockSpec((1, w), const2), pl.BlockSpec((w, w), const2), pl.BlockSpec((1, w), const2)],
        out_specs=[pl.BlockSpec((nseq, sl, w), lambda j: (0, j, 0)), st_spec, st_spec],
        out_shape=[jax.ShapeDtypeStruct((nseq, length, w), F32),
                   jax.ShapeDtypeStruct(h0r.shape, F32), jax.ShapeDtypeStruct(h0i.shape, F32)],
        scratch_shapes=[pltpu.VMEM((_S5_CH * tm, LANES), F32), pltpu.VMEM((_S5_CH * tm, LANES), F32),
                        pltpu.VMEM((4, SUBLANES * tm, LANES), F32),
                        pltpu.VMEM((nseq, _S5_CH, LANES), F32), pltpu.VMEM((nseq, _S5_CH, LANES), F32)],
        compiler_params=_cparams(("arbitrary",)),
        name="s5",
    )(z3, h0r, h0i, p["are"], p["aim"], p["ldt"], p["bdr"], p["bdi"], p["cdr"], p["cdi"],
      p["d"], p["wg"], p["bg"])


def _s5_params(a_re, a_im, log_dt, b_re, b_im, c_re, c_im, d, w_glu, b_glu):
    eye = jnp.eye(S5_GROUPS, dtype=F32)

    def bd_in(b):
        return (eye[:, None, :, None] * jnp.swapaxes(b, 1, 2)[:, :, None, :]).reshape(
            S5_GROUPS * S5_GROUP, S5_STATES).astype(BF16)

    def bd_out(c):
        return (eye[:, None, :, None] * jnp.swapaxes(c, 1, 2)[:, :, None, :]).reshape(
            S5_STATES, S5_GROUPS * S5_GROUP).astype(BF16)

    return dict(are=a_re.reshape(_S5_CH, LANES), aim=a_im.reshape(_S5_CH, LANES),
                ldt=jnp.broadcast_to(log_dt[:, None], (S5_GROUPS, S5_N)).reshape(_S5_CH, LANES),
                bdr=bd_in(b_re), bdi=bd_in(b_im), cdr=bd_out(c_re), cdi=bd_out(c_im),
                d=d.reshape(1, -1), wg=w_glu.astype(BF16), bg=b_glu.reshape(1, -1))


_HALO = SUBLANES


def _lru_kernel(lx_ref, lg_ref, zprev_ref, cst_ref, h0_ref, cw_ref, cb_ref, wa_ref, ba_ref, wx_ref, bx_ref,
                lam_ref, y_ref, hfin_ref, cs_scr, xc_scr, a_scr, b_scr, h_c, *, nseq, sl):
    tm = nseq * sl
    j = pl.program_id(0)

    @pl.when(j == 0)
    def _():
        h_c[...] = h0_ref[...]

    k = LRU_CONV
    for s in range(nseq):
        cs_scr[0:_HALO, :] = jnp.where(j == 0, cst_ref[s], zprev_ref[s])
        cs_scr[_HALO:_HALO + sl, :] = lx_ref[s]
        xc = cb_ref[...]
        for t in range(k):
            off = _HALO - (k - 1) + t
            xc = xc + cw_ref[t:t + 1, :] * cs_scr[off:off + sl, :]
        xc_scr[s * sl:(s + 1) * sl, :] = xc

    lam = lam_ref[...]
    sp = jnp.maximum(-lam, 0.0) + jnp.log1p(jnp.exp(-jnp.abs(lam)))
    for h in range(LRU_HEADS):
        hs = slice(h * LRU_BLOCK, (h + 1) * LRU_BLOCK)
        xc = xc_scr[:, hs]
        xb = xc.astype(BF16)
        r = _sigmoid(_dot(xb, wa_ref[h]) + ba_ref[:, hs])
        i = _sigmoid(_dot(xb, wx_ref[h]) + bx_ref[:, hs])
        la = -LRU_C * r * sp[:, hs]
        a_scr[:, hs] = jnp.exp(la)
        b_scr[:, hs] = jnp.sqrt(-jnp.tanh(la) * (jnp.exp(2.0 * la) + 1.0)) * (i * xc)

    grp = min(nseq, 4)
    for g0 in range(0, nseq, grp):
        def body(t, hs, g0=g0):
            out = []
            for q in range(grp):
                idx = pl.ds((g0 + q) * sl + t, 1)
                h = a_scr[idx, :] * hs[q] + b_scr[idx, :]
                b_scr[idx, :] = h
                out.append(h)
            return tuple(out)

        fin = lax.fori_loop(0, sl, body, tuple(h_c[g0 + q:g0 + q + 1, :] for q in range(grp)), unroll=8)
        for q in range(grp):
            h_c[g0 + q:g0 + q + 1, :] = fin[q]

    y = b_scr[...] * _gelu(lg_ref[...].reshape(tm, BRANCH_WIDTH))
    y_ref[...] = y.reshape(nseq, sl, BRANCH_WIDTH)

    @pl.when(j == pl.num_programs(0) - 1)
    def _():
        hfin_ref[...] = h_c[...]


def _lru(z3, cst, h0, p, sl):
    nseq, length, _ = z3.shape
    tm = nseq * sl
    w = BRANCH_WIDTH
    const2 = lambda j: (0, 0)
    const3 = lambda j: (0, 0, 0)
    st_spec = pl.BlockSpec((nseq, w), const2)
    vec = pl.BlockSpec((1, w), const2)
    wsp = pl.BlockSpec((LRU_HEADS, LRU_BLOCK, LRU_BLOCK), const3)
    rb = sl // _HALO
    kern = functools.partial(_lru_kernel, nseq=nseq, sl=sl)
    return pl.pallas_call(
        kern,
        grid=(length // sl,),
        in_specs=[pl.BlockSpec((nseq, sl, w), lambda j: (0, j, COL_LX // w)),
                  pl.BlockSpec((nseq, sl, w), lambda j: (0, j, COL_LG // w)),
                  pl.BlockSpec((nseq, _HALO, w), lambda j: (0, jnp.maximum(j * rb - 1, 0), COL_LX // w)),
                  pl.BlockSpec((nseq, _HALO, w), const3),
                  st_spec,
                  pl.BlockSpec((LRU_CONV, w), const2), vec, wsp, vec, wsp, vec, vec],
        out_specs=[pl.BlockSpec((nseq, sl, w), lambda j: (0, j, 0)), st_spec],
        out_shape=[jax.ShapeDtypeStruct((nseq, length, w), F32), jax.ShapeDtypeStruct(h0.shape, F32)],
        scratch_shapes=[pltpu.VMEM((_HALO + sl, w), F32), pltpu.VMEM((tm, w), F32),
                        pltpu.VMEM((tm, w), F32), pltpu.VMEM((tm, w), F32),
                        pltpu.VMEM((nseq, w), F32)],
        compiler_params=_cparams(("arbitrary",)),
        name="lru",
    )(z3, z3, z3, cst, h0, p["cw"], p["cb"], p["wa"], p["ba"], p["wx"], p["bx"], p["lam"])


def _da_lambda(lq1_ref, lk1_ref, lq2_ref, lk2_ref, lam_init):
    return (jnp.exp(jnp.sum(lq1_ref[...] * lk1_ref[...])) - jnp.exp(jnp.sum(lq2_ref[...] * lk2_ref[...]))
            + lam_init)


def _attn_kernel(q_ref, k_ref, v_ref, lq1_ref, lk1_ref, lq2_ref, lk2_ref, dn_ref, o_ref,
                 m_scr, l_scr, acc_scr, *, tq, tk, lam_init):
    qi, ki = pl.program_id(2), pl.program_id(3)

    @pl.when(ki == 0)
    def _():
        m_scr[...] = jnp.full(m_scr.shape, -jnp.inf, F32)
        l_scr[...] = jnp.zeros(l_scr.shape, F32)
        acc_scr[...] = jnp.zeros(acc_scr.shape, F32)

    def step(masked):
        q = q_ref[...] * (DA_DK ** -0.5)
        kb = k_ref[...].astype(BF16)
        vb = v_ref[...].astype(BF16)
        lane = lax.broadcasted_iota(jnp.int32, (tq, LANES), 1)
        if masked:
            row = qi * tq + lax.broadcasted_iota(jnp.int32, (tq, tk), 0)
            col = ki * tk + lax.broadcasted_iota(jnp.int32, (tq, tk), 1)
            mask = col <= row
        comps = range(2)
        ss = [lax.dot_general(jnp.where((lane // DA_DK) == c, q, 0.0).astype(BF16), kb, _NT,
                              preferred_element_type=F32) for c in comps]
        if masked:
            ss = [jnp.where(mask, s, -jnp.inf) for s in ss]
        m_prev = [m_scr[c] for c in comps]
        l_prev = [l_scr[c] for c in comps]
        a_prev = [acc_scr[c] for c in comps]
        m_new = [jnp.maximum(m_prev[c], jnp.max(ss[c], axis=-1, keepdims=True)) for c in comps]
        alpha = [jnp.exp(m_prev[c] - m_new[c]) for c in comps]
        pexp = [jnp.exp(ss[c] - m_new[c]) for c in comps]
        pv = [_dot(pexp[c].astype(BF16), vb) for c in comps]
        for c in comps:
            l_scr[c] = alpha[c] * l_prev[c] + jnp.sum(pexp[c], axis=-1, keepdims=True)
            acc_scr[c] = alpha[c] * a_prev[c] + pv[c]
            m_scr[c] = m_new[c]

    last_col, first_col = ki * tk + tk - 1, ki * tk
    first_row, last_row = qi * tq, qi * tq + tq - 1

    @pl.when(last_col <= first_row)
    def _():
        step(False)

    @pl.when((last_col > first_row) & (first_col <= last_row))
    def _():
        step(True)

    @pl.when(ki == pl.num_programs(3) - 1)
    def _():
        lam = _da_lambda(lq1_ref, lk1_ref, lq2_ref, lk2_ref, lam_init)
        o = acc_scr[0] / l_scr[0] - lam * (acc_scr[1] / l_scr[1])
        ms = jnp.mean(o * o, axis=-1, keepdims=True)
        o_ref[...] = o * lax.rsqrt(ms + EPS) * dn_ref[...] * (1.0 - lam_init)


def _attn_prompt(q_rot, k_rot, z, lam_p, dn, nb, length, tq, tk, lam_init):
    m = q_rot.shape[0]
    nq, nk = length // tq, length // tk
    const2 = lambda b, h, qi, ki: (0, 0)
    kv_row = lambda b, qi, ki: b * nk + jnp.minimum(ki, (qi * tq + tq - 1) // tk)
    vec = pl.BlockSpec((1, DA_DK), const2)
    kern = functools.partial(_attn_kernel, tq=tq, tk=tk, lam_init=lam_init)
    return pl.pallas_call(
        kern,
        grid=(nb, DA_HEADS, nq, nk),
        in_specs=[pl.BlockSpec((tq, LANES), lambda b, h, qi, ki: (b * nq + qi, h)),
                  pl.BlockSpec((tk, LANES), lambda b, h, qi, ki: (kv_row(b, qi, ki), h)),
                  pl.BlockSpec((tk, DA_DV), lambda b, h, qi, ki: (kv_row(b, qi, ki), COL_DV // DA_DV + h)),
                  vec, vec, vec, vec, pl.BlockSpec((1, DA_DV), const2)],
        out_specs=pl.BlockSpec((tq, DA_DV), lambda b, h, qi, ki: (b * nq + qi, h)),
        out_shape=jax.ShapeDtypeStruct((m, DA_HEADS * DA_DV), F32),
        scratch_shapes=[pltpu.VMEM((2, tq, 1), F32), pltpu.VMEM((2, tq, 1), F32),
                        pltpu.VMEM((2, tq, DA_DV), F32)],
        compiler_params=_cparams(("arbitrary", "arbitrary", "arbitrary", "arbitrary")),
        name="attn_prompt",
    )(q_rot, k_rot, z, *lam_p, dn)


_PAGES_PER_STEP = 16
_QROWS = DA_HEADS * 2 * SUBLANES


def _attn_sample_kernel(pt_ref, q_ref, *refs, lq, lam_init):
    npg = _PAGES_PER_STEP
    kt_refs, v_refs = refs[0:npg], refs[npg:2 * npg]
    kn_ref, vn_ref, lq1_ref, lk1_ref, lq2_ref, lk2_ref, dn_ref = refs[2 * npg:2 * npg + 7]
    o_ref = refs[2 * npg + 7]
    qbd_scr, m_scr, l_scr, acc_scr = refs[2 * npg + 8:]
    w = DA_HEADS * 2 * DA_DK
    hrows = 2 * lq
    j = pl.program_id(1)

    @pl.when(j == 0)
    def _():
        q = q_ref[...] * (DA_DK ** -0.5)
        lane = lax.broadcasted_iota(jnp.int32, (lq, w), 1)
        tiles = [jnp.where((lane // DA_DK) == hc, q, 0.0) for hc in range(DA_HEADS * 2)]
        qbd_scr[...] = jnp.concatenate(tiles, axis=0)
        m_scr[...] = jnp.full(m_scr.shape, -jnp.inf, F32)
        l_scr[...] = jnp.zeros(l_scr.shape, F32)
        acc_scr[...] = jnp.zeros(acc_scr.shape, F32)

    qbd = qbd_scr[...].astype(BF16)

    def update(s, nchunk, vget):
        m_prev = m_scr[...]
        m_new = jnp.maximum(m_prev, jnp.max(s, axis=-1, keepdims=True))
        alpha = jnp.exp(m_prev - m_new)
        pexp = jnp.exp(s - m_new)
        l_scr[...] = alpha * l_scr[...] + jnp.sum(pexp, axis=-1, keepdims=True)
        pb = pexp.astype(BF16)
        for h in range(DA_HEADS):
            rs = slice(h * hrows, (h + 1) * hrows)
            pv = _dot(pb[rs, 0:PAGE_SIZE], vget(0, h))
            for i in range(1, nchunk):
                pv = pv + _dot(pb[rs, i * PAGE_SIZE:(i + 1) * PAGE_SIZE], vget(i, h))
            acc_scr[rs, :] = alpha[rs] * acc_scr[rs, :] + pv
        m_scr[...] = m_new

    s_pages = jnp.concatenate([_dot(qbd, kt_refs[i][...].astype(BF16)) for i in range(npg)], axis=1)
    update(s_pages, npg, lambda i, h: v_refs[i][pl.ds(h, PAGE_SIZE, stride=DA_HEADS), :].astype(BF16))

    @pl.when(j == pl.num_programs(1) - 1)
    def _():
        nk = kn_ref.shape[0]
        t = lax.broadcasted_iota(jnp.int32, (_QROWS, nk), 0) % lq
        col = lax.broadcasted_iota(jnp.int32, (_QROWS, nk), 1)
        s_new = lax.dot_general(qbd, kn_ref[...].astype(BF16), _NT, preferred_element_type=F32)
        update(jnp.where(col <= t, s_new, -jnp.inf), 1,
               lambda i, h: vn_ref[:, h * DA_DV:(h + 1) * DA_DV].astype(BF16))
        lam = _da_lambda(lq1_ref, lk1_ref, lq2_ref, lk2_ref, lam_init)
        for h in range(DA_HEADS):
            r0 = h * hrows
            o0 = acc_scr[r0:r0 + lq, :] / l_scr[r0:r0 + lq]
            o1 = acc_scr[r0 + lq:r0 + 2 * lq, :] / l_scr[r0 + lq:r0 + 2 * lq]
            o = o0 - lam * o1
            ms = jnp.mean(o * o, axis=-1, keepdims=True)
            o_ref[:, h * DA_DV:(h + 1) * DA_DV] = o * lax.rsqrt(ms + EPS) * dn_ref[...] * (1.0 - lam_init)


def _attn_sample(q_rot, cache_k, cache_v, layer, page_table, k_new, v_new, lam_p, dn, lam_init):
    nb, n_pages = page_table.shape
    lq = q_rot.shape[0] // nb
    w = DA_HEADS * 2 * DA_DK
    wv = DA_HEADS * DA_DV
    npg = _PAGES_PER_STEP
    nsteps = n_pages // npg
    ckt = jnp.transpose(cache_k, (0, 1, 3, 4, 5, 2)).reshape(DEPTH, -1, w, PAGE_SIZE)
    cv2 = cache_v.reshape(DEPTH, -1, PAGE_SIZE * DA_HEADS, DA_DV)
    const2 = lambda b, j, pt: (0, 0)
    page = lambda b, j, pt, i: pt[b * n_pages + j * npg + i]

    def k_spec(i):
        return pl.BlockSpec((None, None, w, PAGE_SIZE), lambda b, j, pt: (layer, page(b, j, pt, i), 0, 0))

    def v_spec(i):
        return pl.BlockSpec((None, None, PAGE_SIZE * DA_HEADS, DA_DV),
                            lambda b, j, pt: (layer, page(b, j, pt, i), 0, 0))

    vec = pl.BlockSpec((1, DA_DK), const2)
    nk = k_new.shape[1]
    grid_spec = pltpu.PrefetchScalarGridSpec(
        num_scalar_prefetch=1,
        grid=(nb, nsteps),
        in_specs=([pl.BlockSpec((lq, w), lambda b, j, pt: (b, 0))]
                  + [k_spec(i) for i in range(npg)] + [v_spec(i) for i in range(npg)]
                  + [pl.BlockSpec((None, nk, w), lambda b, j, pt: (b, 0, 0)),
                     pl.BlockSpec((None, nk, wv), lambda b, j, pt: (b, 0, 0)),
                     vec, vec, vec, vec, pl.BlockSpec((1, DA_DV), const2)]),
        out_specs=pl.BlockSpec((lq, wv), lambda b, j, pt: (b, 0)),
        scratch_shapes=[pltpu.VMEM((_QROWS, w), F32), pltpu.VMEM((_QROWS, 1), F32),
                        pltpu.VMEM((_QROWS, 1), F32), pltpu.VMEM((_QROWS, DA_DV), F32)],
    )
    kern = functools.partial(_attn_sample_kernel, lq=lq, lam_init=lam_init)
    return pl.pallas_call(
        kern,
        grid_spec=grid_spec,
        out_shape=jax.ShapeDtypeStruct((nb * lq, wv), F32),
        compiler_params=_cparams(("arbitrary", "arbitrary")),
        name="attn_sample",
    )(page_table.reshape(-1), q_rot, *([ckt] * npg), *([cv2] * npg), k_new, v_new, *lam_p, dn)


def _merge_kernel(o0_ref, o1_ref, o2_ref, o3_ref, g_ref, wb_ref, wo_ref, x_ref, g1_ref, lng_ref, lnb_ref,
                  out_ref, acc_scr):
    b = pl.program_id(1)

    @pl.when(b == 0)
    def _():
        acc_scr[...] = jnp.zeros(acc_scr.shape, F32)

    o = jnp.where(b == 0, o0_ref[...], jnp.where(b == 1, o1_ref[...], jnp.where(b == 2, o2_ref[...], o3_ref[...])))
    acc_scr[...] += g_ref[...].astype(F32) * _dot(o.astype(BF16), wb_ref[...])

    @pl.when(b == N_BRANCH - 1)
    def _():
        mix = _dot(acc_scr[...].astype(BF16), wo_ref[...])
        y = DEEPNORM_ALPHA * x_ref[...] + g1_ref[...] * mix
        out_ref[...] = _layer_norm(y, lng_ref[...], lnb_ref[...])


def _merge(branches, gates, w_branch, w_out, layer, x2d, g1, gdiv, ln_g, ln_b, tm):
    m = x2d.shape[0]
    d = D_MODEL
    r = g1.shape[1]
    const2 = lambda i, b: (0, 0)
    osp = pl.BlockSpec((tm, BRANCH_WIDTH), lambda i, b: (i, 0))
    return pl.pallas_call(
        _merge_kernel,
        grid=(m // tm, N_BRANCH),
        in_specs=[osp, osp, osp, osp,
                  pl.BlockSpec((tm, d), lambda i, b: (i, b)),
                  pl.BlockSpec((None, None, BRANCH_WIDTH, d), lambda i, b: (layer, b, 0, 0)),
                  pl.BlockSpec((None, d, d), lambda i, b: (layer, 0, 0), pipeline_mode=pl.Buffered(1)),
                  pl.BlockSpec((tm, d), lambda i, b: (i, 0)),
                  pl.BlockSpec((None, r, d), lambda i, b: (i // gdiv, 0, 0)),
                  pl.BlockSpec((1, d), const2), pl.BlockSpec((1, d), const2)],
        out_specs=pl.BlockSpec((tm, d), lambda i, b: (i, 0)),
        out_shape=jax.ShapeDtypeStruct((m, d), F32),
        scratch_shapes=[pltpu.VMEM((tm, d), F32)],
        compiler_params=_cparams(("arbitrary", "arbitrary")),
        name="merge",
    )(*branches, gates, w_branch, w_out, x2d, g1, ln_g, ln_b)


_FHALO = 2 * SUBLANES


def _ffn_kernel(x_ref, xprev_ref, sc_ref, sh_ref, g2_ref, csta_ref, cstv_ref, wa_ref, wv_ref, cw_ref, cb_ref,
                wd_ref, lng_ref, lnb_ref, out_ref, sta_ref, stv_ref, u_scr, hsa, hsv, act_scr, acc_scr,
                *, nseq, sl, first_div, rc):
    tm = nseq * sl
    halo = _FHALO if nseq == 1 else 0
    seg = _FHALO + sl
    i, j = pl.program_id(0), pl.program_id(1)
    nf = pl.num_programs(1)

    @pl.when(j == 0)
    def _():
        sc, sh = 1.0 + sc_ref[...], sh_ref[...]
        if halo:
            u_scr[0:halo, :] = (xprev_ref[...] * sc + sh).astype(BF16)
        u_scr[halo:halo + tm, :] = (x_ref[...] * sc + sh).astype(BF16)
        acc_scr[...] = jnp.zeros(acc_scr.shape, F32)

    parts = ((wa_ref, csta_ref, hsa, sta_ref), (wv_ref, cstv_ref, hsv, stv_ref))
    if nseq == 1:
        half = tm // 2
        spans = ((0, half), (half, tm))
        first = i % first_div == 0
        for lo, hi in ((0, _FHALO + half), (_FHALO + half, _FHALO + tm)):
            for w_ref, cst_ref, hs, _ in parts:
                hs[lo:hi, :] = _dot(u_scr[lo:hi, :], w_ref[...])
                if lo == 0:
                    hs[_FHALO - _HALO:_FHALO, :] = jnp.where(first, cst_ref[0], hs[_FHALO - _HALO:_FHALO, :])
    else:
        spans = ((0, tm),)
        u = u_scr[...]
        for w_ref, cst_ref, hs, _ in parts:
            hup = _dot(u, w_ref[...])
            for s in range(nseq):
                hs[s * seg + _FHALO - _HALO:s * seg + _FHALO, :] = cst_ref[s]
                hs[s * seg + _FHALO:(s + 1) * seg, :] = hup[s * sl:(s + 1) * sl]
    for _, _, hs, st_ref in parts:
        for s in range(nseq):
            st_ref[s] = hs[(s + 1) * seg - _HALO:(s + 1) * seg, :]

    k = FFN_CONV
    base = _FHALO - (k - 1)
    for lo, hi in spans:
        for r0 in range(lo, hi, rc):
            s, t0 = divmod(r0, sl)
            o = s * seg + base + t0
            ya, yv = cb_ref[0], cb_ref[1]
            for t in range(k):
                ya = ya + cw_ref[0, t:t + 1, :] * hsa[o + t:o + t + rc, :]
                yv = yv + cw_ref[1, t:t + 1, :] * hsv[o + t:o + t + rc, :]
            act_scr[r0:r0 + rc, :] = (_gelu(ya) * yv).astype(act_scr.dtype)
        acc_scr[lo:hi, :] += _dot(act_scr[lo:hi, :].astype(BF16), wd_ref[...])

    @pl.when(j == nf - 1)
    def _():
        y = DEEPNORM_ALPHA * x_ref[...] + g2_ref[...] * acc_scr[...]
        out_ref[...] = _layer_norm(y, lng_ref[...], lnb_ref[...])


_FFN_TF = 512


def _tile_major(w, tn):
    depth, k, n = w.shape
    return w.reshape(depth, k, n // tn, tn).transpose(0, 2, 1, 3)


def _ffn(x2d, sc, sh, g2, gdiv, cst, w_up, conv_w, conv_b, w_down, layer, ln_g, ln_b, nseq, sl, first_div, tf):
    m, d = x2d.shape
    tm = nseq * sl
    nblk = m // tm
    nf = D_FF // tf
    r = sc.shape[1]
    halo = _FHALO if nseq == 1 else 0
    const2 = lambda i, j: (0, 0)
    mod_spec = pl.BlockSpec((None, r, d), lambda i, j: (i // gdiv, 0, 0))
    rb = tm // _FHALO
    cw2 = conv_w.reshape(FFN_CONV, 2, D_FF).transpose(1, 0, 2)
    cb2 = conv_b.reshape(2, 1, D_FF)
    csta, cstv = cst[..., :D_FF], cst[..., D_FF:]
    cst_spec = pl.BlockSpec((None, nseq, _HALO, tf), lambda i, j: (i // first_div, 0, 0, j))
    st_spec = pl.BlockSpec((None, nseq, _HALO, tf), lambda i, j: (i, 0, 0, j))
    st_shape = jax.ShapeDtypeStruct((nblk, nseq, _HALO, D_FF), F32)
    rc = 32 if sl % 32 == 0 else sl
    act_dtype = BF16 if rc % (2 * SUBLANES) == 0 else F32
    kern = functools.partial(_ffn_kernel, nseq=nseq, sl=sl, first_div=first_div, rc=rc)
    return pl.pallas_call(
        kern,
        grid=(nblk, nf),
        in_specs=[pl.BlockSpec((tm, d), lambda i, j: (i, 0)),
                  pl.BlockSpec((_FHALO, d), lambda i, j: (jnp.maximum(i * rb - 1, 0), 0)),
                  mod_spec, mod_spec, mod_spec, cst_spec, cst_spec,
                  pl.BlockSpec((None, None, d, tf), lambda i, j: (layer, j, 0, 0)),
                  pl.BlockSpec((None, None, d, tf), lambda i, j: (layer, nf + j, 0, 0)),
                  pl.BlockSpec((2, FFN_CONV, tf), lambda i, j: (0, 0, j)),
                  pl.BlockSpec((2, 1, tf), lambda i, j: (0, 0, j)),
                  pl.BlockSpec((None, tf, d), lambda i, j: (layer, j, 0)),
                  pl.BlockSpec((1, d), const2), pl.BlockSpec((1, d), const2)],
        out_specs=[pl.BlockSpec((tm, d), lambda i, j: (i, 0)), st_spec, st_spec],
        out_shape=[jax.ShapeDtypeStruct((m, d), F32), st_shape, st_shape],
        scratch_shapes=[pltpu.VMEM((halo + tm, d), BF16),
                        pltpu.VMEM((nseq * (_FHALO + sl), tf), F32), pltpu.VMEM((nseq * (_FHALO + sl), tf), F32),
                        pltpu.VMEM((tm, tf), act_dtype),
                        pltpu.VMEM((tm, d), F32)],
        compiler_params=_cparams(("arbitrary", "arbitrary")),
        name="ffn",
    )(x2d, x2d, sc, sh, g2, csta, cstv, w_up, w_up, cw2, cb2, w_down, ln_g, ln_b)


def _pad_state_rows(buf, rows):
    pad = [(0, 0)] * buf.ndim
    pad[-2] = (rows - buf.shape[-2], 0)
    return jnp.pad(buf, pad)


def _run_layer(x2d, mod, cfg, wts, l, states, paged):
    nb, length = cfg["nb"], cfg["len"]
    m = nb * length
    sh1, sc1, g1, sh2, sc2, g2 = mod
    gdiv = cfg["gdiv"]
    st_gla, st_s5re, st_s5im, st_lru, st_lconv, st_fconv = states

    gates, z, zlr = _inproj(x2d, sc1, sh1, gdiv(cfg["tm_in"]), wts["w_gate"], wts["w_mix"], wts["w_lr"], l,
                            cfg["tm_in"], cfg["tng"], cfg["tnm"])

    ng_seq, nseq = cfg["ng"], cfg["nseq"]
    o_gla, gla_new = _gla(z, zlr, _gla_state_pad(st_gla), wts["wa2p"], wts["gla_ba"], wts["gla_norm"],
                          ng_seq, m // ng_seq, cfg["gla_c"], cfg["gla_seq"])
    gla_new = _gla_state_unpad(gla_new)

    sl = cfg["sl"]
    o_s5, s5re_new, s5im_new = _s5(z.reshape(nb, length, Z_MIX), st_s5re.reshape(nb, _S5_CH, LANES),
                                   st_s5im.reshape(nb, _S5_CH, LANES), wts["s5"], cfg["s5_sl"])
    o_s5 = o_s5.reshape(m, BRANCH_WIDTH)
    s5re_new = s5re_new.reshape(nb, S5_GROUPS, S5_N)
    s5im_new = s5im_new.reshape(nb, S5_GROUPS, S5_N)

    o_lru, lru_new = _lru(z.reshape(nb, length, Z_MIX), _pad_state_rows(st_lconv, _HALO), st_lru, wts["lru"],
                          cfg["s5_sl"])
    o_lru = o_lru.reshape(m, BRANCH_WIDTH)
    lconv_new = z[:, COL_LX:COL_LX + BRANCH_WIDTH].reshape(nb, length, BRANCH_WIDTH)[:, length - (LRU_CONV - 1):]

    lam_init = 0.8 - 0.6 * math.exp(-0.3 * l)
    q_rot, k_rot = _rope(z, cfg["rope"], nb, length, cfg["tm_rope"])
    v_new = z[:, COL_DV:COL_DV + DA_HEADS * DA_DV]
    if paged is None:
        o_da = _attn_prompt(q_rot, k_rot, z, wts["lam_p"], wts["da_norm"], nb, length,
                            cfg["tq"], cfg["tk"], lam_init)
    else:
        page_table, cache_k, cache_v = paged
        padk = ((0, 0), (0, PAGE_SIZE - length), (0, 0))
        k_pad = jnp.pad(k_rot.reshape(nb, length, -1), padk)
        v_pad = jnp.pad(v_new.reshape(nb, length, -1), padk)
        o_da = _attn_sample(q_rot, cache_k, cache_v, l, page_table, k_pad, v_pad, wts["lam_p"],
                            wts["da_norm"], lam_init)

    x1 = _merge((o_gla, o_s5, o_lru, o_da), gates, wts["w_branch"], wts["w_out"], l, x2d, g1, gdiv(cfg["tm_mg"]),
                wts["ln1_g"], wts["ln1_b"], cfg["tm_mg"])

    fcst = _pad_state_rows(st_fconv, _HALO).reshape(ng_seq, nseq, _HALO, 2 * D_FF)
    fsl = cfg["ffn_sl"]
    x2, sta, stv = _ffn(x1, sc2, sh2, g2, gdiv(nseq * fsl), fcst, wts["w_up"], wts["ffn_conv_w"],
                        wts["ffn_conv_b"], wts["w_down"], l, wts["ln2_g"], wts["ln2_b"], nseq, fsl,
                        length // fsl if nseq == 1 else 1, _FFN_TF)
    fst = jnp.concatenate([sta, stv], axis=-1)
    if nseq == 1:
        per_seq = length // fsl
        fst = fst.reshape(nb, per_seq, _HALO, 2 * D_FF)[:, per_seq - 1]
    else:
        fst = fst.reshape(nb, _HALO, 2 * D_FF)
    fconv_new = fst[:, _HALO - (FFN_CONV - 1):]

    new = (k_rot.reshape(nb, length, DA_HEADS, 2, DA_DK), v_new.reshape(nb, length, DA_HEADS, DA_DV),
           gla_new, s5re_new, s5im_new, lru_new, lconv_new, fconv_new)
    return x2, new


def _layer_weights(l, w_in, gla_wa2, gla_ba, gla_norm, s5_raw, lru_raw, lam_raw, da_norm, w_branch, w_out,
                   ln1_g, ln1_b, ffn_w_up, ffn_conv_w, ffn_conv_b, ffn_w_down, ln2_g, ln2_b):
    n_qkv = 1024
    rest0 = n_qkv + GLA_LOWRANK
    gate0 = w_in.shape[2] - Z_GATE
    w_gate = _wprep(w_in, gate0 // _WP_TILE, Z_GATE // _WP_TILE, 0, gate0 % _WP_TILE)
    w_mix = _wprep(w_in, 0, Z_MIX // _WP_TILE, n_qkv // _WP_TILE, rest0 % _WP_TILE)
    w_lr = jnp.pad(w_in[:, :, n_qkv:rest0], ((0, 0), (0, 0), (0, LANES - GLA_LOWRANK))).astype(BF16)
    wa2p = jnp.pad(gla_wa2[l], ((0, LANES - GLA_LOWRANK), (0, 0))).astype(BF16)
    row = lambda a: a[l].reshape(1, -1)
    cw, cb, wa, ba, wx, bx, lam = [a[l] for a in lru_raw]
    return dict(
        w_gate=w_gate, w_mix=w_mix, w_lr=w_lr, wa2p=wa2p, gla_ba=row(gla_ba), gla_norm=row(gla_norm),
        s5=_s5_params(*[a[l] for a in s5_raw]),
        lru=dict(cw=cw, cb=cb.reshape(1, -1), wa=wa.astype(BF16), ba=ba.reshape(1, -1),
                 wx=wx.astype(BF16), bx=bx.reshape(1, -1), lam=lam.reshape(1, -1)),
        lam_p=tuple(row(a) for a in lam_raw), da_norm=row(da_norm),
        w_branch=w_branch.astype(BF16), w_out=w_out.astype(BF16),
        ln1_g=row(ln1_g), ln1_b=row(ln1_b),
        w_up=_tile_major(ffn_w_up.astype(BF16), _FFN_TF), ffn_conv_w=ffn_conv_w[l], ffn_conv_b=ffn_conv_b[l],
        w_down=ffn_w_down.astype(BF16), ln2_g=row(ln2_g), ln2_b=row(ln2_b))


def kernel(x_prompt, x_sample, cache_k, cache_v, state_gla, state_s5_re, state_s5_im, state_lru, state_lru_conv, state_ffn_conv, page_table, c_prompt, c_sample, w_ada, b_ada, w_in, gla_wa2, gla_ba, gla_norm, s5_a_re, s5_a_im, s5_log_dt, s5_b_re, s5_b_im, s5_c_re, s5_c_im, s5_d, s5_w_glu, s5_b_glu, lru_conv_w, lru_conv_b, lru_w_a, lru_b_a, lru_w_x, lru_b_x, lru_lambda, da_lq1, da_lk1, da_lq2, da_lk2, da_norm, w_branch, w_out, ln1_g, ln1_b, ffn_w_up, ffn_conv_w, ffn_conv_b, ffn_w_down, ln2_g, ln2_b):
    bp, lp, d = x_prompt.shape
    bs, ls, _ = x_sample.shape
    past_len = page_table.shape[1] * PAGE_SIZE

    rows = bp + bs
    rows_pad = -(-rows // SUBLANES) * SUBLANES
    c_all = jnp.pad(jnp.concatenate([c_prompt, c_sample], axis=0), ((0, rows_pad - rows), (0, 0)))
    mod_all = _ada(c_all, w_ada, b_ada)

    def mods(l, lo, n, per_row):
        parts = jnp.split(mod_all[l, lo:lo + n], 6, axis=-1)
        if per_row:
            return [jnp.repeat(p, per_row, axis=0)[None] for p in parts]
        return [p[:, None, :] for p in parts]

    cfg_p = dict(nb=bp, len=lp, ng=bp, nseq=1, sl=256, s5_sl=128, ffn_sl=512, tm_in=1024, tng=1024, tnm=768, gla_c=128,
                 gla_seq=128, tm_rope=512, tq=1024, tk=512, tm_mg=512, tf=512,
                 gdiv=lambda tm: lp // tm, rope=_rope_tables(0, lp))
    cfg_s = dict(nb=bs, len=ls, ng=1, nseq=bs, sl=ls, s5_sl=ls, ffn_sl=ls, tm_in=bs * ls, tng=1024, tnm=768, gla_c=bs * ls,
                 gla_seq=ls, tm_rope=ls, tm_mg=bs * ls, tf=512,
                 gdiv=lambda tm: 1, rope=_rope_tables(past_len, ls))

    zeros_p = (jnp.zeros((bp, GLA_HEADS, GLA_DK, GLA_DV), F32), jnp.zeros((bp, S5_GROUPS, S5_N), F32),
               jnp.zeros((bp, S5_GROUPS, S5_N), F32), jnp.zeros((bp, BRANCH_WIDTH), F32),
               jnp.zeros((bp, LRU_CONV - 1, BRANCH_WIDTH), F32), jnp.zeros((bp, FFN_CONV - 1, 2 * D_FF), F32))

    xp = x_prompt.reshape(bp * lp, d)
    xs = x_sample.reshape(bs * ls, d)
    col_p = [[] for _ in range(8)]
    col_s = [[] for _ in range(8)]
    s5_raw = (s5_a_re, s5_a_im, s5_log_dt, s5_b_re, s5_b_im, s5_c_re, s5_c_im, s5_d, s5_w_glu, s5_b_glu)
    lru_raw = (lru_conv_w, lru_conv_b, lru_w_a, lru_b_a, lru_w_x, lru_b_x, lru_lambda)
    lam_raw = (da_lq1, da_lk1, da_lq2, da_lk2)
    for l in range(DEPTH):
        wts = _layer_weights(l, w_in, gla_wa2, gla_ba, gla_norm, s5_raw, lru_raw, lam_raw, da_norm, w_branch,
                             w_out, ln1_g, ln1_b, ffn_w_up, ffn_conv_w, ffn_conv_b, ffn_w_down, ln2_g, ln2_b)
        xp, new_p = _run_layer(xp, mods(l, 0, bp, 0), cfg_p, wts, l, zeros_p, None)
        st_s = (state_gla[l], state_s5_re[l], state_s5_im[l], state_lru[l], state_lru_conv[l], state_ffn_conv[l])
        xs, new_s = _run_layer(xs, mods(l, bp, bs, ls), cfg_s, wts, l, st_s, (page_table, cache_k, cache_v))
        for lst, s in zip(col_p, new_p):
            lst.append(s)
        for lst, s in zip(col_s, new_s):
            lst.append(s)

    sp = [jnp.stack(s) for s in col_p]
    ss = [jnp.stack(s) for s in col_s]
    out = [xp.reshape(bp, lp, d), xs.reshape(bs, ls, d)]
    for a, b in zip(sp, ss):
        out += [a, b]
    return tuple(out)
```

```python
import functools
import math

import numpy as np
import jax
import jax.numpy as jnp
from jax import lax
from jax.experimental import pallas as pl
from jax.experimental.pallas import tpu as pltpu

F32 = jnp.float32
BF16 = jnp.bfloat16

D_MODEL = 2048
DEPTH = 2
PAGE_SIZE = 128
N_BRANCH = 4
BRANCH_WIDTH = 512
GLA_HEADS = 4
GLA_DK = 64
GLA_DV = 128
GLA_LOWRANK = 16
GLA_TAU = 16.0
S5_GROUP = 16
S5_GROUPS = 32
S5_N = 64
S5_STATES = S5_GROUPS * S5_N
LRU_HEADS = 4
LRU_BLOCK = 128
LRU_CONV = 4
LRU_C = 8.0
DA_HEADS = 4
DA_DK = 64
DA_DV = 128
ROPE_DIM = 16
ROPE_THETA = 500000.0
D_FF = 5632
FFN_CONV = 3
DEEPNORM_ALPHA = (2.0 * DEPTH) ** 0.25
EPS = 1e-5

LANES = 128
SUBLANES = 8
VMEM_LIMIT = 56 * 1024 * 1024

Z_GATE = N_BRANCH * D_MODEL
Z_MIX = 4608
COL_GQ, COL_GK, COL_GV, COL_OG = 0, 256, 512, 1024
COL_SU, COL_LX, COL_LG = 1536, 2048, 2560
COL_DQ, COL_DK, COL_DV = 3072, 3584, 4096

_NT = (((1,), (1,)), ((), ()))
_TN = (((0,), (0,)), ((), ()))


def _cparams(sem):
    return pltpu.CompilerParams(dimension_semantics=sem, vmem_limit_bytes=VMEM_LIMIT)


def _dot(a, b):
    return jnp.dot(a, b, preferred_element_type=F32)


def _sigmoid(x):
    return 0.5 * jnp.tanh(0.5 * x) + 0.5


def _gelu(x):
    return 0.5 * x * (1.0 + jnp.tanh(math.sqrt(2.0 / math.pi) * (x + 0.044715 * (x * x * x))))


def _layer_norm(y, g, b):
    mu = jnp.mean(y, axis=-1, keepdims=True)
    d = y - mu
    var = jnp.mean(d * d, axis=-1, keepdims=True)
    return d * lax.rsqrt(var + EPS) * g + b


def _ada_kernel(c_ref, w_ref, b_ref, o_ref):
    c = c_ref[...]
    s = c * _sigmoid(c)
    o_ref[...] = _dot(s.astype(BF16), w_ref[...].astype(BF16)) + b_ref[...]


def _ada(c_all, w_ada, b_ada):
    rows = c_all.shape[0]
    n = w_ada.shape[-1]
    tn = 1536
    return pl.pallas_call(
        _ada_kernel,
        grid=(DEPTH, n // tn),
        in_specs=[pl.BlockSpec((rows, D_MODEL), lambda l, j: (0, 0)),
                  pl.BlockSpec((None, D_MODEL, tn), lambda l, j: (l, 0, j)),
                  pl.BlockSpec((None, 1, tn), lambda l, j: (l, 0, j))],
        out_specs=pl.BlockSpec((None, rows, tn), lambda l, j: (l, 0, j)),
        out_shape=jax.ShapeDtypeStruct((DEPTH, rows, n), F32),
        compiler_params=_cparams(("arbitrary", "arbitrary")),
        name="ada",
    )(c_all, w_ada, b_ada.reshape(DEPTH, 1, n))


def _inproj_kernel(x_ref, sc_ref, sh_ref, wg_ref, wm_ref, wlr_ref, zg_ref, zm_ref, zlr_ref, u_scr, *, ngate):
    j = pl.program_id(1)

    @pl.when(j == 0)
    def _():
        u = (x_ref[...] * (1.0 + sc_ref[...]) + sh_ref[...]).astype(BF16)
        u_scr[...] = u
        zlr_ref[...] = lax.dot_general(u, wlr_ref[0], _NT, preferred_element_type=F32)

    @pl.when(j < ngate)
    def _():
        zg_ref[...] = lax.dot_general(u_scr[...], wg_ref[0], _NT, preferred_element_type=F32).astype(BF16)

    @pl.when(j >= ngate)
    def _():
        zm_ref[...] = lax.dot_general(u_scr[...], wm_ref[0], _NT, preferred_element_type=F32)


def _inproj(x2d, sc, sh, gdiv, wt, layer, tm, tng, tnm):
    m = x2d.shape[0]
    r = sc.shape[1]
    ngate, nmix = Z_GATE // tng, Z_MIX // tnm
    n_qkv = 1024
    rest0 = n_qkv + GLA_LOWRANK
    gate0 = wt.shape[1] - Z_GATE
    assert n_qkv % tnm == 0

    def mix_row(t):
        return jnp.where(t * tnm >= n_qkv, t * tnm + (rest0 - n_qkv), t * tnm)

    row_align = 2 * SUBLANES
    assert gate0 % row_align == 0 and rest0 % row_align == 0
    aligned = lambda r: pl.multiple_of(r, row_align)

    def el_spec(rows, index_map):
        return pl.BlockSpec((pl.Element(1), pl.Element(rows), pl.Element(D_MODEL)), index_map)
    mod_spec = pl.BlockSpec((None, r, D_MODEL), lambda i, j: (i // gdiv, 0, 0))
    kern = functools.partial(_inproj_kernel, ngate=ngate)
    return pl.pallas_call(
        kern,
        grid=(m // tm, ngate + nmix),
        in_specs=[pl.BlockSpec((tm, D_MODEL), lambda i, j: (i, 0)), mod_spec, mod_spec,
                  el_spec(tng, lambda i, j: (layer, aligned(gate0 + jnp.minimum(j, ngate - 1) * tng), 0)),
                  el_spec(tnm, lambda i, j: (layer, aligned(mix_row(jnp.maximum(j - ngate, 0))), 0)),
                  el_spec(LANES, lambda i, j: (layer, n_qkv, 0))],
        out_specs=[pl.BlockSpec((tm, tng), lambda i, j: (i, jnp.minimum(j, ngate - 1))),
                   pl.BlockSpec((tm, tnm), lambda i, j: (i, jnp.maximum(j - ngate, 0))),
                   pl.BlockSpec((tm, LANES), lambda i, j: (i, 0))],
        out_shape=[jax.ShapeDtypeStruct((m, Z_GATE), BF16), jax.ShapeDtypeStruct((m, Z_MIX), F32),
                   jax.ShapeDtypeStruct((m, LANES), F32)],
        scratch_shapes=[pltpu.VMEM((tm, D_MODEL), BF16)],
        compiler_params=_cparams(("arbitrary", "arbitrary")),
        name="inproj",
    )(x2d, sc, sh, wt, wt, wt)


def _rope_tables(pos0, length):
    half = ROPE_DIM // 2
    inv = ROPE_THETA ** (-jnp.arange(half, dtype=F32) * 2.0 / ROPE_DIM)
    ang = (pos0 + jnp.arange(length)).astype(F32)[:, None] * inv
    cos, sin = jnp.cos(ang), jnp.sin(ang)
    ones = jnp.ones((length, DA_DK - ROPE_DIM), F32)
    zeros = jnp.zeros((length, DA_DK - ROPE_DIM), F32)
    zh = jnp.zeros((length, half), F32)
    c = jnp.concatenate([cos, cos, ones], axis=1)
    s_next = jnp.concatenate([-sin, zh, zeros], axis=1)
    s_prev = jnp.concatenate([zh, sin, zeros], axis=1)
    rep = LANES // DA_DK
    return jnp.tile(c, (1, rep)), jnp.tile(s_next, (1, rep)), jnp.tile(s_prev, (1, rep))


def _rope_kernel(q_ref, k_ref, c_ref, sn_ref, sp_ref, qo_ref, ko_ref):
    half = ROPE_DIM // 2
    c, sn, sp = c_ref[...], sn_ref[...], sp_ref[...]
    for src, dst in ((q_ref, qo_ref), (k_ref, ko_ref)):
        for i in range(src.shape[1] // LANES):
            x = src[:, i * LANES:(i + 1) * LANES]
            nxt = pltpu.roll(x, LANES - half, axis=1)
            prv = pltpu.roll(x, half, axis=1)
            dst[:, i * LANES:(i + 1) * LANES] = x * c + nxt * sn + prv * sp


def _rope(z, tabs, nb, length, tm):
    m = z.shape[0]
    w = DA_HEADS * 2 * DA_DK
    nj = length // tm
    tab_spec = pl.BlockSpec((tm, LANES), lambda b, j: (j, 0))
    return pl.pallas_call(
        _rope_kernel,
        grid=(nb, nj),
        in_specs=[pl.BlockSpec((tm, w), lambda b, j: (b * nj + j, COL_DQ // w)),
                  pl.BlockSpec((tm, w), lambda b, j: (b * nj + j, COL_DK // w)),
                  tab_spec, tab_spec, tab_spec],
        out_specs=[pl.BlockSpec((tm, w), lambda b, j: (b * nj + j, 0))] * 2,
        out_shape=[jax.ShapeDtypeStruct((m, w), F32)] * 2,
        compiler_params=_cparams(("arbitrary", "arbitrary")),
        name="rope",
    )(z, z, *tabs)


def _gla_consts(c, seq):
    nlev = int(math.log2(seq))
    t = np.arange(c)
    tt, rr = t[:, None], t[None, :]
    same = (tt // seq) == (rr // seq)
    tril = same & (rr <= tt)
    lvl = np.full((c, c), -1, np.int32)
    lvl[t, t] = nlev
    for i in range(nlev):
        b = seq >> (i + 1)
        valid = ((tt // (2 * b)) == (rr // (2 * b))) & ((tt % (2 * b)) >= b) & ((rr % (2 * b)) < b)
        lvl[valid] = i
    return tril.astype(np.float32), lvl, nlev


def _gla_kernel(q_ref, k_ref, v_ref, og_ref, lr_ref, s0_ref, lmat_ref, lvl_ref, wa2_ref, ba_ref, gn_ref,
                o_ref, st_ref, cum_scr, *, c, seq, nlev):
    nb = c // seq
    j = pl.program_id(1)

    @pl.when(j == 0)
    def _():
        st_ref[...] = s0_ref[...]

    x = _dot(lr_ref[...].astype(BF16), wa2_ref[...]) + ba_ref[...]
    la = (jnp.minimum(x, 0.0) - jnp.log1p(jnp.exp(-jnp.abs(x)))) * (1.0 / GLA_TAU)
    cum = jnp.dot(lmat_ref[...], la, precision=lax.Precision.HIGHEST, preferred_element_type=F32)
    cum_scr[...] = cum
    wq = cum.shape[1]

    def cum_at(idx, group):
        sub = lax.broadcasted_iota(jnp.int32, (SUBLANES, wq), 0)
        bcast = lambda r: jnp.broadcast_to(cum_scr[r:r + 1, :], (SUBLANES, wq))
        tiles = []
        for t0 in range(0, c, SUBLANES):
            tile = bcast(idx(t0))
            for g in range(group, SUBLANES, group):
                tile = jnp.where(sub >= g, bcast(idx(t0 + g)), tile)
            tiles.append(tile)
        return jnp.concatenate(tiles, axis=0)

    e_end = cum_at(lambda t: (t // seq) * seq + seq - 1, seq) - cum
    e_lvl = [cum - cum_at(lambda t, b=seq >> (i + 1): (t // (2 * b)) * (2 * b) + b - 1, 2 * (seq >> (i + 1)))
             for i in range(nlev)]
    q = q_ref[...] * (GLA_DK ** -0.5)
    k = k_ref[...]
    lvl = lvl_ref[...]
    lane = lax.broadcasted_iota(jnp.int32, (c, LANES), 1)
    if nb > 1:
        rowb = lax.broadcasted_iota(jnp.int32, (c, nb * LANES), 0) // seq
        colb = lax.broadcasted_iota(jnp.int32, (c, nb * LANES), 1) // LANES
        bmask = rowb == colb

    def spread(a):
        if nb == 1:
            return a
        return jnp.where(bmask, jnp.concatenate([a] * nb, axis=1), 0.0)

    for p in range(2):
        sl = slice(p * LANES, (p + 1) * LANES)
        qp, kp, cump = q[:, sl], k[:, sl], cum[:, sl]
        qts, kts = [], []
        for i in range(nlev):
            eb = e_lvl[i][:, sl]
            qts.append(qp * jnp.exp(eb))
            kts.append((kp * jnp.exp(-eb)).astype(BF16))
        qg = qp * jnp.exp(cump)
        kh = kp * jnp.exp(e_end[:, sl])
        kpb = kp.astype(BF16)
        decays = []
        for b in range(nb):
            cl = cump[b * seq + seq - 1:b * seq + seq, :]
            decays.append(jnp.broadcast_to(jnp.exp(cl), (LANES, LANES)).T)
        for hh in range(2):
            h = 2 * p + hh
            hm = (lane // GLA_DK) == hh
            att = lax.dot_general(jnp.where(hm, qp, 0.0).astype(BF16), kpb, _NT, preferred_element_type=F32)
            att = jnp.where(lvl == nlev, att, 0.0)
            for i in range(nlev):
                a = lax.dot_general(jnp.where(hm, qts[i], 0.0).astype(BF16), kts[i], _NT,
                                    preferred_element_type=F32)
                att = jnp.where(lvl == i, a, att)
            vh = v_ref[:, h * GLA_DV:(h + 1) * GLA_DV].astype(BF16)
            o = _dot(att.astype(BF16), vh)
            s_all = st_ref[:, h].reshape(nb * LANES, GLA_DV)
            o = o + _dot(spread(jnp.where(hm, qg, 0.0)).astype(BF16), s_all.astype(BF16))
            upd = lax.dot_general(spread(jnp.where(hm, kh, 0.0)).astype(BF16), vh, _TN,
                                  preferred_element_type=F32)
            for b in range(nb):
                st_ref[b, h] = st_ref[b, h] * decays[b] + upd[b * LANES:(b + 1) * LANES]
            ms = jnp.mean(o * o, axis=-1, keepdims=True)
            og = og_ref[:, h * GLA_DV:(h + 1) * GLA_DV]
            o_ref[:, h * GLA_DV:(h + 1) * GLA_DV] = o * lax.rsqrt(ms + EPS) * gn_ref[...] * (og * _sigmoid(og))


def _gla(z, zlr, s0_pad, wa2p, ba, gn, ng, length, c, seq):
    m = z.shape[0]
    nb = c // seq
    nj = length // c
    lmat, lvl, nlev = _gla_consts(c, seq)
    w = GLA_HEADS * GLA_DK
    wv = GLA_HEADS * GLA_DV
    row = lambda g, j: g * nj + j
    const2 = lambda g, j: (0, 0)
    st_spec = pl.BlockSpec((nb, GLA_HEADS, LANES, GLA_DV), lambda g, j: (g, 0, 0, 0))
    kern = functools.partial(_gla_kernel, c=c, seq=seq, nlev=nlev)
    return pl.pallas_call(
        kern,
        grid=(ng, nj),
        in_specs=[pl.BlockSpec((c, w), lambda g, j: (row(g, j), COL_GQ // w)),
                  pl.BlockSpec((c, w), lambda g, j: (row(g, j), COL_GK // w)),
                  pl.BlockSpec((c, wv), lambda g, j: (row(g, j), COL_GV // wv)),
                  pl.BlockSpec((c, wv), lambda g, j: (row(g, j), COL_OG // wv)),
                  pl.BlockSpec((c, LANES), lambda g, j: (row(g, j), 0)),
                  st_spec,
                  pl.BlockSpec(lmat.shape, const2),
                  pl.BlockSpec(lvl.shape, const2),
                  pl.BlockSpec((LANES, w), const2),
                  pl.BlockSpec((1, w), const2),
                  pl.BlockSpec((1, GLA_DV), const2)],
        out_specs=[pl.BlockSpec((c, wv), lambda g, j: (row(g, j), 0)), st_spec],
        out_shape=[jax.ShapeDtypeStruct((m, wv), F32),
                   jax.ShapeDtypeStruct(s0_pad.shape, F32)],
        scratch_shapes=[pltpu.VMEM((c, w), F32)],
        compiler_params=_cparams(("arbitrary", "arbitrary")),
        name="gla",
    )(z, z, z, z, zlr, s0_pad, jnp.asarray(lmat), jnp.asarray(lvl), wa2p, ba, gn)


def _gla_state_pad(s):
    b = s.shape[0]
    s6 = s.reshape(b, 2, 2, 1, GLA_DK, GLA_DV) * jnp.eye(2, dtype=F32)[None, None, :, :, None, None]
    return s6.reshape(b, GLA_HEADS, LANES, GLA_DV)


def _gla_state_unpad(sp):
    b = sp.shape[0]
    s6 = sp.reshape(b, 2, 2, 2, GLA_DK, GLA_DV)
    return jnp.stack([s6[:, :, 0, 0], s6[:, :, 1, 1]], axis=2).reshape(b, GLA_HEADS, GLA_DK, GLA_DV)


_S5_CH = S5_STATES // LANES


def _s5_kernel(u_ref, h0r_ref, h0i_ref, are_ref, aim_ref, ldt_ref, bdr_ref, bdi_ref, cdr_ref, cdi_ref,
               d_ref, wg_ref, bg_ref, y_ref, hfr_ref, hfi_ref, sre, sim, hsc, hcr, hci, *, nseq, sl):
    tm = nseq * sl
    w = BRANCH_WIDTH
    j = pl.program_id(0)

    @pl.when(j == 0)
    def _():
        hcr[...] = h0r_ref[...]
        hci[...] = h0i_ref[...]

    lr, li = are_ref[...], aim_ref[...]
    dt = jnp.exp(ldt_ref[...])
    mag = jnp.exp(lr * dt)
    abr, abi = mag * jnp.cos(li * dt), mag * jnp.sin(li * dt)
    den = lr * lr + li * li
    fr = ((abr - 1.0) * lr + abi * li) / den
    fi = (abi * lr - (abr - 1.0) * li) / den

    u = u_ref[...].reshape(tm, w)
    ub = u.astype(BF16)
    for c2 in range(_S5_CH // 2):
        ws = slice((c2 // 2) * LANES, (c2 // 2 + 1) * LANES)
        tr = _dot(ub[:, ws], bdr_ref[ws, c2 * 256:(c2 + 1) * 256])
        ti = _dot(ub[:, ws], bdi_ref[ws, c2 * 256:(c2 + 1) * 256])
        for e in range(2):
            ch = 2 * c2 + e
            trc, tic = tr[:, e * LANES:(e + 1) * LANES], ti[:, e * LANES:(e + 1) * LANES]
            frc, fic = fr[ch:ch + 1, :], fi[ch:ch + 1, :]
            sre[ch * tm:(ch + 1) * tm, :] = frc * trc - fic * tic
            sim[ch * tm:(ch + 1) * tm, :] = frc * tic + fic * trc

    ar0, ai0, ar1, ai1 = abr[0:8], abi[0:8], abr[8:16], abi[8:16]
    grp = min(nseq, 4)
    for g0 in range(0, nseq, grp):
        def body(t, carry, g0=g0):
            out = []
            for q in range(grp):
                r0, i0, r1, i1 = carry[4 * q:4 * q + 4]
                row = (g0 + q) * sl + t
                lo = pl.ds(row, SUBLANES, stride=tm)
                hi = pl.ds(SUBLANES * tm + row, SUBLANES, stride=tm)
                n_r0 = ar0 * r0 - ai0 * i0 + sre[lo, :]
                n_i0 = ar0 * i0 + ai0 * r0 + sim[lo, :]
                n_r1 = ar1 * r1 - ai1 * i1 + sre[hi, :]
                n_i1 = ar1 * i1 + ai1 * r1 + sim[hi, :]
                dst = pl.ds(pl.multiple_of(row * SUBLANES, SUBLANES), SUBLANES)
                hsc[0, dst, :] = n_r0
                hsc[1, dst, :] = n_i0
                hsc[2, dst, :] = n_r1
                hsc[3, dst, :] = n_i1
                out += [n_r0, n_i0, n_r1, n_i1]
            return tuple(out)

        init = tuple(a for q in range(grp) for a in (hcr[g0 + q, 0:8], hci[g0 + q, 0:8],
                                                     hcr[g0 + q, 8:16], hci[g0 + q, 8:16]))
        fin = lax.fori_loop(0, sl, body, init, unroll=8)
        for q in range(grp):
            hcr[g0 + q, 0:8] = fin[4 * q]
            hci[g0 + q, 0:8] = fin[4 * q + 1]
            hcr[g0 + q, 8:16] = fin[4 * q + 2]
            hci[g0 + q, 8:16] = fin[4 * q + 3]

    ywin = []
    for wi in range(w // LANES):
        ws = slice(wi * LANES, (wi + 1) * LANES)
        yw = d_ref[:, ws] * u[:, ws]
        for c2 in (2 * wi, 2 * wi + 1):
            part = 2 * ((2 * c2) // SUBLANES)
            ca, cb = (2 * c2) % SUBLANES, (2 * c2 + 1) % SUBLANES
            rows = lambda cc: pl.ds(cc, tm, stride=SUBLANES)
            hr, hi = hsc.at[part], hsc.at[part + 1]
            hre = jnp.concatenate([hr[rows(ca), :], hr[rows(cb), :]], axis=1).astype(BF16)
            him = jnp.concatenate([hi[rows(ca), :], hi[rows(cb), :]], axis=1).astype(BF16)
            yw = yw + _dot(hre, cdr_ref[c2 * 256:(c2 + 1) * 256, ws]) - _dot(him, cdi_ref[c2 * 256:(c2 + 1) * 256, ws])
        ywin.append(yw)
    y = _gelu(jnp.concatenate(ywin, axis=1))
    y = y * _sigmoid(_dot(y.astype(BF16), wg_ref[...]) + bg_ref[...])
    y_ref[...] = y.reshape(nseq, sl, w)

    @pl.when(j == pl.num_programs(0) - 1)
    def _():
        hfr_ref[...] = hcr[...]
        hfi_ref[...] = hci[...]


def _s5(z3, h0r, h0i, p, sl):
    nseq, length, _ = z3.shape
    tm = nseq * sl
    w = BRANCH_WIDTH
    const2 = lambda j: (0, 0)
    st_spec = pl.BlockSpec((nseq, _S5_CH, LANES), lambda j: (0, 0, 0))
    vec = pl.BlockSpec((_S5_CH, LANES), const2)
    kern = functools.partial(_s5_kernel, nseq=nseq, sl=sl)
    return pl.pallas_call(
        kern,
        grid=(length // sl,),
        in_specs=[pl.BlockSpec((nseq, sl, w), lambda j: (0, j, COL_SU // w)),
                  st_spec, st_spec, vec, vec, vec,
                  pl.BlockSpec((w, S5_STATES), const2), pl.BlockSpec((w, S5_STATES), const2),
                  pl.BlockSpec((S5_STATES, w), const2), pl.BlockSpec((S5_STATES, w), const2),
                  pl.BlockSpec((1, w), const2), pl.BlockSpec((w, w), const2), pl.BlockSpec((1, w), const2)],
        out_specs=[pl.BlockSpec((nseq, sl, w), lambda j: (0, j, 0)), st_spec, st_spec],
        out_shape=[jax.ShapeDtypeStruct((nseq, length, w), F32),
                   jax.ShapeDtypeStruct(h0r.shape, F32), jax.ShapeDtypeStruct(h0i.shape, F32)],
        scratch_shapes=[pltpu.VMEM((_S5_CH * tm, LANES), F32), pltpu.VMEM((_S5_CH * tm, LANES), F32),
                        pltpu.VMEM((4, SUBLANES * tm, LANES), F32),
                        pltpu.VMEM((nseq, _S5_CH, LANES), F32), pltpu.VMEM((nseq, _S5_CH, LANES), F32)],
        compiler_params=_cparams(("arbitrary",)),
        name="s5",
    )(z3, h0r, h0i, p["are"], p["aim"], p["ldt"], p["bdr"], p["bdi"], p["cdr"], p["cdi"],
      p["d"], p["wg"], p["bg"])


def _s5_params(a_re, a_im, log_dt, b_re, b_im, c_re, c_im, d, w_glu, b_glu):
    eye = jnp.eye(S5_GROUPS, dtype=F32)

    def bd_in(b):
        return (eye[:, None, :, None] * jnp.swapaxes(b, 1, 2)[:, :, None, :]).reshape(
            S5_GROUPS * S5_GROUP, S5_STATES).astype(BF16)

    def bd_out(c):
        return (eye[:, None, :, None] * jnp.swapaxes(c, 1, 2)[:, :, None, :]).reshape(
            S5_STATES, S5_GROUPS * S5_GROUP).astype(BF16)

    return dict(are=a_re.reshape(_S5_CH, LANES), aim=a_im.reshape(_S5_CH, LANES),
                ldt=jnp.broadcast_to(log_dt[:, None], (S5_GROUPS, S5_N)).reshape(_S5_CH, LANES),
                bdr=bd_in(b_re), bdi=bd_in(b_im), cdr=bd_out(c_re), cdi=bd_out(c_im),
                d=d.reshape(1, -1), wg=w_glu.astype(BF16), bg=b_glu.reshape(1, -1))


_HALO = SUBLANES


def _lru_kernel(lx_ref, lg_ref, zprev_ref, cst_ref, h0_ref, cw_ref, cb_ref, wa_ref, ba_ref, wx_ref, bx_ref,
                lam_ref, y_ref, hfin_ref, cs_scr, xc_scr, a_scr, b_scr, h_c, *, nseq, sl):
    tm = nseq * sl
    j = pl.program_id(0)

    @pl.when(j == 0)
    def _():
        h_c[...] = h0_ref[...]

    k = LRU_CONV
    for s in range(nseq):
        cs_scr[0:_HALO, :] = jnp.where(j == 0, cst_ref[s], zprev_ref[s])
        cs_scr[_HALO:_HALO + sl, :] = lx_ref[s]
        xc = cb_ref[...]
        for t in range(k):
            off = _HALO - (k - 1) + t
            xc = xc + cw_ref[t:t + 1, :] * cs_scr[off:off + sl, :]
        xc_scr[s * sl:(s + 1) * sl, :] = xc

    lam = lam_ref[...]
    sp = jnp.maximum(-lam, 0.0) + jnp.log1p(jnp.exp(-jnp.abs(lam)))
    for h in range(LRU_HEADS):
        hs = slice(h * LRU_BLOCK, (h + 1) * LRU_BLOCK)
        xc = xc_scr[:, hs]
        xb = xc.astype(BF16)
        r = _sigmoid(_dot(xb, wa_ref[h]) + ba_ref[:, hs])
        i = _sigmoid(_dot(xb, wx_ref[h]) + bx_ref[:, hs])
        la = -LRU_C * r * sp[:, hs]
        a_scr[:, hs] = jnp.exp(la)
        b_scr[:, hs] = jnp.sqrt(-jnp.tanh(la) * (jnp.exp(2.0 * la) + 1.0)) * (i * xc)

    grp = min(nseq, 4)
    for g0 in range(0, nseq, grp):
        def body(t, hs, g0=g0):
            out = []
            for q in range(grp):
                idx = pl.ds((g0 + q) * sl + t, 1)
                h = a_scr[idx, :] * hs[q] + b_scr[idx, :]
                b_scr[idx, :] = h
                out.append(h)
            return tuple(out)

        fin = lax.fori_loop(0, sl, body, tuple(h_c[g0 + q:g0 + q + 1, :] for q in range(grp)), unroll=8)
        for q in range(grp):
            h_c[g0 + q:g0 + q + 1, :] = fin[q]

    y = b_scr[...] * _gelu(lg_ref[...].reshape(tm, BRANCH_WIDTH))
    y_ref[...] = y.reshape(nseq, sl, BRANCH_WIDTH)

    @pl.when(j == pl.num_programs(0) - 1)
    def _():
        hfin_ref[...] = h_c[...]


def _lru(z3, cst, h0, p, sl):
    nseq, length, _ = z3.shape
    tm = nseq * sl
    w = BRANCH_WIDTH
    const2 = lambda j: (0, 0)
    const3 = lambda j: (0, 0, 0)
    st_spec = pl.BlockSpec((nseq, w), const2)
    vec = pl.BlockSpec((1, w), const2)
    wsp = pl.BlockSpec((LRU_HEADS, LRU_BLOCK, LRU_BLOCK), const3)
    rb = sl // _HALO
    kern = functools.partial(_lru_kernel, nseq=nseq, sl=sl)
    return pl.pallas_call(
        kern,
        grid=(length // sl,),
        in_specs=[pl.BlockSpec((nseq, sl, w), lambda j: (0, j, COL_LX // w)),
                  pl.BlockSpec((nseq, sl, w), lambda j: (0, j, COL_LG // w)),
                  pl.BlockSpec((nseq, _HALO, w), lambda j: (0, jnp.maximum(j * rb - 1, 0), COL_LX // w)),
                  pl.BlockSpec((nseq, _HALO, w), const3),
                  st_spec,
                  pl.BlockSpec((LRU_CONV, w), const2), vec, wsp, vec, wsp, vec, vec],
        out_specs=[pl.BlockSpec((nseq, sl, w), lambda j: (0, j, 0)), st_spec],
        out_shape=[jax.ShapeDtypeStruct((nseq, length, w), F32), jax.ShapeDtypeStruct(h0.shape, F32)],
        scratch_shapes=[pltpu.VMEM((_HALO + sl, w), F32), pltpu.VMEM((tm, w), F32),
                        pltpu.VMEM((tm, w), F32), pltpu.VMEM((tm, w), F32),
                        pltpu.VMEM((nseq, w), F32)],
        compiler_params=_cparams(("arbitrary",)),
        name="lru",
    )(z3, z3, z3, cst, h0, p["cw"], p["cb"], p["wa"], p["ba"], p["wx"], p["bx"], p["lam"])


def _da_lambda(lq1_ref, lk1_ref, lq2_ref, lk2_ref, lam_init):
    return (jnp.exp(jnp.sum(lq1_ref[...] * lk1_ref[...])) - jnp.exp(jnp.sum(lq2_ref[...] * lk2_ref[...]))
            + lam_init)


def _attn_kernel(q_ref, k_ref, v_ref, lq1_ref, lk1_ref, lq2_ref, lk2_ref, dn_ref, o_ref,
                 m_scr, l_scr, acc_scr, *, tq, tk, lam_init):
    qi, ki = pl.program_id(2), pl.program_id(3)

    @pl.when(ki == 0)
    def _():
        m_scr[...] = jnp.full(m_scr.shape, -jnp.inf, F32)
        l_scr[...] = jnp.zeros(l_scr.shape, F32)
        acc_scr[...] = jnp.zeros(acc_scr.shape, F32)

    def step(masked):
        q = q_ref[...] * (DA_DK ** -0.5)
        kb = k_ref[...].astype(BF16)
        vb = v_ref[...].astype(BF16)
        lane = lax.broadcasted_iota(jnp.int32, (tq, LANES), 1)
        if masked:
            row = qi * tq + lax.broadcasted_iota(jnp.int32, (tq, tk), 0)
            col = ki * tk + lax.broadcasted_iota(jnp.int32, (tq, tk), 1)
            mask = col <= row
        comps = range(2)
        ss = [lax.dot_general(jnp.where((lane // DA_DK) == c, q, 0.0).astype(BF16), kb, _NT,
                              preferred_element_type=F32) for c in comps]
        if masked:
            ss = [jnp.where(mask, s, -jnp.inf) for s in ss]
        m_prev = [m_scr[c] for c in comps]
        l_prev = [l_scr[c] for c in comps]
        a_prev = [acc_scr[c] for c in comps]
        m_new = [jnp.maximum(m_prev[c], jnp.max(ss[c], axis=-1, keepdims=True)) for c in comps]
        alpha = [jnp.exp(m_prev[c] - m_new[c]) for c in comps]
        pexp = [jnp.exp(ss[c] - m_new[c]) for c in comps]
        pv = [_dot(pexp[c].astype(BF16), vb) for c in comps]
        for c in comps:
            l_scr[c] = alpha[c] * l_prev[c] + jnp.sum(pexp[c], axis=-1, keepdims=True)
            acc_scr[c] = alpha[c] * a_prev[c] + pv[c]
            m_scr[c] = m_new[c]

    last_col, first_col = ki * tk + tk - 1, ki * tk
    first_row, last_row = qi * tq, qi * tq + tq - 1

    @pl.when(last_col <= first_row)
    def _():
        step(False)

    @pl.when((last_col > first_row) & (first_col <= last_row))
    def _():
        step(True)

    @pl.when(ki == pl.num_programs(3) - 1)
    def _():
        lam = _da_lambda(lq1_ref, lk1_ref, lq2_ref, lk2_ref, lam_init)
        o = acc_scr[0] / l_scr[0] - lam * (acc_scr[1] / l_scr[1])
        ms = jnp.mean(o * o, axis=-1, keepdims=True)
        o_ref[...] = o * lax.rsqrt(ms + EPS) * dn_ref[...] * (1.0 - lam_init)


def _attn_prompt(q_rot, k_rot, z, lam_p, dn, nb, length, tq, tk, lam_init):
    m = q_rot.shape[0]
    nq, nk = length // tq, length // tk
    const2 = lambda b, h, qi, ki: (0, 0)
    kv_row = lambda b, qi, ki: b * nk + jnp.minimum(ki, (qi * tq + tq - 1) // tk)
    vec = pl.BlockSpec((1, DA_DK), const2)
    kern = functools.partial(_attn_kernel, tq=tq, tk=tk, lam_init=lam_init)
    return pl.pallas_call(
        kern,
        grid=(nb, DA_HEADS, nq, nk),
        in_specs=[pl.BlockSpec((tq, LANES), lambda b, h, qi, ki: (b * nq + qi, h)),
                  pl.BlockSpec((tk, LANES), lambda b, h, qi, ki: (kv_row(b, qi, ki), h)),
                  pl.BlockSpec((tk, DA_DV), lambda b, h, qi, ki: (kv_row(b, qi, ki), COL_DV // DA_DV + h)),
                  vec, vec, vec, vec, pl.BlockSpec((1, DA_DV), const2)],
        out_specs=pl.BlockSpec((tq, DA_DV), lambda b, h, qi, ki: (b * nq + qi, h)),
        out_shape=jax.ShapeDtypeStruct((m, DA_HEADS * DA_DV), F32),
        scratch_shapes=[pltpu.VMEM((2, tq, 1), F32), pltpu.VMEM((2, tq, 1), F32),
                        pltpu.VMEM((2, tq, DA_DV), F32)],
        compiler_params=_cparams(("arbitrary", "arbitrary", "arbitrary", "arbitrary")),
        name="attn_prompt",
    )(q_rot, k_rot, z, *lam_p, dn)


_PAGES_PER_STEP = 16
_QROWS = DA_HEADS * 2 * SUBLANES


def _attn_sample_kernel(pt_ref, q_ref, *refs, lq, lam_init):
    npg = _PAGES_PER_STEP
    kt_refs, v_refs = refs[0:npg], refs[npg:2 * npg]
    kn_ref, vn_ref, lq1_ref, lk1_ref, lq2_ref, lk2_ref, dn_ref = refs[2 * npg:2 * npg + 7]
    o_ref = refs[2 * npg + 7]
    qbd_scr, m_scr, l_scr, acc_scr = refs[2 * npg + 8:]
    w = DA_HEADS * 2 * DA_DK
    hrows = 2 * lq
    j = pl.program_id(1)

    @pl.when(j == 0)
    def _():
        q = q_ref[...] * (DA_DK ** -0.5)
        lane = lax.broadcasted_iota(jnp.int32, (lq, w), 1)
        tiles = [jnp.where((lane // DA_DK) == hc, q, 0.0) for hc in range(DA_HEADS * 2)]
        qbd_scr[...] = jnp.concatenate(tiles, axis=0)
        m_scr[...] = jnp.full(m_scr.shape, -jnp.inf, F32)
        l_scr[...] = jnp.zeros(l_scr.shape, F32)
        acc_scr[...] = jnp.zeros(acc_scr.shape, F32)

    qbd = qbd_scr[...].astype(BF16)

    def update(s, nchunk, vget):
        m_prev = m_scr[...]
        m_new = jnp.maximum(m_prev, jnp.max(s, axis=-1, keepdims=True))
        alpha = jnp.exp(m_prev - m_new)
        pexp = jnp.exp(s - m_new)
        l_scr[...] = alpha * l_scr[...] + jnp.sum(pexp, axis=-1, keepdims=True)
        pb = pexp.astype(BF16)
        for h in range(DA_HEADS):
            rs = slice(h * hrows, (h + 1) * hrows)
            pv = _dot(pb[rs, 0:PAGE_SIZE], vget(0, h))
            for i in range(1, nchunk):
                pv = pv + _dot(pb[rs, i * PAGE_SIZE:(i + 1) * PAGE_SIZE], vget(i, h))
            acc_scr[rs, :] = alpha[rs] * acc_scr[rs, :] + pv
        m_scr[...] = m_new

    s_pages = jnp.concatenate([_dot(qbd, kt_refs[i][...].astype(BF16)) for i in range(npg)], axis=1)
    update(s_pages, npg, lambda i, h: v_refs[i][pl.ds(h, PAGE_SIZE, stride=DA_HEADS), :].astype(BF16))

    @pl.when(j == pl.num_programs(1) - 1)
    def _():
        nk = kn_ref.shape[0]
        t = lax.broadcasted_iota(jnp.int32, (_QROWS, nk), 0) % lq
        col = lax.broadcasted_iota(jnp.int32, (_QROWS, nk), 1)
        s_new = lax.dot_general(qbd, kn_ref[...].astype(BF16), _NT, preferred_element_type=F32)
        update(jnp.where(col <= t, s_new, -jnp.inf), 1,
               lambda i, h: vn_ref[:, h * DA_DV:(h + 1) * DA_DV].astype(BF16))
        lam = _da_lambda(lq1_ref, lk1_ref, lq2_ref, lk2_ref, lam_init)
        for h in range(DA_HEADS):
            r0 = h * hrows
            o0 = acc_scr[r0:r0 + lq, :] / l_scr[r0:r0 + lq]
            o1 = acc_scr[r0 + lq:r0 + 2 * lq, :] / l_scr[r0 + lq:r0 + 2 * lq]
            o = o0 - lam * o1
            ms = jnp.mean(o * o, axis=-1, keepdims=True)
            o_ref[:, h * DA_DV:(h + 1) * DA_DV] = o * lax.rsqrt(ms + EPS) * dn_ref[...] * (1.0 - lam_init)


def _attn_sample(q_rot, cache_k, cache_v, layer, page_table, k_new, v_new, lam_p, dn, lam_init):
    nb, n_pages = page_table.shape
    lq = q_rot.shape[0] // nb
    w = DA_HEADS * 2 * DA_DK
    wv = DA_HEADS * DA_DV
    npg = _PAGES_PER_STEP
    nsteps = n_pages // npg
    ckt = jnp.transpose(cache_k, (0, 1, 3, 4, 5, 2)).reshape(DEPTH, -1, w, PAGE_SIZE)
    cv2 = cache_v.reshape(DEPTH, -1, PAGE_SIZE * DA_HEADS, DA_DV)
    const2 = lambda b, j, pt: (0, 0)
    page = lambda b, j, pt, i: pt[b * n_pages + j * npg + i]

    def k_spec(i):
        return pl.BlockSpec((None, None, w, PAGE_SIZE), lambda b, j, pt: (layer, page(b, j, pt, i), 0, 0))

    def v_spec(i):
        return pl.BlockSpec((None, None, PAGE_SIZE * DA_HEADS, DA_DV),
                            lambda b, j, pt: (layer, page(b, j, pt, i), 0, 0))

    vec = pl.BlockSpec((1, DA_DK), const2)
    nk = k_new.shape[1]
    grid_spec = pltpu.PrefetchScalarGridSpec(
        num_scalar_prefetch=1,
        grid=(nb, nsteps),
        in_specs=([pl.BlockSpec((lq, w), lambda b, j, pt: (b, 0))]
                  + [k_spec(i) for i in range(npg)] + [v_spec(i) for i in range(npg)]
                  + [pl.BlockSpec((None, nk, w), lambda b, j, pt: (b, 0, 0)),
                     pl.BlockSpec((None, nk, wv), lambda b, j, pt: (b, 0, 0)),
                     vec, vec, vec, vec, pl.BlockSpec((1, DA_DV), const2)]),
        out_specs=pl.BlockSpec((lq, wv), lambda b, j, pt: (b, 0)),
        scratch_shapes=[pltpu.VMEM((_QROWS, w), F32), pltpu.VMEM((_QROWS, 1), F32),
                        pltpu.VMEM((_QROWS, 1), F32), pltpu.VMEM((_QROWS, DA_DV), F32)],
    )
    kern = functools.partial(_attn_sample_kernel, lq=lq, lam_init=lam_init)
    return pl.pallas_call(
        kern,
        grid_spec=grid_spec,
        out_shape=jax.ShapeDtypeStruct((nb * lq, wv), F32),
        compiler_params=_cparams(("arbitrary", "arbitrary")),
        name="attn_sample",
    )(page_table.reshape(-1), q_rot, *([ckt] * npg), *([cv2] * npg), k_new, v_new, *lam_p, dn)


def _merge_kernel(o0_ref, o1_ref, o2_ref, o3_ref, g_ref, wb_ref, wo_ref, x_ref, g1_ref, lng_ref, lnb_ref,
                  out_ref, acc_scr):
    b = pl.program_id(1)

    @pl.when(b == 0)
    def _():
        acc_scr[...] = jnp.zeros(acc_scr.shape, F32)

    o = jnp.where(b == 0, o0_ref[...], jnp.where(b == 1, o1_ref[...], jnp.where(b == 2, o2_ref[...], o3_ref[...])))
    acc_scr[...] += _sigmoid(g_ref[...].astype(F32)) * _dot(o.astype(BF16), wb_ref[...])

    @pl.when(b == N_BRANCH - 1)
    def _():
        mix = _dot(acc_scr[...].astype(BF16), wo_ref[...])
        y = DEEPNORM_ALPHA * x_ref[...] + g1_ref[...] * mix
        out_ref[...] = _layer_norm(y, lng_ref[...], lnb_ref[...])


def _merge(branches, gates, w_branch, w_out, layer, x2d, g1, gdiv, ln_g, ln_b, tm):
    m = x2d.shape[0]
    d = D_MODEL
    r = g1.shape[1]
    const2 = lambda i, b: (0, 0)
    osp = pl.BlockSpec((tm, BRANCH_WIDTH), lambda i, b: (i, 0))
    return pl.pallas_call(
        _merge_kernel,
        grid=(m // tm, N_BRANCH),
        in_specs=[osp, osp, osp, osp,
                  pl.BlockSpec((tm, d), lambda i, b: (i, b)),
                  pl.BlockSpec((None, None, BRANCH_WIDTH, d), lambda i, b: (layer, b, 0, 0)),
                  pl.BlockSpec((None, d, d), lambda i, b: (layer, 0, 0), pipeline_mode=pl.Buffered(1)),
                  pl.BlockSpec((tm, d), lambda i, b: (i, 0)),
                  pl.BlockSpec((None, r, d), lambda i, b: (i // gdiv, 0, 0)),
                  pl.BlockSpec((1, d), const2), pl.BlockSpec((1, d), const2)],
        out_specs=pl.BlockSpec((tm, d), lambda i, b: (i, 0)),
        out_shape=jax.ShapeDtypeStruct((m, d), F32),
        scratch_shapes=[pltpu.VMEM((tm, d), F32)],
        compiler_params=_cparams(("arbitrary", "arbitrary")),
        name="merge",
    )(*branches, gates, w_branch, w_out, x2d, g1, ln_g, ln_b)


_FHALO = 2 * SUBLANES


def _ffn_kernel(x_ref, xprev_ref, sc_ref, sh_ref, g2_ref, csta_ref, cstv_ref, wa_ref, wv_ref, cw_ref, cb_ref,
                wd_ref, lng_ref, lnb_ref, out_ref, sta_ref, stv_ref, u_scr, hsa, hsv, act_scr, acc_scr,
                *, nseq, sl, first_div, rc):
    tm = nseq * sl
    halo = _FHALO if nseq == 1 else 0
    seg = _FHALO + sl
    i, j = pl.program_id(0), pl.program_id(1)
    nf = pl.num_programs(1)

    @pl.when(j == 0)
    def _():
        sc, sh = 1.0 + sc_ref[...], sh_ref[...]
        if halo:
            u_scr[0:halo, :] = (xprev_ref[...] * sc + sh).astype(BF16)
        u_scr[halo:halo + tm, :] = (x_ref[...] * sc + sh).astype(BF16)
        acc_scr[...] = jnp.zeros(acc_scr.shape, F32)

    parts = ((wa_ref, csta_ref, hsa, sta_ref), (wv_ref, cstv_ref, hsv, stv_ref))
    if nseq == 1:
        half = tm // 2
        spans = ((0, half), (half, tm))
        first = i % first_div == 0
        for lo, hi in ((0, _FHALO + half), (_FHALO + half, _FHALO + tm)):
            for w_ref, cst_ref, hs, _ in parts:
                hs[lo:hi, :] = _dot(u_scr[lo:hi, :], w_ref[...])
                if lo == 0:
                    hs[_FHALO - _HALO:_FHALO, :] = jnp.where(first, cst_ref[0], hs[_FHALO - _HALO:_FHALO, :])
    else:
        spans = ((0, tm),)
        u = u_scr[...]
        for w_ref, cst_ref, hs, _ in parts:
            hup = _dot(u, w_ref[...])
            for s in range(nseq):
                hs[s * seg + _FHALO - _HALO:s * seg + _FHALO, :] = cst_ref[s]
                hs[s * seg + _FHALO:(s + 1) * seg, :] = hup[s * sl:(s + 1) * sl]
    for _, _, hs, st_ref in parts:
        for s in range(nseq):
            st_ref[s] = hs[(s + 1) * seg - _HALO:(s + 1) * seg, :]

    k = FFN_CONV
    base = _FHALO - (k - 1)
    for lo, hi in spans:
        for r0 in range(lo, hi, rc):
            s, t0 = divmod(r0, sl)
            o = s * seg + base + t0
            ya, yv = cb_ref[0], cb_ref[1]
            for t in range(k):
                ya = ya + cw_ref[0, t:t + 1, :] * hsa[o + t:o + t + rc, :]
                yv = yv + cw_ref[1, t:t + 1, :] * hsv[o + t:o + t + rc, :]
            act_scr[r0:r0 + rc, :] = (_gelu(ya) * yv).astype(act_scr.dtype)
        acc_scr[lo:hi, :] += _dot(act_scr[lo:hi, :].astype(BF16), wd_ref[...])

    @pl.when(j == nf - 1)
    def _():
        y = DEEPNORM_ALPHA * x_ref[...] + g2_ref[...] * acc_scr[...]
        out_ref[...] = _layer_norm(y, lng_ref[...], lnb_ref[...])


_FFN_TF = 512


def _ffn(x2d, sc, sh, g2, gdiv, cst, w_up, conv_w, conv_b, w_down, layer, ln_g, ln_b, nseq, sl, first_div, tf):
    m, d = x2d.shape
    tm = nseq * sl
    nblk = m // tm
    nf = D_FF // tf
    r = sc.shape[1]
    halo = _FHALO if nseq == 1 else 0
    const2 = lambda i, j: (0, 0)
    mod_spec = pl.BlockSpec((None, r, d), lambda i, j: (i // gdiv, 0, 0))
    rb = tm // _FHALO
    cw2 = conv_w.reshape(FFN_CONV, 2, D_FF).transpose(1, 0, 2)
    cb2 = conv_b.reshape(2, 1, D_FF)
    csta, cstv = cst[..., :D_FF], cst[..., D_FF:]
    cst_spec = pl.BlockSpec((None, nseq, _HALO, tf), lambda i, j: (i // first_div, 0, 0, j))
    st_spec = pl.BlockSpec((None, nseq, _HALO, tf), lambda i, j: (i, 0, 0, j))
    st_shape = jax.ShapeDtypeStruct((nblk, nseq, _HALO, D_FF), F32)
    rc = 32 if sl % 32 == 0 else sl
    act_dtype = BF16 if rc % (2 * SUBLANES) == 0 else F32
    kern = functools.partial(_ffn_kernel, nseq=nseq, sl=sl, first_div=first_div, rc=rc)
    return pl.pallas_call(
        kern,
        grid=(nblk, nf),
        in_specs=[pl.BlockSpec((tm, d), lambda i, j: (i, 0)),
                  pl.BlockSpec((_FHALO, d), lambda i, j: (jnp.maximum(i * rb - 1, 0), 0)),
                  mod_spec, mod_spec, mod_spec, cst_spec, cst_spec,
                  pl.BlockSpec((None, d, tf), lambda i, j: (layer, 0, j)),
                  pl.BlockSpec((None, d, tf), lambda i, j: (layer, 0, nf + j)),
                  pl.BlockSpec((2, FFN_CONV, tf), lambda i, j: (0, 0, j)),
                  pl.BlockSpec((2, 1, tf), lambda i, j: (0, 0, j)),
                  pl.BlockSpec((None, tf, d), lambda i, j: (layer, j, 0)),
                  pl.BlockSpec((1, d), const2), pl.BlockSpec((1, d), const2)],
        out_specs=[pl.BlockSpec((tm, d), lambda i, j: (i, 0)), st_spec, st_spec],
        out_shape=[jax.ShapeDtypeStruct((m, d), F32), st_shape, st_shape],
        scratch_shapes=[pltpu.VMEM((halo + tm, d), BF16),
                        pltpu.VMEM((nseq * (_FHALO + sl), tf), F32), pltpu.VMEM((nseq * (_FHALO + sl), tf), F32),
                        pltpu.VMEM((tm, tf), act_dtype),
                        pltpu.VMEM((tm, d), F32)],
        compiler_params=_cparams(("arbitrary", "arbitrary")),
        name="ffn",
    )(x2d, x2d, sc, sh, g2, csta, cstv, w_up, w_up, cw2, cb2, w_down, ln_g, ln_b)


def _pad_state_rows(buf, rows):
    pad = [(0, 0)] * buf.ndim
    pad[-2] = (rows - buf.shape[-2], 0)
    return jnp.pad(buf, pad)


def _run_layer(x2d, mod, cfg, wts, l, states, paged):
    nb, length = cfg["nb"], cfg["len"]
    m = nb * length
    sh1, sc1, g1, sh2, sc2, g2 = mod
    gdiv = cfg["gdiv"]
    st_gla, st_s5re, st_s5im, st_lru, st_lconv, st_fconv = states

    gates, z, zlr = _inproj(x2d, sc1, sh1, gdiv(cfg["tm_in"]), wts["w_in_t"], l,
                            cfg["tm_in"], cfg["tng"], cfg["tnm"])

    ng_seq, nseq = cfg["ng"], cfg["nseq"]
    o_gla, gla_new = _gla(z, zlr, _gla_state_pad(st_gla), wts["wa2p"], wts["gla_ba"], wts["gla_norm"],
                          ng_seq, m // ng_seq, cfg["gla_c"], cfg["gla_seq"])
    gla_new = _gla_state_unpad(gla_new)

    sl = cfg["sl"]
    o_s5, s5re_new, s5im_new = _s5(z.reshape(nb, length, Z_MIX), st_s5re.reshape(nb, _S5_CH, LANES),
                                   st_s5im.reshape(nb, _S5_CH, LANES), wts["s5"], cfg["s5_sl"])
    o_s5 = o_s5.reshape(m, BRANCH_WIDTH)
    s5re_new = s5re_new.reshape(nb, S5_GROUPS, S5_N)
    s5im_new = s5im_new.reshape(nb, S5_GROUPS, S5_N)

    o_lru, lru_new = _lru(z.reshape(nb, length, Z_MIX), _pad_state_rows(st_lconv, _HALO), st_lru, wts["lru"],
                          cfg["s5_sl"])
    o_lru = o_lru.reshape(m, BRANCH_WIDTH)
    lconv_new = z[:, COL_LX:COL_LX + BRANCH_WIDTH].reshape(nb, length, BRANCH_WIDTH)[:, length - (LRU_CONV - 1):]

    lam_init = 0.8 - 0.6 * math.exp(-0.3 * l)
    q_rot, k_rot = _rope(z, cfg["rope"], nb, length, cfg["tm_rope"])
    v_new = z[:, COL_DV:COL_DV + DA_HEADS * DA_DV]
    if paged is None:
        o_da = _attn_prompt(q_rot, k_rot, z, wts["lam_p"], wts["da_norm"], nb, length,
                            cfg["tq"], cfg["tk"], lam_init)
    else:
        page_table, cache_k, cache_v = paged
        padk = ((0, 0), (0, PAGE_SIZE - length), (0, 0))
        k_pad = jnp.pad(k_rot.reshape(nb, length, -1), padk)
        v_pad = jnp.pad(v_new.reshape(nb, length, -1), padk)
        o_da = _attn_sample(q_rot, cache_k, cache_v, l, page_table, k_pad, v_pad, wts["lam_p"],
                            wts["da_norm"], lam_init)

    x1 = _merge((o_gla, o_s5, o_lru, o_da), gates, wts["w_branch"], wts["w_out"], l, x2d, g1, gdiv(cfg["tm_mg"]),
                wts["ln1_g"], wts["ln1_b"], cfg["tm_mg"])

    fcst = _pad_state_rows(st_fconv, _HALO).reshape(ng_seq, nseq, _HALO, 2 * D_FF)
    fsl = cfg["ffn_sl"]
    x2, sta, stv = _ffn(x1, sc2, sh2, g2, gdiv(nseq * fsl), fcst, wts["w_up"], wts["ffn_conv_w"],
                        wts["ffn_conv_b"], wts["w_down"], l, wts["ln2_g"], wts["ln2_b"], nseq, fsl,
                        length // fsl if nseq == 1 else 1, _FFN_TF)
    fst = jnp.concatenate([sta, stv], axis=-1)
    if nseq == 1:
        per_seq = length // fsl
        fst = fst.reshape(nb, per_seq, _HALO, 2 * D_FF)[:, per_seq - 1]
    else:
        fst = fst.reshape(nb, _HALO, 2 * D_FF)
    fconv_new = fst[:, _HALO - (FFN_CONV - 1):]

    new = (k_rot.reshape(nb, length, DA_HEADS, 2, DA_DK), v_new.reshape(nb, length, DA_HEADS, DA_DV),
           gla_new, s5re_new, s5im_new, lru_new, lconv_new, fconv_new)
    return x2, new


def _layer_weights(l, w_in, gla_wa2, gla_ba, gla_norm, s5_raw, lru_raw, lam_raw, da_norm, w_branch, w_out,
                   ln1_g, ln1_b, ffn_w_up, ffn_conv_w, ffn_conv_b, ffn_w_down, ln2_g, ln2_b):
    w_in_t = jnp.swapaxes(w_in, 1, 2).astype(BF16)
    wa2p = jnp.pad(gla_wa2[l], ((0, LANES - GLA_LOWRANK), (0, 0))).astype(BF16)
    row = lambda a: a[l].reshape(1, -1)
    cw, cb, wa, ba, wx, bx, lam = [a[l] for a in lru_raw]
    return dict(
        w_in_t=w_in_t, wa2p=wa2p, gla_ba=row(gla_ba), gla_norm=row(gla_norm),
        s5=_s5_params(*[a[l] for a in s5_raw]),
        lru=dict(cw=cw, cb=cb.reshape(1, -1), wa=wa.astype(BF16), ba=ba.reshape(1, -1),
                 wx=wx.astype(BF16), bx=bx.reshape(1, -1), lam=lam.reshape(1, -1)),
        lam_p=tuple(row(a) for a in lam_raw), da_norm=row(da_norm),
        w_branch=w_branch.astype(BF16), w_out=w_out.astype(BF16),
        ln1_g=row(ln1_g), ln1_b=row(ln1_b),
        w_up=ffn_w_up.astype(BF16), ffn_conv_w=ffn_conv_w[l], ffn_conv_b=ffn_conv_b[l],
        w_down=ffn_w_down.astype(BF16), ln2_g=row(ln2_g), ln2_b=row(ln2_b))


def kernel(x_prompt, x_sample, cache_k, cache_v, state_gla, state_s5_re, state_s5_im, state_lru, state_lru_conv, state_ffn_conv, page_table, c_prompt, c_sample, w_ada, b_ada, w_in, gla_wa2, gla_ba, gla_norm, s5_a_re, s5_a_im, s5_log_dt, s5_b_re, s5_b_im, s5_c_re, s5_c_im, s5_d, s5_w_glu, s5_b_glu, lru_conv_w, lru_conv_b, lru_w_a, lru_b_a, lru_w_x, lru_b_x, lru_lambda, da_lq1, da_lk1, da_lq2, da_lk2, da_norm, w_branch, w_out, ln1_g, ln1_b, ffn_w_up, ffn_conv_w, ffn_conv_b, ffn_w_down, ln2_g, ln2_b):
    bp, lp, d = x_prompt.shape
    bs, ls, _ = x_sample.shape
    past_len = page_table.shape[1] * PAGE_SIZE

    rows = bp + bs
    rows_pad = -(-rows // SUBLANES) * SUBLANES
    c_all = jnp.pad(jnp.concatenate([c_prompt, c_sample], axis=0), ((0, rows_pad - rows), (0, 0)))
    mod_all = _ada(c_all, w_ada, b_ada)

    def mods(l, lo, n, per_row):
        parts = jnp.split(mod_all[l, lo:lo + n], 6, axis=-1)
        if per_row:
            return [jnp.repeat(p, per_row, axis=0)[None] for p in parts]
        return [p[:, None, :] for p in parts]

    cfg_p = dict(nb=bp, len=lp, ng=bp, nseq=1, sl=256, s5_sl=128, ffn_sl=512, tm_in=1024, tng=1024, tnm=512, gla_c=128,
                 gla_seq=128, tm_rope=512, tq=1024, tk=512, tm_mg=512, tf=512,
                 gdiv=lambda tm: lp // tm, rope=_rope_tables(0, lp))
    cfg_s = dict(nb=bs, len=ls, ng=1, nseq=bs, sl=ls, s5_sl=ls, ffn_sl=ls, tm_in=bs * ls, tng=1024, tnm=512, gla_c=bs * ls,
                 gla_seq=ls, tm_rope=ls, tm_mg=bs * ls, tf=512,
                 gdiv=lambda tm: 1, rope=_rope_tables(past_len, ls))

    zeros_p = (jnp.zeros((bp, GLA_HEADS, GLA_DK, GLA_DV), F32), jnp.zeros((bp, S5_GROUPS, S5_N), F32),
               jnp.zeros((bp, S5_GROUPS, S5_N), F32), jnp.zeros((bp, BRANCH_WIDTH), F32),
               jnp.zeros((bp, LRU_CONV - 1, BRANCH_WIDTH), F32), jnp.zeros((bp, FFN_CONV - 1, 2 * D_FF), F32))

    xp = x_prompt.reshape(bp * lp, d)
    xs = x_sample.reshape(bs * ls, d)
    col_p = [[] for _ in range(8)]
    col_s = [[] for _ in range(8)]
    s5_raw = (s5_a_re, s5_a_im, s5_log_dt, s5_b_re, s5_b_im, s5_c_re, s5_c_im, s5_d, s5_w_glu, s5_b_glu)
    lru_raw = (lru_conv_w, lru_conv_b, lru_w_a, lru_b_a, lru_w_x, lru_b_x, lru_lambda)
    lam_raw = (da_lq1, da_lk1, da_lq2, da_lk2)
    for l in range(DEPTH):
        wts = _layer_weights(l, w_in, gla_wa2, gla_ba, gla_norm, s5_raw, lru_raw, lam_raw, da_norm, w_branch,
                             w_out, ln1_g, ln1_b, ffn_w_up, ffn_conv_w, ffn_conv_b, ffn_w_down, ln2_g, ln2_b)
        xp, new_p = _run_layer(xp, mods(l, 0, bp, 0), cfg_p, wts, l, zeros_p, None)
        st_s = (state_gla[l], state_s5_re[l], state_s5_im[l], state_lru[l], state_lru_conv[l], state_ffn_conv[l])
        xs, new_s = _run_layer(xs, mods(l, bp, bs, ls), cfg_s, wts, l, st_s, (page_table, cache_k, cache_v))
        for lst, s in zip(col_p, new_p):
            lst.append(s)
        for lst, s in zip(col_s, new_s):
            lst.append(s)

    sp = [jnp.stack(s) for s in col_p]
    ss = [jnp.stack(s) for s in col_s]
    out = [xp.reshape(bp, lp, d), xs.reshape(bs, ls, d)]
    for a, b in zip(sp, ss):
        out += [a, b]
    return tuple(out)
```

```python
import functools
import math

import numpy as np
import jax
import jax.numpy as jnp
from jax import lax
from jax.experimental import pallas as pl
from jax.experimental.pallas import tpu as pltpu

F32 = jnp.float32
BF16 = jnp.bfloat16

D_MODEL = 2048
DEPTH = 2
PAGE_SIZE = 128
N_BRANCH = 4
BRANCH_WIDTH = 512
GLA_HEADS = 4
GLA_DK = 64
GLA_DV = 128
GLA_LOWRANK = 16
GLA_TAU = 16.0
S5_GROUP = 16
S5_GROUPS = 32
S5_N = 64
S5_STATES = S5_GROUPS * S5_N
LRU_HEADS = 4
LRU_BLOCK = 128
LRU_CONV = 4
LRU_C = 8.0
DA_HEADS = 4
DA_DK = 64
DA_DV = 128
ROPE_DIM = 16
ROPE_THETA = 500000.0
D_FF = 5632
FFN_CONV = 3
DEEPNORM_ALPHA = (2.0 * DEPTH) ** 0.25
EPS = 1e-5

LANES = 128
SUBLANES = 8
VMEM_LIMIT = 58 * 1024 * 1024

Z_GATE = N_BRANCH * D_MODEL
Z_MIX = 4608
COL_GQ, COL_GK, COL_GV, COL_OG = 0, 256, 512, 1024
COL_SU, COL_LX, COL_LG = 1536, 2048, 2560
COL_DQ, COL_DK, COL_DV = 3072, 3584, 4096

_NT = (((1,), (1,)), ((), ()))
_TN = (((0,), (0,)), ((), ()))


def _cparams(sem):
    return pltpu.CompilerParams(dimension_semantics=sem, vmem_limit_bytes=VMEM_LIMIT)


def _dot(a, b):
    return jnp.dot(a, b, preferred_element_type=F32)


def _sigmoid(x):
    return 0.5 * jnp.tanh(0.5 * x) + 0.5


def _gelu(x):
    return 0.5 * x * (1.0 + jnp.tanh(math.sqrt(2.0 / math.pi) * (x + 0.044715 * (x * x * x))))


def _layer_norm(y, g, b):
    mu = jnp.mean(y, axis=-1, keepdims=True)
    d = y - mu
    var = jnp.mean(d * d, axis=-1, keepdims=True)
    return d * lax.rsqrt(var + EPS) * g + b


def _ada_kernel(c_ref, w_ref, b_ref, o_ref):
    c = c_ref[...]
    s = c * _sigmoid(c)
    o_ref[...] = _dot(s.astype(BF16), w_ref[...].astype(BF16)) + b_ref[...]


def _ada(c_all, w_ada, b_ada):
    rows = c_all.shape[0]
    n = w_ada.shape[-1]
    tn = 1536
    return pl.pallas_call(
        _ada_kernel,
        grid=(DEPTH, n // tn),
        in_specs=[pl.BlockSpec((rows, D_MODEL), lambda l, j: (0, 0)),
                  pl.BlockSpec((None, D_MODEL, tn), lambda l, j: (l, 0, j)),
                  pl.BlockSpec((None, 1, tn), lambda l, j: (l, 0, j))],
        out_specs=pl.BlockSpec((None, rows, tn), lambda l, j: (l, 0, j)),
        out_shape=jax.ShapeDtypeStruct((DEPTH, rows, n), F32),
        compiler_params=_cparams(("arbitrary", "arbitrary")),
        name="ada",
    )(c_all, w_ada, b_ada.reshape(DEPTH, 1, n))


def _inproj_kernel(x_ref, sc_ref, sh_ref, wg_ref, wm_ref, wlr_ref, zg_ref, zm_ref, zlr_ref, u_scr, *, ngate):
    j = pl.program_id(1)

    @pl.when(j == 0)
    def _():
        u = (x_ref[...] * (1.0 + sc_ref[...]) + sh_ref[...]).astype(BF16)
        u_scr[...] = u
        zlr_ref[...] = lax.dot_general(u, wlr_ref[0], _NT, preferred_element_type=F32)

    @pl.when(j < ngate)
    def _():
        zg_ref[...] = lax.dot_general(u_scr[...], wg_ref[0], _NT, preferred_element_type=F32).astype(BF16)

    @pl.when(j >= ngate)
    def _():
        zm_ref[...] = lax.dot_general(u_scr[...], wm_ref[0], _NT, preferred_element_type=F32)


def _inproj(x2d, sc, sh, gdiv, wt, layer, tm, tng, tnm):
    m = x2d.shape[0]
    r = sc.shape[1]
    ngate, nmix = Z_GATE // tng, Z_MIX // tnm
    n_qkv = 1024
    rest0 = n_qkv + GLA_LOWRANK
    gate0 = wt.shape[1] - Z_GATE
    assert n_qkv % tnm == 0

    def mix_row(t):
        return jnp.where(t * tnm >= n_qkv, t * tnm + (rest0 - n_qkv), t * tnm)

    row_align = 2 * SUBLANES
    assert gate0 % row_align == 0 and rest0 % row_align == 0
    aligned = lambda r: pl.multiple_of(r, row_align)

    def el_spec(rows, index_map):
        return pl.BlockSpec((pl.Element(1), pl.Element(rows), pl.Element(D_MODEL)), index_map)
    mod_spec = pl.BlockSpec((None, r, D_MODEL), lambda i, j: (i // gdiv, 0, 0))
    kern = functools.partial(_inproj_kernel, ngate=ngate)
    return pl.pallas_call(
        kern,
        grid=(m // tm, ngate + nmix),
        in_specs=[pl.BlockSpec((tm, D_MODEL), lambda i, j: (i, 0)), mod_spec, mod_spec,
                  el_spec(tng, lambda i, j: (layer, aligned(gate0 + jnp.minimum(j, ngate - 1) * tng), 0)),
                  el_spec(tnm, lambda i, j: (layer, aligned(mix_row(jnp.maximum(j - ngate, 0))), 0)),
                  el_spec(LANES, lambda i, j: (layer, n_qkv, 0))],
        out_specs=[pl.BlockSpec((tm, tng), lambda i, j: (i, jnp.minimum(j, ngate - 1))),
                   pl.BlockSpec((tm, tnm), lambda i, j: (i, jnp.maximum(j - ngate, 0))),
                   pl.BlockSpec((tm, LANES), lambda i, j: (i, 0))],
        out_shape=[jax.ShapeDtypeStruct((m, Z_GATE), BF16), jax.ShapeDtypeStruct((m, Z_MIX), F32),
                   jax.ShapeDtypeStruct((m, LANES), F32)],
        scratch_shapes=[pltpu.VMEM((tm, D_MODEL), BF16)],
        compiler_params=_cparams(("arbitrary", "arbitrary")),
        name="inproj",
    )(x2d, sc, sh, wt, wt, wt)


def _rope_tables(pos0, length):
    half = ROPE_DIM // 2
    inv = ROPE_THETA ** (-jnp.arange(half, dtype=F32) * 2.0 / ROPE_DIM)
    ang = (pos0 + jnp.arange(length)).astype(F32)[:, None] * inv
    cos, sin = jnp.cos(ang), jnp.sin(ang)
    ones = jnp.ones((length, DA_DK - ROPE_DIM), F32)
    zeros = jnp.zeros((length, DA_DK - ROPE_DIM), F32)
    zh = jnp.zeros((length, half), F32)
    c = jnp.concatenate([cos, cos, ones], axis=1)
    s_next = jnp.concatenate([-sin, zh, zeros], axis=1)
    s_prev = jnp.concatenate([zh, sin, zeros], axis=1)
    rep = LANES // DA_DK
    return jnp.tile(c, (1, rep)), jnp.tile(s_next, (1, rep)), jnp.tile(s_prev, (1, rep))


def _rope_kernel(q_ref, k_ref, c_ref, sn_ref, sp_ref, qo_ref, ko_ref):
    half = ROPE_DIM // 2
    c, sn, sp = c_ref[...], sn_ref[...], sp_ref[...]
    for src, dst in ((q_ref, qo_ref), (k_ref, ko_ref)):
        for i in range(src.shape[1] // LANES):
            x = src[:, i * LANES:(i + 1) * LANES]
            nxt = pltpu.roll(x, LANES - half, axis=1)
            prv = pltpu.roll(x, half, axis=1)
            dst[:, i * LANES:(i + 1) * LANES] = x * c + nxt * sn + prv * sp


def _rope(z, tabs, nb, length, tm):
    m = z.shape[0]
    w = DA_HEADS * 2 * DA_DK
    nj = length // tm
    tab_spec = pl.BlockSpec((tm, LANES), lambda b, j: (j, 0))
    return pl.pallas_call(
        _rope_kernel,
        grid=(nb, nj),
        in_specs=[pl.BlockSpec((tm, w), lambda b, j: (b * nj + j, COL_DQ // w)),
                  pl.BlockSpec((tm, w), lambda b, j: (b * nj + j, COL_DK // w)),
                  tab_spec, tab_spec, tab_spec],
        out_specs=[pl.BlockSpec((tm, w), lambda b, j: (b * nj + j, 0))] * 2,
        out_shape=[jax.ShapeDtypeStruct((m, w), F32)] * 2,
        compiler_params=_cparams(("arbitrary", "arbitrary")),
        name="rope",
    )(z, z, *tabs)


def _gla_consts(c, seq):
    nlev = int(math.log2(seq))
    t = np.arange(c)
    tt, rr = t[:, None], t[None, :]
    same = (tt // seq) == (rr // seq)
    tril = same & (rr <= tt)
    lvl = np.full((c, c), -1, np.int32)
    lvl[t, t] = nlev
    for i in range(nlev):
        b = seq >> (i + 1)
        valid = ((tt // (2 * b)) == (rr // (2 * b))) & ((tt % (2 * b)) >= b) & ((rr % (2 * b)) < b)
        lvl[valid] = i
    return tril.astype(np.float32), lvl, nlev


def _gla_kernel(q_ref, k_ref, v_ref, og_ref, lr_ref, s0_ref, lmat_ref, lvl_ref, wa2_ref, ba_ref, gn_ref,
                o_ref, st_ref, cum_scr, *, c, seq, nlev):
    nb = c // seq
    j = pl.program_id(1)

    @pl.when(j == 0)
    def _():
        st_ref[...] = s0_ref[...]

    x = _dot(lr_ref[...].astype(BF16), wa2_ref[...]) + ba_ref[...]
    la = (jnp.minimum(x, 0.0) - jnp.log1p(jnp.exp(-jnp.abs(x)))) * (1.0 / GLA_TAU)
    cum = jnp.dot(lmat_ref[...], la, precision=lax.Precision.HIGHEST, preferred_element_type=F32)
    cum_scr[...] = cum
    wq = cum.shape[1]

    def cum_at(idx, group):
        sub = lax.broadcasted_iota(jnp.int32, (SUBLANES, wq), 0)
        bcast = lambda r: jnp.broadcast_to(cum_scr[r:r + 1, :], (SUBLANES, wq))
        tiles = []
        for t0 in range(0, c, SUBLANES):
            tile = bcast(idx(t0))
            for g in range(group, SUBLANES, group):
                tile = jnp.where(sub >= g, bcast(idx(t0 + g)), tile)
            tiles.append(tile)
        return jnp.concatenate(tiles, axis=0)

    e_end = cum_at(lambda t: (t // seq) * seq + seq - 1, seq) - cum
    e_lvl = [cum - cum_at(lambda t, b=seq >> (i + 1): (t // (2 * b)) * (2 * b) + b - 1, 2 * (seq >> (i + 1)))
             for i in range(nlev)]
    q = q_ref[...] * (GLA_DK ** -0.5)
    k = k_ref[...]
    lvl = lvl_ref[...]
    lane = lax.broadcasted_iota(jnp.int32, (c, LANES), 1)
    if nb > 1:
        rowb = lax.broadcasted_iota(jnp.int32, (c, nb * LANES), 0) // seq
        colb = lax.broadcasted_iota(jnp.int32, (c, nb * LANES), 1) // LANES
        bmask = rowb == colb

    def spread(a):
        if nb == 1:
            return a
        return jnp.where(bmask, jnp.concatenate([a] * nb, axis=1), 0.0)

    for p in range(2):
        sl = slice(p * LANES, (p + 1) * LANES)
        qp, kp, cump = q[:, sl], k[:, sl], cum[:, sl]
        qts, kts = [], []
        for i in range(nlev):
            eb = e_lvl[i][:, sl]
            qts.append(qp * jnp.exp(eb))
            kts.append((kp * jnp.exp(-eb)).astype(BF16))
        qg = qp * jnp.exp(cump)
        kh = kp * jnp.exp(e_end[:, sl])
        kpb = kp.astype(BF16)
        decays = []
        for b in range(nb):
            cl = cump[b * seq + seq - 1:b * seq + seq, :]
            decays.append(jnp.broadcast_to(jnp.exp(cl), (LANES, LANES)).T)
        for hh in range(2):
            h = 2 * p + hh
            hm = (lane // GLA_DK) == hh
            att = lax.dot_general(jnp.where(hm, qp, 0.0).astype(BF16), kpb, _NT, preferred_element_type=F32)
            att = jnp.where(lvl == nlev, att, 0.0)
            for i in range(nlev):
                a = lax.dot_general(jnp.where(hm, qts[i], 0.0).astype(BF16), kts[i], _NT,
                                    preferred_element_type=F32)
                att = jnp.where(lvl == i, a, att)
            vh = v_ref[:, h * GLA_DV:(h + 1) * GLA_DV].astype(BF16)
            o = _dot(att.astype(BF16), vh)
            s_all = st_ref[:, h].reshape(nb * LANES, GLA_DV)
            o = o + _dot(spread(jnp.where(hm, qg, 0.0)).astype(BF16), s_all.astype(BF16))
            upd = lax.dot_general(spread(jnp.where(hm, kh, 0.0)).astype(BF16), vh, _TN,
                                  preferred_element_type=F32)
            for b in range(nb):
                st_ref[b, h] = st_ref[b, h] * decays[b] + upd[b * LANES:(b + 1) * LANES]
            ms = jnp.mean(o * o, axis=-1, keepdims=True)
            og = og_ref[:, h * GLA_DV:(h + 1) * GLA_DV]
            o_ref[:, h * GLA_DV:(h + 1) * GLA_DV] = o * lax.rsqrt(ms + EPS) * gn_ref[...] * (og * _sigmoid(og))


def _gla(z, zlr, s0_pad, wa2p, ba, gn, ng, length, c, seq):
    m = z.shape[0]
    nb = c // seq
    nj = length // c
    lmat, lvl, nlev = _gla_consts(c, seq)
    w = GLA_HEADS * GLA_DK
    wv = GLA_HEADS * GLA_DV
    row = lambda g, j: g * nj + j
    const2 = lambda g, j: (0, 0)
    st_spec = pl.BlockSpec((nb, GLA_HEADS, LANES, GLA_DV), lambda g, j: (g, 0, 0, 0))
    kern = functools.partial(_gla_kernel, c=c, seq=seq, nlev=nlev)
    return pl.pallas_call(
        kern,
        grid=(ng, nj),
        in_specs=[pl.BlockSpec((c, w), lambda g, j: (row(g, j), COL_GQ // w)),
                  pl.BlockSpec((c, w), lambda g, j: (row(g, j), COL_GK // w)),
                  pl.BlockSpec((c, wv), lambda g, j: (row(g, j), COL_GV // wv)),
                  pl.BlockSpec((c, wv), lambda g, j: (row(g, j), COL_OG // wv)),
                  pl.BlockSpec((c, LANES), lambda g, j: (row(g, j), 0)),
                  st_spec,
                  pl.BlockSpec(lmat.shape, const2),
                  pl.BlockSpec(lvl.shape, const2),
                  pl.BlockSpec((LANES, w), const2),
                  pl.BlockSpec((1, w), const2),
                  pl.BlockSpec((1, GLA_DV), const2)],
        out_specs=[pl.BlockSpec((c, wv), lambda g, j: (row(g, j), 0)), st_spec],
        out_shape=[jax.ShapeDtypeStruct((m, wv), F32),
                   jax.ShapeDtypeStruct(s0_pad.shape, F32)],
        scratch_shapes=[pltpu.VMEM((c, w), F32)],
        compiler_params=_cparams(("arbitrary", "arbitrary")),
        name="gla",
    )(z, z, z, z, zlr, s0_pad, jnp.asarray(lmat), jnp.asarray(lvl), wa2p, ba, gn)


def _gla_state_pad(s):
    b = s.shape[0]
    s6 = s.reshape(b, 2, 2, 1, GLA_DK, GLA_DV) * jnp.eye(2, dtype=F32)[None, None, :, :, None, None]
    return s6.reshape(b, GLA_HEADS, LANES, GLA_DV)


def _gla_state_unpad(sp):
    b = sp.shape[0]
    s6 = sp.reshape(b, 2, 2, 2, GLA_DK, GLA_DV)
    return jnp.stack([s6[:, :, 0, 0], s6[:, :, 1, 1]], axis=2).reshape(b, GLA_HEADS, GLA_DK, GLA_DV)


_S5_CH = S5_STATES // LANES


def _s5_kernel(u_ref, h0r_ref, h0i_ref, are_ref, aim_ref, ldt_ref, bdr_ref, bdi_ref, cdr_ref, cdi_ref,
               d_ref, wg_ref, bg_ref, y_ref, hfr_ref, hfi_ref, sre, sim, hsc, hcr, hci, *, nseq, sl):
    tm = nseq * sl
    w = BRANCH_WIDTH
    j = pl.program_id(0)

    @pl.when(j == 0)
    def _():
        hcr[...] = h0r_ref[...]
        hci[...] = h0i_ref[...]

    lr, li = are_ref[...], aim_ref[...]
    dt = jnp.exp(ldt_ref[...])
    mag = jnp.exp(lr * dt)
    abr, abi = mag * jnp.cos(li * dt), mag * jnp.sin(li * dt)
    den = lr * lr + li * li
    fr = ((abr - 1.0) * lr + abi * li) / den
    fi = (abi * lr - (abr - 1.0) * li) / den

    u = u_ref[...].reshape(tm, w)
    ub = u.astype(BF16)
    for c2 in range(_S5_CH // 2):
        ws = slice((c2 // 2) * LANES, (c2 // 2 + 1) * LANES)
        tr = _dot(ub[:, ws], bdr_ref[ws, c2 * 256:(c2 + 1) * 256])
        ti = _dot(ub[:, ws], bdi_ref[ws, c2 * 256:(c2 + 1) * 256])
        for e in range(2):
            ch = 2 * c2 + e
            trc, tic = tr[:, e * LANES:(e + 1) * LANES], ti[:, e * LANES:(e + 1) * LANES]
            frc, fic = fr[ch:ch + 1, :], fi[ch:ch + 1, :]
            sre[ch * tm:(ch + 1) * tm, :] = frc * trc - fic * tic
            sim[ch * tm:(ch + 1) * tm, :] = frc * tic + fic * trc

    ar0, ai0, ar1, ai1 = abr[0:8], abi[0:8], abr[8:16], abi[8:16]
    grp = min(nseq, 4)
    for g0 in range(0, nseq, grp):
        def body(t, carry, g0=g0):
            out = []
            for q in range(grp):
                r0, i0, r1, i1 = carry[4 * q:4 * q + 4]
                row = (g0 + q) * sl + t
                lo = pl.ds(row, SUBLANES, stride=tm)
                hi = pl.ds(SUBLANES * tm + row, SUBLANES, stride=tm)
                n_r0 = ar0 * r0 - ai0 * i0 + sre[lo, :]
                n_i0 = ar0 * i0 + ai0 * r0 + sim[lo, :]
                n_r1 = ar1 * r1 - ai1 * i1 + sre[hi, :]
                n_i1 = ar1 * i1 + ai1 * r1 + sim[hi, :]
                dst = pl.ds(pl.multiple_of(row * SUBLANES, SUBLANES), SUBLANES)
                hsc[0, dst, :] = n_r0
                hsc[1, dst, :] = n_i0
                hsc[2, dst, :] = n_r1
                hsc[3, dst, :] = n_i1
                out += [n_r0, n_i0, n_r1, n_i1]
            return tuple(out)

        init = tuple(a for q in range(grp) for a in (hcr[g0 + q, 0:8], hci[g0 + q, 0:8],
                                                     hcr[g0 + q, 8:16], hci[g0 + q, 8:16]))
        fin = lax.fori_loop(0, sl, body, init, unroll=8)
        for q in range(grp):
            hcr[g0 + q, 0:8] = fin[4 * q]
            hci[g0 + q, 0:8] = fin[4 * q + 1]
            hcr[g0 + q, 8:16] = fin[4 * q + 2]
            hci[g0 + q, 8:16] = fin[4 * q + 3]

    ywin = []
    for wi in range(w // LANES):
        ws = slice(wi * LANES, (wi + 1) * LANES)
        yw = d_ref[:, ws] * u[:, ws]
        for c2 in (2 * wi, 2 * wi + 1):
            part = 2 * ((2 * c2) // SUBLANES)
            ca, cb = (2 * c2) % SUBLANES, (2 * c2 + 1) % SUBLANES
            rows = lambda cc: pl.ds(cc, tm, stride=SUBLANES)
            hr, hi = hsc.at[part], hsc.at[part + 1]
            hre = jnp.concatenate([hr[rows(ca), :], hr[rows(cb), :]], axis=1).astype(BF16)
            him = jnp.concatenate([hi[rows(ca), :], hi[rows(cb), :]], axis=1).astype(BF16)
            yw = yw + _dot(hre, cdr_ref[c2 * 256:(c2 + 1) * 256, ws]) - _dot(him, cdi_ref[c2 * 256:(c2 + 1) * 256, ws])
        ywin.append(yw)
    y = _gelu(jnp.concatenate(ywin, axis=1))
    y = y * _sigmoid(_dot(y.astype(BF16), wg_ref[...]) + bg_ref[...])
    y_ref[...] = y.reshape(nseq, sl, w)

    @pl.when(j == pl.num_programs(0) - 1)
    def _():
        hfr_ref[...] = hcr[...]
        hfi_ref[...] = hci[...]


def _s5(z3, h0r, h0i, p, sl):
    nseq, length, _ = z3.shape
    tm = nseq * sl
    w = BRANCH_WIDTH
    const2 = lambda j: (0, 0)
    st_spec = pl.BlockSpec((nseq, _S5_CH, LANES), lambda j: (0, 0, 0))
    vec = pl.BlockSpec((_S5_CH, LANES), const2)
    kern = functools.partial(_s5_kernel, nseq=nseq, sl=sl)
    return pl.pallas_call(
        kern,
        grid=(length // sl,),
        in_specs=[pl.BlockSpec((nseq, sl, w), lambda j: (0, j, COL_SU // w)),
                  st_spec, st_spec, vec, vec, vec,
                  pl.BlockSpec((w, S5_STATES), const2), pl.BlockSpec((w, S5_STATES), const2),
                  pl.BlockSpec((S5_STATES, w), const2), pl.BlockSpec((S5_STATES, w), const2),
                  pl.BlockSpec((1, w), const2), pl.BlockSpec((w, w), const2), pl.BlockSpec((1, w), const2)],
        out_specs=[pl.BlockSpec((nseq, sl, w), lambda j: (0, j, 0)), st_spec, st_spec],
        out_shape=[jax.ShapeDtypeStruct((nseq, length, w), F32),
                   jax.ShapeDtypeStruct(h0r.shape, F32), jax.ShapeDtypeStruct(h0i.shape, F32)],
        scratch_shapes=[pltpu.VMEM((_S5_CH * tm, LANES), F32), pltpu.VMEM((_S5_CH * tm, LANES), F32),
                        pltpu.VMEM((4, SUBLANES * tm, LANES), F32),
                        pltpu.VMEM((nseq, _S5_CH, LANES), F32), pltpu.VMEM((nseq, _S5_CH, LANES), F32)],
        compiler_params=_cparams(("arbitrary",)),
        name="s5",
    )(z3, h0r, h0i, p["are"], p["aim"], p["ldt"], p["bdr"], p["bdi"], p["cdr"], p["cdi"],
      p["d"], p["wg"], p["bg"])


def _s5_params(a_re, a_im, log_dt, b_re, b_im, c_re, c_im, d, w_glu, b_glu):
    eye = jnp.eye(S5_GROUPS, dtype=F32)

    def bd_in(b):
        return (eye[:, None, :, None] * jnp.swapaxes(b, 1, 2)[:, :, None, :]).reshape(
            S5_GROUPS * S5_GROUP, S5_STATES).astype(BF16)

    def bd_out(c):
        return (eye[:, None, :, None] * jnp.swapaxes(c, 1, 2)[:, :, None, :]).reshape(
            S5_STATES, S5_GROUPS * S5_GROUP).astype(BF16)

    return dict(are=a_re.reshape(_S5_CH, LANES), aim=a_im.reshape(_S5_CH, LANES),
                ldt=jnp.broadcast_to(log_dt[:, None], (S5_GROUPS, S5_N)).reshape(_S5_CH, LANES),
                bdr=bd_in(b_re), bdi=bd_in(b_im), cdr=bd_out(c_re), cdi=bd_out(c_im),
                d=d.reshape(1, -1), wg=w_glu.astype(BF16), bg=b_glu.reshape(1, -1))


_HALO = SUBLANES


def _lru_kernel(lx_ref, lg_ref, zprev_ref, cst_ref, h0_ref, cw_ref, cb_ref, wa_ref, ba_ref, wx_ref, bx_ref,
                lam_ref, y_ref, hfin_ref, cs_scr, xc_scr, a_scr, b_scr, h_c, *, nseq, sl):
    tm = nseq * sl
    j = pl.program_id(0)

    @pl.when(j == 0)
    def _():
        h_c[...] = h0_ref[...]

    k = LRU_CONV
    for s in range(nseq):
        cs_scr[0:_HALO, :] = jnp.where(j == 0, cst_ref[s], zprev_ref[s])
        cs_scr[_HALO:_HALO + sl, :] = lx_ref[s]
        xc = cb_ref[...]
        for t in range(k):
            off = _HALO - (k - 1) + t
            xc = xc + cw_ref[t:t + 1, :] * cs_scr[off:off + sl, :]
        xc_scr[s * sl:(s + 1) * sl, :] = xc

    lam = lam_ref[...]
    sp = jnp.maximum(-lam, 0.0) + jnp.log1p(jnp.exp(-jnp.abs(lam)))
    for h in range(LRU_HEADS):
        hs = slice(h * LRU_BLOCK, (h + 1) * LRU_BLOCK)
        xc = xc_scr[:, hs]
        xb = xc.astype(BF16)
        r = _sigmoid(_dot(xb, wa_ref[h]) + ba_ref[:, hs])
        i = _sigmoid(_dot(xb, wx_ref[h]) + bx_ref[:, hs])
        la = -LRU_C * r * sp[:, hs]
        a_scr[:, hs] = jnp.exp(la)
        b_scr[:, hs] = jnp.sqrt(-jnp.tanh(la) * (jnp.exp(2.0 * la) + 1.0)) * (i * xc)

    grp = min(nseq, 4)
    for g0 in range(0, nseq, grp):
        def body(t, hs, g0=g0):
            out = []
            for q in range(grp):
                idx = pl.ds((g0 + q) * sl + t, 1)
                h = a_scr[idx, :] * hs[q] + b_scr[idx, :]
                b_scr[idx, :] = h
                out.append(h)
            return tuple(out)

        fin = lax.fori_loop(0, sl, body, tuple(h_c[g0 + q:g0 + q + 1, :] for q in range(grp)), unroll=8)
        for q in range(grp):
            h_c[g0 + q:g0 + q + 1, :] = fin[q]

    y = b_scr[...] * _gelu(lg_ref[...].reshape(tm, BRANCH_WIDTH))
    y_ref[...] = y.reshape(nseq, sl, BRANCH_WIDTH)

    @pl.when(j == pl.num_programs(0) - 1)
    def _():
        hfin_ref[...] = h_c[...]


def _lru(z3, cst, h0, p, sl):
    nseq, length, _ = z3.shape
    tm = nseq * sl
    w = BRANCH_WIDTH
    const2 = lambda j: (0, 0)
    const3 = lambda j: (0, 0, 0)
    st_spec = pl.BlockSpec((nseq, w), const2)
    vec = pl.BlockSpec((1, w), const2)
    wsp = pl.BlockSpec((LRU_HEADS, LRU_BLOCK, LRU_BLOCK), const3)
    rb = sl // _HALO
    kern = functools.partial(_lru_kernel, nseq=nseq, sl=sl)
    return pl.pallas_call(
        kern,
        grid=(length // sl,),
        in_specs=[pl.BlockSpec((nseq, sl, w), lambda j: (0, j, COL_LX // w)),
                  pl.BlockSpec((nseq, sl, w), lambda j: (0, j, COL_LG // w)),
                  pl.BlockSpec((nseq, _HALO, w), lambda j: (0, jnp.maximum(j * rb - 1, 0), COL_LX // w)),
                  pl.BlockSpec((nseq, _HALO, w), const3),
                  st_spec,
                  pl.BlockSpec((LRU_CONV, w), const2), vec, wsp, vec, wsp, vec, vec],
        out_specs=[pl.BlockSpec((nseq, sl, w), lambda j: (0, j, 0)), st_spec],
        out_shape=[jax.ShapeDtypeStruct((nseq, length, w), F32), jax.ShapeDtypeStruct(h0.shape, F32)],
        scratch_shapes=[pltpu.VMEM((_HALO + sl, w), F32), pltpu.VMEM((tm, w), F32),
                        pltpu.VMEM((tm, w), F32), pltpu.VMEM((tm, w), F32),
                        pltpu.VMEM((nseq, w), F32)],
        compiler_params=_cparams(("arbitrary",)),
        name="lru",
    )(z3, z3, z3, cst, h0, p["cw"], p["cb"], p["wa"], p["ba"], p["wx"], p["bx"], p["lam"])


def _da_lambda(lq1_ref, lk1_ref, lq2_ref, lk2_ref, lam_init):
    return (jnp.exp(jnp.sum(lq1_ref[...] * lk1_ref[...])) - jnp.exp(jnp.sum(lq2_ref[...] * lk2_ref[...]))
            + lam_init)


def _attn_kernel(q_ref, k_ref, v_ref, lq1_ref, lk1_ref, lq2_ref, lk2_ref, dn_ref, o_ref,
                 m_scr, l_scr, acc_scr, *, tq, tk, lam_init):
    qi, ki = pl.program_id(2), pl.program_id(3)

    @pl.when(ki == 0)
    def _():
        m_scr[...] = jnp.full(m_scr.shape, -jnp.inf, F32)
        l_scr[...] = jnp.zeros(l_scr.shape, F32)
        acc_scr[...] = jnp.zeros(acc_scr.shape, F32)

    def step(masked):
        q = q_ref[...] * (DA_DK ** -0.5)
        kb = k_ref[...].astype(BF16)
        vb = v_ref[...].astype(BF16)
        lane = lax.broadcasted_iota(jnp.int32, (tq, LANES), 1)
        if masked:
            row = qi * tq + lax.broadcasted_iota(jnp.int32, (tq, tk), 0)
            col = ki * tk + lax.broadcasted_iota(jnp.int32, (tq, tk), 1)
            mask = col <= row
        comps = range(2)
        ss = [lax.dot_general(jnp.where((lane // DA_DK) == c, q, 0.0).astype(BF16), kb, _NT,
                              preferred_element_type=F32) for c in comps]
        if masked:
            ss = [jnp.where(mask, s, -jnp.inf) for s in ss]
        m_prev = [m_scr[c] for c in comps]
        l_prev = [l_scr[c] for c in comps]
        a_prev = [acc_scr[c] for c in comps]
        m_new = [jnp.maximum(m_prev[c], jnp.max(ss[c], axis=-1, keepdims=True)) for c in comps]
        alpha = [jnp.exp(m_prev[c] - m_new[c]) for c in comps]
        pexp = [jnp.exp(ss[c] - m_new[c]) for c in comps]
        pv = [_dot(pexp[c].astype(BF16), vb) for c in comps]
        for c in comps:
            l_scr[c] = alpha[c] * l_prev[c] + jnp.sum(pexp[c], axis=-1, keepdims=True)
            acc_scr[c] = alpha[c] * a_prev[c] + pv[c]
            m_scr[c] = m_new[c]

    last_col, first_col = ki * tk + tk - 1, ki * tk
    first_row, last_row = qi * tq, qi * tq + tq - 1

    @pl.when(last_col <= first_row)
    def _():
        step(False)

    @pl.when((last_col > first_row) & (first_col <= last_row))
    def _():
        step(True)

    @pl.when(ki == pl.num_programs(3) - 1)
    def _():
        lam = _da_lambda(lq1_ref, lk1_ref, lq2_ref, lk2_ref, lam_init)
        o = acc_scr[0] / l_scr[0] - lam * (acc_scr[1] / l_scr[1])
        ms = jnp.mean(o * o, axis=-1, keepdims=True)
        o_ref[...] = o * lax.rsqrt(ms + EPS) * dn_ref[...] * (1.0 - lam_init)


def _attn_prompt(q_rot, k_rot, z, lam_p, dn, nb, length, tq, tk, lam_init):
    m = q_rot.shape[0]
    nq, nk = length // tq, length // tk
    const2 = lambda b, h, qi, ki: (0, 0)
    kv_row = lambda b, qi, ki: b * nk + jnp.minimum(ki, (qi * tq + tq - 1) // tk)
    vec = pl.BlockSpec((1, DA_DK), const2)
    kern = functools.partial(_attn_kernel, tq=tq, tk=tk, lam_init=lam_init)
    return pl.pallas_call(
        kern,
        grid=(nb, DA_HEADS, nq, nk),
        in_specs=[pl.BlockSpec((tq, LANES), lambda b, h, qi, ki: (b * nq + qi, h)),
                  pl.BlockSpec((tk, LANES), lambda b, h, qi, ki: (kv_row(b, qi, ki), h)),
                  pl.BlockSpec((tk, DA_DV), lambda b, h, qi, ki: (kv_row(b, qi, ki), COL_DV // DA_DV + h)),
                  vec, vec, vec, vec, pl.BlockSpec((1, DA_DV), const2)],
        out_specs=pl.BlockSpec((tq, DA_DV), lambda b, h, qi, ki: (b * nq + qi, h)),
        out_shape=jax.ShapeDtypeStruct((m, DA_HEADS * DA_DV), F32),
        scratch_shapes=[pltpu.VMEM((2, tq, 1), F32), pltpu.VMEM((2, tq, 1), F32),
                        pltpu.VMEM((2, tq, DA_DV), F32)],
        compiler_params=_cparams(("arbitrary", "arbitrary", "arbitrary", "arbitrary")),
        name="attn_prompt",
    )(q_rot, k_rot, z, *lam_p, dn)


_PAGES_PER_STEP = 16
_QROWS = DA_HEADS * 2 * SUBLANES


def _attn_sample_kernel(pt_ref, q_ref, *refs, lq, lam_init):
    npg = _PAGES_PER_STEP
    kt_refs, v_refs = refs[0:npg], refs[npg:2 * npg]
    kn_ref, vn_ref, lq1_ref, lk1_ref, lq2_ref, lk2_ref, dn_ref = refs[2 * npg:2 * npg + 7]
    o_ref = refs[2 * npg + 7]
    qbd_scr, m_scr, l_scr, acc_scr = refs[2 * npg + 8:]
    w = DA_HEADS * 2 * DA_DK
    hrows = 2 * lq
    j = pl.program_id(1)

    @pl.when(j == 0)
    def _():
        q = q_ref[...] * (DA_DK ** -0.5)
        lane = lax.broadcasted_iota(jnp.int32, (lq, w), 1)
        tiles = [jnp.where((lane // DA_DK) == hc, q, 0.0) for hc in range(DA_HEADS * 2)]
        qbd_scr[...] = jnp.concatenate(tiles, axis=0)
        m_scr[...] = jnp.full(m_scr.shape, -jnp.inf, F32)
        l_scr[...] = jnp.zeros(l_scr.shape, F32)
        acc_scr[...] = jnp.zeros(acc_scr.shape, F32)

    qbd = qbd_scr[...].astype(BF16)

    def update(s, nchunk, vget):
        m_prev = m_scr[...]
        m_new = jnp.maximum(m_prev, jnp.max(s, axis=-1, keepdims=True))
        alpha = jnp.exp(m_prev - m_new)
        pexp = jnp.exp(s - m_new)
        l_scr[...] = alpha * l_scr[...] + jnp.sum(pexp, axis=-1, keepdims=True)
        pb = pexp.astype(BF16)
        for h in range(DA_HEADS):
            rs = slice(h * hrows, (h + 1) * hrows)
            pv = _dot(pb[rs, 0:PAGE_SIZE], vget(0, h))
            for i in range(1, nchunk):
                pv = pv + _dot(pb[rs, i * PAGE_SIZE:(i + 1) * PAGE_SIZE], vget(i, h))
            acc_scr[rs, :] = alpha[rs] * acc_scr[rs, :] + pv
        m_scr[...] = m_new

    s_pages = jnp.concatenate([_dot(qbd, kt_refs[i][...].astype(BF16)) for i in range(npg)], axis=1)
    update(s_pages, npg, lambda i, h: v_refs[i][pl.ds(h, PAGE_SIZE, stride=DA_HEADS), :].astype(BF16))

    @pl.when(j == pl.num_programs(1) - 1)
    def _():
        nk = kn_ref.shape[0]
        t = lax.broadcasted_iota(jnp.int32, (_QROWS, nk), 0) % lq
        col = lax.broadcasted_iota(jnp.int32, (_QROWS, nk), 1)
        s_new = lax.dot_general(qbd, kn_ref[...].astype(BF16), _NT, preferred_element_type=F32)
        update(jnp.where(col <= t, s_new, -jnp.inf), 1,
               lambda i, h: vn_ref[:, h * DA_DV:(h + 1) * DA_DV].astype(BF16))
        lam = _da_lambda(lq1_ref, lk1_ref, lq2_ref, lk2_ref, lam_init)
        for h in range(DA_HEADS):
            r0 = h * hrows
            o0 = acc_scr[r0:r0 + lq, :] / l_scr[r0:r0 + lq]
            o1 = acc_scr[r0 + lq:r0 + 2 * lq, :] / l_scr[r0 + lq:r0 + 2 * lq]
            o = o0 - lam * o1
            ms = jnp.mean(o * o, axis=-1, keepdims=True)
            o_ref[:, h * DA_DV:(h + 1) * DA_DV] = o * lax.rsqrt(ms + EPS) * dn_ref[...] * (1.0 - lam_init)


def _attn_sample(q_rot, cache_k, cache_v, layer, page_table, k_new, v_new, lam_p, dn, lam_init):
    nb, n_pages = page_table.shape
    lq = q_rot.shape[0] // nb
    w = DA_HEADS * 2 * DA_DK
    wv = DA_HEADS * DA_DV
    npg = _PAGES_PER_STEP
    nsteps = n_pages // npg
    ckt = jnp.transpose(cache_k, (0, 1, 3, 4, 5, 2)).reshape(DEPTH, -1, w, PAGE_SIZE)
    cv2 = cache_v.reshape(DEPTH, -1, PAGE_SIZE * DA_HEADS, DA_DV)
    const2 = lambda b, j, pt: (0, 0)
    page = lambda b, j, pt, i: pt[b * n_pages + j * npg + i]

    def k_spec(i):
        return pl.BlockSpec((None, None, w, PAGE_SIZE), lambda b, j, pt: (layer, page(b, j, pt, i), 0, 0))

    def v_spec(i):
        return pl.BlockSpec((None, None, PAGE_SIZE * DA_HEADS, DA_DV),
                            lambda b, j, pt: (layer, page(b, j, pt, i), 0, 0))

    vec = pl.BlockSpec((1, DA_DK), const2)
    nk = k_new.shape[1]
    grid_spec = pltpu.PrefetchScalarGridSpec(
        num_scalar_prefetch=1,
        grid=(nb, nsteps),
        in_specs=([pl.BlockSpec((lq, w), lambda b, j, pt: (b, 0))]
                  + [k_spec(i) for i in range(npg)] + [v_spec(i) for i in range(npg)]
                  + [pl.BlockSpec((None, nk, w), lambda b, j, pt: (b, 0, 0)),
                     pl.BlockSpec((None, nk, wv), lambda b, j, pt: (b, 0, 0)),
                     vec, vec, vec, vec, pl.BlockSpec((1, DA_DV), const2)]),
        out_specs=pl.BlockSpec((lq, wv), lambda b, j, pt: (b, 0)),
        scratch_shapes=[pltpu.VMEM((_QROWS, w), F32), pltpu.VMEM((_QROWS, 1), F32),
                        pltpu.VMEM((_QROWS, 1), F32), pltpu.VMEM((_QROWS, DA_DV), F32)],
    )
    kern = functools.partial(_attn_sample_kernel, lq=lq, lam_init=lam_init)
    return pl.pallas_call(
        kern,
        grid_spec=grid_spec,
        out_shape=jax.ShapeDtypeStruct((nb * lq, wv), F32),
        compiler_params=_cparams(("arbitrary", "arbitrary")),
        name="attn_sample",
    )(page_table.reshape(-1), q_rot, *([ckt] * npg), *([cv2] * npg), k_new, v_new, *lam_p, dn)


def _merge_kernel(o0_ref, o1_ref, o2_ref, o3_ref, g_ref, wb_ref, wo_ref, x_ref, g1_ref, lng_ref, lnb_ref,
                  out_ref, acc_scr):
    b = pl.program_id(1)

    @pl.when(b == 0)
    def _():
        acc_scr[...] = jnp.zeros(acc_scr.shape, F32)

    o = jnp.where(b == 0, o0_ref[...], jnp.where(b == 1, o1_ref[...], jnp.where(b == 2, o2_ref[...], o3_ref[...])))
    acc_scr[...] += _sigmoid(g_ref[...].astype(F32)) * _dot(o.astype(BF16), wb_ref[...])

    @pl.when(b == N_BRANCH - 1)
    def _():
        mix = _dot(acc_scr[...].astype(BF16), wo_ref[...])
        y = DEEPNORM_ALPHA * x_ref[...] + g1_ref[...] * mix
        out_ref[...] = _layer_norm(y, lng_ref[...], lnb_ref[...])


def _merge(branches, gates, w_branch, w_out, layer, x2d, g1, gdiv, ln_g, ln_b, tm):
    m = x2d.shape[0]
    d = D_MODEL
    r = g1.shape[1]
    const2 = lambda i, b: (0, 0)
    osp = pl.BlockSpec((tm, BRANCH_WIDTH), lambda i, b: (i, 0))
    return pl.pallas_call(
        _merge_kernel,
        grid=(m // tm, N_BRANCH),
        in_specs=[osp, osp, osp, osp,
                  pl.BlockSpec((tm, d), lambda i, b: (i, b)),
                  pl.BlockSpec((None, None, BRANCH_WIDTH, d), lambda i, b: (layer, b, 0, 0)),
                  pl.BlockSpec((None, d, d), lambda i, b: (layer, 0, 0), pipeline_mode=pl.Buffered(1)),
                  pl.BlockSpec((tm, d), lambda i, b: (i, 0)),
                  pl.BlockSpec((None, r, d), lambda i, b: (i // gdiv, 0, 0)),
                  pl.BlockSpec((1, d), const2), pl.BlockSpec((1, d), const2)],
        out_specs=pl.BlockSpec((tm, d), lambda i, b: (i, 0)),
        out_shape=jax.ShapeDtypeStruct((m, d), F32),
        scratch_shapes=[pltpu.VMEM((tm, d), F32)],
        compiler_params=_cparams(("arbitrary", "arbitrary")),
        name="merge",
    )(*branches, gates, w_branch, w_out, x2d, g1, ln_g, ln_b)


_FHALO = 2 * SUBLANES


def _ffn_kernel(x_ref, xprev_ref, sc_ref, sh_ref, g2_ref, csta_ref, cstv_ref, wa_ref, wv_ref, cw_ref, cb_ref,
                wd_ref, lng_ref, lnb_ref, out_ref, sta_ref, stv_ref, u_scr, hsa, hsv, act_scr,
                *, nseq, sl, first_div, rc):
    tm = nseq * sl
    halo = _FHALO if nseq == 1 else 0
    seg = _FHALO + sl
    i, j = pl.program_id(0), pl.program_id(1)
    nf = pl.num_programs(1)

    @pl.when(j == 0)
    def _():
        sc, sh = 1.0 + sc_ref[...], sh_ref[...]
        if halo:
            u_scr[0:halo, :] = (xprev_ref[...] * sc + sh).astype(BF16)
        u_scr[halo:halo + tm, :] = (x_ref[...] * sc + sh).astype(BF16)
        out_ref[...] = jnp.zeros(out_ref.shape, F32)

    parts = ((wa_ref, csta_ref, hsa, sta_ref), (wv_ref, cstv_ref, hsv, stv_ref))
    if nseq == 1:
        half = tm // 2
        spans = ((0, half), (half, tm))
        first = i % first_div == 0
        for lo, hi in ((0, _FHALO + half), (_FHALO + half, _FHALO + tm)):
            for w_ref, cst_ref, hs, _ in parts:
                hs[lo:hi, :] = _dot(u_scr[lo:hi, :], w_ref[...])
                if lo == 0:
                    hs[_FHALO - _HALO:_FHALO, :] = jnp.where(first, cst_ref[0], hs[_FHALO - _HALO:_FHALO, :])
    else:
        spans = ((0, tm),)
        u = u_scr[...]
        for w_ref, cst_ref, hs, _ in parts:
            hup = _dot(u, w_ref[...])
            for s in range(nseq):
                hs[s * seg + _FHALO - _HALO:s * seg + _FHALO, :] = cst_ref[s]
                hs[s * seg + _FHALO:(s + 1) * seg, :] = hup[s * sl:(s + 1) * sl]
    for _, _, hs, st_ref in parts:
        for s in range(nseq):
            st_ref[s] = hs[(s + 1) * seg - _HALO:(s + 1) * seg, :]

    k = FFN_CONV
    base = _FHALO - (k - 1)
    for lo, hi in spans:
        for r0 in range(lo, hi, rc):
            s, t0 = divmod(r0, sl)
            o = s * seg + base + t0
            ya, yv = cb_ref[0], cb_ref[1]
            for t in range(k):
                ya = ya + cw_ref[0, t:t + 1, :] * hsa[o + t:o + t + rc, :]
                yv = yv + cw_ref[1, t:t + 1, :] * hsv[o + t:o + t + rc, :]
            act_scr[r0:r0 + rc, :] = (_gelu(ya) * yv).astype(act_scr.dtype)
        out_ref[lo:hi, :] += _dot(act_scr[lo:hi, :].astype(BF16), wd_ref[...])

    @pl.when(j == nf - 1)
    def _():
        y = DEEPNORM_ALPHA * x_ref[...] + g2_ref[...] * out_ref[...]
        out_ref[...] = _layer_norm(y, lng_ref[...], lnb_ref[...])


_FFN_TF = 512


def _ffn(x2d, sc, sh, g2, gdiv, cst, w_up, conv_w, conv_b, w_down, layer, ln_g, ln_b, nseq, sl, first_div, tf):
    m, d = x2d.shape
    tm = nseq * sl
    nblk = m // tm
    nf = D_FF // tf
    r = sc.shape[1]
    halo = _FHALO if nseq == 1 else 0
    const2 = lambda i, j: (0, 0)
    mod_spec = pl.BlockSpec((None, r, d), lambda i, j: (i // gdiv, 0, 0))
    rb = tm // _FHALO
    cw2 = conv_w.reshape(FFN_CONV, 2, D_FF).transpose(1, 0, 2)
    cb2 = conv_b.reshape(2, 1, D_FF)
    csta, cstv = cst[..., :D_FF], cst[..., D_FF:]
    cst_spec = pl.BlockSpec((None, nseq, _HALO, tf), lambda i, j: (i // first_div, 0, 0, j))
    st_spec = pl.BlockSpec((None, nseq, _HALO, tf), lambda i, j: (i, 0, 0, j))
    st_shape = jax.ShapeDtypeStruct((nblk, nseq, _HALO, D_FF), F32)
    rc = 32 if sl % 32 == 0 else sl
    act_dtype = BF16 if rc % (2 * SUBLANES) == 0 else F32
    kern = functools.partial(_ffn_kernel, nseq=nseq, sl=sl, first_div=first_div, rc=rc)
    return pl.pallas_call(
        kern,
        grid=(nblk, nf),
        in_specs=[pl.BlockSpec((tm, d), lambda i, j: (i, 0), pipeline_mode=pl.Buffered(1)),
                  pl.BlockSpec((_FHALO, d), lambda i, j: (jnp.maximum(i * rb - 1, 0), 0)),
                  mod_spec, mod_spec, mod_spec, cst_spec, cst_spec,
                  pl.BlockSpec((None, d, tf), lambda i, j: (layer, 0, j)),
                  pl.BlockSpec((None, d, tf), lambda i, j: (layer, 0, nf + j)),
                  pl.BlockSpec((2, FFN_CONV, tf), lambda i, j: (0, 0, j)),
                  pl.BlockSpec((2, 1, tf), lambda i, j: (0, 0, j)),
                  pl.BlockSpec((None, tf, d), lambda i, j: (layer, j, 0)),
                  pl.BlockSpec((1, d), const2), pl.BlockSpec((1, d), const2)],
        out_specs=[pl.BlockSpec((tm, d), lambda i, j: (i, 0), pipeline_mode=pl.Buffered(1)), st_spec, st_spec],
        out_shape=[jax.ShapeDtypeStruct((m, d), F32), st_shape, st_shape],
        scratch_shapes=[pltpu.VMEM((halo + tm, d), BF16),
                        pltpu.VMEM((nseq * (_FHALO + sl), tf), F32), pltpu.VMEM((nseq * (_FHALO + sl), tf), F32),
                        pltpu.VMEM((tm, tf), act_dtype)],
        compiler_params=_cparams(("arbitrary", "arbitrary")),
        name="ffn",
    )(x2d, x2d, sc, sh, g2, csta, cstv, w_up, w_up, cw2, cb2, w_down, ln_g, ln_b)


def _pad_state_rows(buf, rows):
    pad = [(0, 0)] * buf.ndim
    pad[-2] = (rows - buf.shape[-2], 0)
    return jnp.pad(buf, pad)


def _run_layer(x2d, mod, cfg, wts, l, states, paged):
    nb, length = cfg["nb"], cfg["len"]
    m = nb * length
    sh1, sc1, g1, sh2, sc2, g2 = mod
    gdiv = cfg["gdiv"]
    st_gla, st_s5re, st_s5im, st_lru, st_lconv, st_fconv = states

    gates, z, zlr = _inproj(x2d, sc1, sh1, gdiv(cfg["tm_in"]), wts["w_in_t"], l,
                            cfg["tm_in"], cfg["tng"], cfg["tnm"])

    ng_seq, nseq = cfg["ng"], cfg["nseq"]
    o_gla, gla_new = _gla(z, zlr, _gla_state_pad(st_gla), wts["wa2p"], wts["gla_ba"], wts["gla_norm"],
                          ng_seq, m // ng_seq, cfg["gla_c"], cfg["gla_seq"])
    gla_new = _gla_state_unpad(gla_new)

    sl = cfg["sl"]
    o_s5, s5re_new, s5im_new = _s5(z.reshape(nb, length, Z_MIX), st_s5re.reshape(nb, _S5_CH, LANES),
                                   st_s5im.reshape(nb, _S5_CH, LANES), wts["s5"], cfg["s5_sl"])
    o_s5 = o_s5.reshape(m, BRANCH_WIDTH)
    s5re_new = s5re_new.reshape(nb, S5_GROUPS, S5_N)
    s5im_new = s5im_new.reshape(nb, S5_GROUPS, S5_N)

    o_lru, lru_new = _lru(z.reshape(nb, length, Z_MIX), _pad_state_rows(st_lconv, _HALO), st_lru, wts["lru"],
                          cfg["s5_sl"])
    o_lru = o_lru.reshape(m, BRANCH_WIDTH)
    lconv_new = z[:, COL_LX:COL_LX + BRANCH_WIDTH].reshape(nb, length, BRANCH_WIDTH)[:, length - (LRU_CONV - 1):]

    lam_init = 0.8 - 0.6 * math.exp(-0.3 * l)
    q_rot, k_rot = _rope(z, cfg["rope"], nb, length, cfg["tm_rope"])
    v_new = z[:, COL_DV:COL_DV + DA_HEADS * DA_DV]
    if paged is None:
        o_da = _attn_prompt(q_rot, k_rot, z, wts["lam_p"], wts["da_norm"], nb, length,
                            cfg["tq"], cfg["tk"], lam_init)
    else:
        page_table, cache_k, cache_v = paged
        padk = ((0, 0), (0, PAGE_SIZE - length), (0, 0))
        k_pad = jnp.pad(k_rot.reshape(nb, length, -1), padk)
        v_pad = jnp.pad(v_new.reshape(nb, length, -1), padk)
        o_da = _attn_sample(q_rot, cache_k, cache_v, l, page_table, k_pad, v_pad, wts["lam_p"],
                            wts["da_norm"], lam_init)

    x1 = _merge((o_gla, o_s5, o_lru, o_da), gates, wts["w_branch"], wts["w_out"], l, x2d, g1, gdiv(cfg["tm_mg"]),
                wts["ln1_g"], wts["ln1_b"], cfg["tm_mg"])

    fcst = _pad_state_rows(st_fconv, _HALO).reshape(ng_seq, nseq, _HALO, 2 * D_FF)
    fsl = cfg["ffn_sl"]
    x2, sta, stv = _ffn(x1, sc2, sh2, g2, gdiv(nseq * fsl), fcst, wts["w_up"], wts["ffn_conv_w"],
                        wts["ffn_conv_b"], wts["w_down"], l, wts["ln2_g"], wts["ln2_b"], nseq, fsl,
                        length // fsl if nseq == 1 else 1, _FFN_TF)
    fst = jnp.concatenate([sta, stv], axis=-1)
    if nseq == 1:
        per_seq = length // fsl
        fst = fst.reshape(nb, per_seq, _HALO, 2 * D_FF)[:, per_seq - 1]
    else:
        fst = fst.reshape(nb, _HALO, 2 * D_FF)
    fconv_new = fst[:, _HALO - (FFN_CONV - 1):]

    new = (k_rot.reshape(nb, length, DA_HEADS, 2, DA_DK), v_new.reshape(nb, length, DA_HEADS, DA_DV),
           gla_new, s5re_new, s5im_new, lru_new, lconv_new, fconv_new)
    return x2, new


def _layer_weights(l, w_in, gla_wa2, gla_ba, gla_norm, s5_raw, lru_raw, lam_raw, da_norm, w_branch, w_out,
                   ln1_g, ln1_b, ffn_w_up, ffn_conv_w, ffn_conv_b, ffn_w_down, ln2_g, ln2_b):
    w_in_t = jnp.swapaxes(w_in, 1, 2).astype(BF16)
    wa2p = jnp.pad(gla_wa2[l], ((0, LANES - GLA_LOWRANK), (0, 0))).astype(BF16)
    row = lambda a: a[l].reshape(1, -1)
    cw, cb, wa, ba, wx, bx, lam = [a[l] for a in lru_raw]
    return dict(
        w_in_t=w_in_t, wa2p=wa2p, gla_ba=row(gla_ba), gla_norm=row(gla_norm),
        s5=_s5_params(*[a[l] for a in s5_raw]),
        lru=dict(cw=cw, cb=cb.reshape(1, -1), wa=wa.astype(BF16), ba=ba.reshape(1, -1),
                 wx=wx.astype(BF16), bx=bx.reshape(1, -1), lam=lam.reshape(1, -1)),
        lam_p=tuple(row(a) for a in lam_raw), da_norm=row(da_norm),
        w_branch=w_branch.astype(BF16), w_out=w_out.astype(BF16),
        ln1_g=row(ln1_g), ln1_b=row(ln1_b),
        w_up=ffn_w_up.astype(BF16), ffn_conv_w=ffn_conv_w[l], ffn_conv_b=ffn_conv_b[l],
        w_down=ffn_w_down.astype(BF16), ln2_g=row(ln2_g), ln2_b=row(ln2_b))


def kernel(x_prompt, x_sample, cache_k, cache_v, state_gla, state_s5_re, state_s5_im, state_lru, state_lru_conv, state_ffn_conv, page_table, c_prompt, c_sample, w_ada, b_ada, w_in, gla_wa2, gla_ba, gla_norm, s5_a_re, s5_a_im, s5_log_dt, s5_b_re, s5_b_im, s5_c_re, s5_c_im, s5_d, s5_w_glu, s5_b_glu, lru_conv_w, lru_conv_b, lru_w_a, lru_b_a, lru_w_x, lru_b_x, lru_lambda, da_lq1, da_lk1, da_lq2, da_lk2, da_norm, w_branch, w_out, ln1_g, ln1_b, ffn_w_up, ffn_conv_w, ffn_conv_b, ffn_w_down, ln2_g, ln2_b):
    bp, lp, d = x_prompt.shape
    bs, ls, _ = x_sample.shape
    past_len = page_table.shape[1] * PAGE_SIZE

    rows = bp + bs
    rows_pad = -(-rows // SUBLANES) * SUBLANES
    c_all = jnp.pad(jnp.concatenate([c_prompt, c_sample], axis=0), ((0, rows_pad - rows), (0, 0)))
    mod_all = _ada(c_all, w_ada, b_ada)

    def mods(l, lo, n, per_row):
        parts = jnp.split(mod_all[l, lo:lo + n], 6, axis=-1)
        if per_row:
            return [jnp.repeat(p, per_row, axis=0)[None] for p in parts]
        return [p[:, None, :] for p in parts]

    cfg_p = dict(nb=bp, len=lp, ng=bp, nseq=1, sl=256, s5_sl=128, ffn_sl=1024, tm_in=1024, tng=1024, tnm=512, gla_c=128,
                 gla_seq=128, tm_rope=512, tq=1024, tk=512, tm_mg=512, tf=512,
                 gdiv=lambda tm: lp // tm, rope=_rope_tables(0, lp))
    cfg_s = dict(nb=bs, len=ls, ng=1, nseq=bs, sl=ls, s5_sl=ls, ffn_sl=ls, tm_in=bs * ls, tng=1024, tnm=512, gla_c=bs * ls,
                 gla_seq=ls, tm_rope=ls, tm_mg=bs * ls, tf=512,
                 gdiv=lambda tm: 1, rope=_rope_tables(past_len, ls))

    zeros_p = (jnp.zeros((bp, GLA_HEADS, GLA_DK, GLA_DV), F32), jnp.zeros((bp, S5_GROUPS, S5_N), F32),
               jnp.zeros((bp, S5_GROUPS, S5_N), F32), jnp.zeros((bp, BRANCH_WIDTH), F32),
               jnp.zeros((bp, LRU_CONV - 1, BRANCH_WIDTH), F32), jnp.zeros((bp, FFN_CONV - 1, 2 * D_FF), F32))

    xp = x_prompt.reshape(bp * lp, d)
    xs = x_sample.reshape(bs * ls, d)
    col_p = [[] for _ in range(8)]
    col_s = [[] for _ in range(8)]
    s5_raw = (s5_a_re, s5_a_im, s5_log_dt, s5_b_re, s5_b_im, s5_c_re, s5_c_im, s5_d, s5_w_glu, s5_b_glu)
    lru_raw = (lru_conv_w, lru_conv_b, lru_w_a, lru_b_a, lru_w_x, lru_b_x, lru_lambda)
    lam_raw = (da_lq1, da_lk1, da_lq2, da_lk2)
    for l in range(DEPTH):
        wts = _layer_weights(l, w_in, gla_wa2, gla_ba, gla_norm, s5_raw, lru_raw, lam_raw, da_norm, w_branch,
                             w_out, ln1_g, ln1_b, ffn_w_up, ffn_conv_w, ffn_conv_b, ffn_w_down, ln2_g, ln2_b)
        xp, new_p = _run_layer(xp, mods(l, 0, bp, 0), cfg_p, wts, l, zeros_p, None)
        st_s = (state_gla[l], state_s5_re[l], state_s5_im[l], state_lru[l], state_lru_conv[l], state_ffn_conv[l])
        xs, new_s = _run_layer(xs, mods(l, bp, bs, ls), cfg_s, wts, l, st_s, (page_table, cache_k, cache_v))
        for lst, s in zip(col_p, new_p):
            lst.append(s)
        for lst, s in zip(col_s, new_s):
            lst.append(s)

    sp = [jnp.stack(s) for s in col_p]
    ss = [jnp.stack(s) for s in col_s]
    out = [xp.reshape(bp, lp, d), xs.reshape(bs, ls, d)]
    for a, b in zip(sp, ss):
        out += [a, b]
    return tuple(out)
```

```python
import functools
import math

import numpy as np
import jax
import jax.numpy as jnp
from jax import lax
from jax.experimental import pallas as pl
from jax.experimental.pallas import tpu as pltpu

F32 = jnp.float32
BF16 = jnp.bfloat16

D_MODEL = 2048
DEPTH = 2
PAGE_SIZE = 128
N_BRANCH = 4
BRANCH_WIDTH = 512
GLA_HEADS = 4
GLA_DK = 64
GLA_DV = 128
GLA_LOWRANK = 16
GLA_TAU = 16.0
S5_GROUP = 16
S5_GROUPS = 32
S5_N = 64
S5_STATES = S5_GROUPS * S5_N
LRU_HEADS = 4
LRU_BLOCK = 128
LRU_CONV = 4
LRU_C = 8.0
DA_HEADS = 4
DA_DK = 64
DA_DV = 128
ROPE_DIM = 16
ROPE_THETA = 500000.0
D_FF = 5632
FFN_CONV = 3
DEEPNORM_ALPHA = (2.0 * DEPTH) ** 0.25
EPS = 1e-5

LANES = 128
SUBLANES = 8
VMEM_LIMIT = 58 * 1024 * 1024

Z_GATE = N_BRANCH * D_MODEL
Z_MIX = 4608
COL_GQ, COL_GK, COL_GV, COL_OG = 0, 256, 512, 1024
COL_SU, COL_LX, COL_LG = 1536, 2048, 2560
COL_DQ, COL_DK, COL_DV = 3072, 3584, 4096

_NT = (((1,), (1,)), ((), ()))
_TN = (((0,), (0,)), ((), ()))


def _cparams(sem):
    return pltpu.CompilerParams(dimension_semantics=sem, vmem_limit_bytes=VMEM_LIMIT)


def _dot(a, b):
    return jnp.dot(a, b, preferred_element_type=F32)


def _sigmoid(x):
    return 0.5 * jnp.tanh(0.5 * x) + 0.5


def _gelu(x):
    return 0.5 * x * (1.0 + jnp.tanh(math.sqrt(2.0 / math.pi) * (x + 0.044715 * (x * x * x))))


def _layer_norm(y, g, b):
    mu = jnp.mean(y, axis=-1, keepdims=True)
    d = y - mu
    var = jnp.mean(d * d, axis=-1, keepdims=True)
    return d * lax.rsqrt(var + EPS) * g + b


def _ada_kernel(c_ref, w_ref, b_ref, o_ref):
    c = c_ref[...]
    s = c * _sigmoid(c)
    o_ref[...] = _dot(s.astype(BF16), w_ref[...].astype(BF16)) + b_ref[...]


def _ada(c_all, w_ada, b_ada):
    rows = c_all.shape[0]
    n = w_ada.shape[-1]
    tn = 1536
    return pl.pallas_call(
        _ada_kernel,
        grid=(DEPTH, n // tn),
        in_specs=[pl.BlockSpec((rows, D_MODEL), lambda l, j: (0, 0)),
                  pl.BlockSpec((None, D_MODEL, tn), lambda l, j: (l, 0, j)),
                  pl.BlockSpec((None, 1, tn), lambda l, j: (l, 0, j))],
        out_specs=pl.BlockSpec((None, rows, tn), lambda l, j: (l, 0, j)),
        out_shape=jax.ShapeDtypeStruct((DEPTH, rows, n), F32),
        compiler_params=_cparams(("arbitrary", "arbitrary")),
        name="ada",
    )(c_all, w_ada, b_ada.reshape(DEPTH, 1, n))


_W_SHIFT = LANES - GLA_LOWRANK


def _inproj_kernel(x_ref, sc_ref, sh_ref, wg_ref, wq_ref, wr_ref, wlr_ref, zg_ref, zm_ref, zlr_ref, u_scr,
                   *, ngate, nq):
    j = pl.program_id(1)

    @pl.when(j == 0)
    def _():
        u = (x_ref[...] * (1.0 + sc_ref[...]) + sh_ref[...]).astype(BF16)
        u_scr[...] = u
        zlr_ref[...] = _dot(u, wlr_ref[0])

    @pl.when(j < ngate)
    def _():
        zg_ref[...] = _dot(u_scr[...], wg_ref[0]).astype(BF16)

    @pl.when((j >= ngate) & (j < ngate + nq))
    def _():
        zm_ref[...] = _dot(u_scr[...], wq_ref[...])

    @pl.when(j >= ngate + nq)
    def _():
        zm_ref[...] = _dot(u_scr[...], wr_ref[0])


def _inproj(x2d, sc, sh, gdiv, w_kn, w_qkv, layer, tm, tng, tnm):
    m = x2d.shape[0]
    r = sc.shape[1]
    n_qkv = w_qkv.shape[2]
    ngate, nq, nr = Z_GATE // tng, n_qkv // tnm, (Z_MIX - n_qkv) // tnm
    rest_col = n_qkv + GLA_LOWRANK + _W_SHIFT
    gate_col = w_kn.shape[2] - Z_GATE
    assert rest_col % LANES == 0 and gate_col % LANES == 0 and n_qkv % tnm == 0
    aligned = lambda c: pl.multiple_of(c, LANES)

    def el_spec(cols, index_map):
        return pl.BlockSpec((pl.Element(1), pl.Element(D_MODEL), pl.Element(cols)), index_map)

    mod_spec = pl.BlockSpec((None, r, D_MODEL), lambda i, j: (i // gdiv, 0, 0))
    kern = functools.partial(_inproj_kernel, ngate=ngate, nq=nq)
    return pl.pallas_call(
        kern,
        grid=(m // tm, ngate + nq + nr),
        in_specs=[pl.BlockSpec((tm, D_MODEL), lambda i, j: (i, 0)), mod_spec, mod_spec,
                  el_spec(tng, lambda i, j: (layer, 0, aligned(gate_col + jnp.minimum(j, ngate - 1) * tng))),
                  pl.BlockSpec((None, D_MODEL, tnm), lambda i, j: (layer, 0, jnp.clip(j - ngate, 0, nq - 1))),
                  el_spec(tnm, lambda i, j: (layer, 0,
                                             aligned(rest_col + jnp.clip(j - ngate - nq, 0, nr - 1) * tnm))),
                  el_spec(LANES, lambda i, j: (layer, 0, rest_col - LANES))],
        out_specs=[pl.BlockSpec((tm, tng), lambda i, j: (i, jnp.minimum(j, ngate - 1))),
                   pl.BlockSpec((tm, tnm), lambda i, j: (i, jnp.maximum(j - ngate, 0))),
                   pl.BlockSpec((tm, LANES), lambda i, j: (i, 0))],
        out_shape=[jax.ShapeDtypeStruct((m, Z_GATE), BF16), jax.ShapeDtypeStruct((m, Z_MIX), F32),
                   jax.ShapeDtypeStruct((m, LANES), F32)],
        scratch_shapes=[pltpu.VMEM((tm, D_MODEL), BF16)],
        compiler_params=_cparams(("arbitrary", "arbitrary")),
        name="inproj",
    )(x2d, sc, sh, w_kn, w_qkv, w_kn, w_kn)


def _rope_tables(pos0, length):
    half = ROPE_DIM // 2
    inv = ROPE_THETA ** (-jnp.arange(half, dtype=F32) * 2.0 / ROPE_DIM)
    ang = (pos0 + jnp.arange(length)).astype(F32)[:, None] * inv
    cos, sin = jnp.cos(ang), jnp.sin(ang)
    ones = jnp.ones((length, DA_DK - ROPE_DIM), F32)
    zeros = jnp.zeros((length, DA_DK - ROPE_DIM), F32)
    zh = jnp.zeros((length, half), F32)
    c = jnp.concatenate([cos, cos, ones], axis=1)
    s_next = jnp.concatenate([-sin, zh, zeros], axis=1)
    s_prev = jnp.concatenate([zh, sin, zeros], axis=1)
    rep = LANES // DA_DK
    return jnp.tile(c, (1, rep)), jnp.tile(s_next, (1, rep)), jnp.tile(s_prev, (1, rep))


def _rope_kernel(q_ref, k_ref, c_ref, sn_ref, sp_ref, qo_ref, ko_ref):
    half = ROPE_DIM // 2
    c, sn, sp = c_ref[...], sn_ref[...], sp_ref[...]
    for src, dst in ((q_ref, qo_ref), (k_ref, ko_ref)):
        for i in range(src.shape[1] // LANES):
            x = src[:, i * LANES:(i + 1) * LANES]
            nxt = pltpu.roll(x, LANES - half, axis=1)
            prv = pltpu.roll(x, half, axis=1)
            dst[:, i * LANES:(i + 1) * LANES] = x * c + nxt * sn + prv * sp


def _rope(z, tabs, nb, length, tm):
    m = z.shape[0]
    w = DA_HEADS * 2 * DA_DK
    nj = length // tm
    tab_spec = pl.BlockSpec((tm, LANES), lambda b, j: (j, 0))
    return pl.pallas_call(
        _rope_kernel,
        grid=(nb, nj),
        in_specs=[pl.BlockSpec((tm, w), lambda b, j: (b * nj + j, COL_DQ // w)),
                  pl.BlockSpec((tm, w), lambda b, j: (b * nj + j, COL_DK // w)),
                  tab_spec, tab_spec, tab_spec],
        out_specs=[pl.BlockSpec((tm, w), lambda b, j: (b * nj + j, 0))] * 2,
        out_shape=[jax.ShapeDtypeStruct((m, w), F32)] * 2,
        compiler_params=_cparams(("arbitrary", "arbitrary")),
        name="rope",
    )(z, z, *tabs)


def _gla_consts(c, seq):
    nlev = int(math.log2(seq))
    t = np.arange(c)
    tt, rr = t[:, None], t[None, :]
    same = (tt // seq) == (rr // seq)
    tril = same & (rr <= tt)
    lvl = np.full((c, c), -1, np.int32)
    lvl[t, t] = nlev
    for i in range(nlev):
        b = seq >> (i + 1)
        valid = ((tt // (2 * b)) == (rr // (2 * b))) & ((tt % (2 * b)) >= b) & ((rr % (2 * b)) < b)
        lvl[valid] = i
    return tril.astype(np.float32), lvl, nlev


def _gla_kernel(q_ref, k_ref, v_ref, og_ref, lr_ref, s0_ref, lmat_ref, lvl_ref, wa2_ref, ba_ref, gn_ref,
                o_ref, st_ref, cum_scr, *, c, seq, nlev):
    nb = c // seq
    j = pl.program_id(1)

    @pl.when(j == 0)
    def _():
        st_ref[...] = s0_ref[...]

    x = _dot(lr_ref[...].astype(BF16), wa2_ref[...]) + ba_ref[...]
    la = (jnp.minimum(x, 0.0) - jnp.log1p(jnp.exp(-jnp.abs(x)))) * (1.0 / GLA_TAU)
    cum = jnp.dot(lmat_ref[...], la, precision=lax.Precision.HIGHEST, preferred_element_type=F32)
    cum_scr[...] = cum
    wq = cum.shape[1]

    def cum_at(idx, group):
        sub = lax.broadcasted_iota(jnp.int32, (SUBLANES, wq), 0)
        bcast = lambda r: jnp.broadcast_to(cum_scr[r:r + 1, :], (SUBLANES, wq))
        tiles = []
        for t0 in range(0, c, SUBLANES):
            tile = bcast(idx(t0))
            for g in range(group, SUBLANES, group):
                tile = jnp.where(sub >= g, bcast(idx(t0 + g)), tile)
            tiles.append(tile)
        return jnp.concatenate(tiles, axis=0)

    e_end = cum_at(lambda t: (t // seq) * seq + seq - 1, seq) - cum
    e_lvl = [cum - cum_at(lambda t, b=seq >> (i + 1): (t // (2 * b)) * (2 * b) + b - 1, 2 * (seq >> (i + 1)))
             for i in range(nlev)]
    q = q_ref[...] * (GLA_DK ** -0.5)
    k = k_ref[...]
    lvl = lvl_ref[...]
    lane = lax.broadcasted_iota(jnp.int32, (c, LANES), 1)
    if nb > 1:
        rowb = lax.broadcasted_iota(jnp.int32, (c, nb * LANES), 0) // seq
        colb = lax.broadcasted_iota(jnp.int32, (c, nb * LANES), 1) // LANES
        bmask = rowb == colb

    def spread(a):
        if nb == 1:
            return a
        return jnp.where(bmask, jnp.concatenate([a] * nb, axis=1), 0.0)

    for p in range(2):
        sl = slice(p * LANES, (p + 1) * LANES)
        qp, kp, cump = q[:, sl], k[:, sl], cum[:, sl]
        qts, kts = [], []
        for i in range(nlev):
            eb = e_lvl[i][:, sl]
            qts.append(qp * jnp.exp(eb))
            kts.append((kp * jnp.exp(-eb)).astype(BF16))
        qg = qp * jnp.exp(cump)
        kh = kp * jnp.exp(e_end[:, sl])
        kpb = kp.astype(BF16)
        decays = []
        for b in range(nb):
            cl = cump[b * seq + seq - 1:b * seq + seq, :]
            decays.append(jnp.broadcast_to(jnp.exp(cl), (LANES, LANES)).T)
        for hh in range(2):
            h = 2 * p + hh
            hm = (lane // GLA_DK) == hh
            att = lax.dot_general(jnp.where(hm, qp, 0.0).astype(BF16), kpb, _NT, preferred_element_type=F32)
            att = jnp.where(lvl == nlev, att, 0.0)
            for i in range(nlev):
                a = lax.dot_general(jnp.where(hm, qts[i], 0.0).astype(BF16), kts[i], _NT,
                                    preferred_element_type=F32)
                att = jnp.where(lvl == i, a, att)
            vh = v_ref[:, h * GLA_DV:(h + 1) * GLA_DV].astype(BF16)
            o = _dot(att.astype(BF16), vh)
            s_all = st_ref[:, h].reshape(nb * LANES, GLA_DV)
            o = o + _dot(spread(jnp.where(hm, qg, 0.0)).astype(BF16), s_all.astype(BF16))
            upd = lax.dot_general(spread(jnp.where(hm, kh, 0.0)).astype(BF16), vh, _TN,
                                  preferred_element_type=F32)
            for b in range(nb):
                st_ref[b, h] = st_ref[b, h] * decays[b] + upd[b * LANES:(b + 1) * LANES]
            ms = jnp.mean(o * o, axis=-1, keepdims=True)
            og = og_ref[:, h * GLA_DV:(h + 1) * GLA_DV]
            o_ref[:, h * GLA_DV:(h + 1) * GLA_DV] = o * lax.rsqrt(ms + EPS) * gn_ref[...] * (og * _sigmoid(og))


def _gla(z, zlr, s0_pad, wa2p, ba, gn, ng, length, c, seq):
    m = z.shape[0]
    nb = c // seq
    nj = length // c
    lmat, lvl, nlev = _gla_consts(c, seq)
    w = GLA_HEADS * GLA_DK
    wv = GLA_HEADS * GLA_DV
    row = lambda g, j: g * nj + j
    const2 = lambda g, j: (0, 0)
    st_spec = pl.BlockSpec((nb, GLA_HEADS, LANES, GLA_DV), lambda g, j: (g, 0, 0, 0))
    kern = functools.partial(_gla_kernel, c=c, seq=seq, nlev=nlev)
    return pl.pallas_call(
        kern,
        grid=(ng, nj),
        in_specs=[pl.BlockSpec((c, w), lambda g, j: (row(g, j), COL_GQ // w)),
                  pl.BlockSpec((c, w), lambda g, j: (row(g, j), COL_GK // w)),
                  pl.BlockSpec((c, wv), lambda g, j: (row(g, j), COL_GV // wv)),
                  pl.BlockSpec((c, wv), lambda g, j: (row(g, j), COL_OG // wv)),
                  pl.BlockSpec((c, LANES), lambda g, j: (row(g, j), 0)),
                  st_spec,
                  pl.BlockSpec(lmat.shape, const2),
                  pl.BlockSpec(lvl.shape, const2),
                  pl.BlockSpec((LANES, w), const2),
                  pl.BlockSpec((1, w), const2),
                  pl.BlockSpec((1, GLA_DV), const2)],
        out_specs=[pl.BlockSpec((c, wv), lambda g, j: (row(g, j), 0)), st_spec],
        out_shape=[jax.ShapeDtypeStruct((m, wv), F32),
                   jax.ShapeDtypeStruct(s0_pad.shape, F32)],
        scratch_shapes=[pltpu.VMEM((c, w), F32)],
        compiler_params=_cparams(("arbitrary", "arbitrary")),
        name="gla",
    )(z, z, z, z, zlr, s0_pad, jnp.asarray(lmat), jnp.asarray(lvl), wa2p, ba, gn)


def _gla_state_pad(s):
    b = s.shape[0]
    s6 = s.reshape(b, 2, 2, 1, GLA_DK, GLA_DV) * jnp.eye(2, dtype=F32)[None, None, :, :, None, None]
    return s6.reshape(b, GLA_HEADS, LANES, GLA_DV)


def _gla_state_unpad(sp):
    b = sp.shape[0]
    s6 = sp.reshape(b, 2, 2, 2, GLA_DK, GLA_DV)
    return jnp.stack([s6[:, :, 0, 0], s6[:, :, 1, 1]], axis=2).reshape(b, GLA_HEADS, GLA_DK, GLA_DV)


_S5_CH = S5_STATES // LANES


def _s5_kernel(u_ref, h0r_ref, h0i_ref, are_ref, aim_ref, ldt_ref, bdr_ref, bdi_ref, cdr_ref, cdi_ref,
               d_ref, wg_ref, bg_ref, y_ref, hfr_ref, hfi_ref, sre, sim, hsc, hcr, hci, *, nseq, sl):
    tm = nseq * sl
    w = BRANCH_WIDTH
    j = pl.program_id(0)

    @pl.when(j == 0)
    def _():
        hcr[...] = h0r_ref[...]
        hci[...] = h0i_ref[...]

    lr, li = are_ref[...], aim_ref[...]
    dt = jnp.exp(ldt_ref[...])
    mag = jnp.exp(lr * dt)
    abr, abi = mag * jnp.cos(li * dt), mag * jnp.sin(li * dt)
    den = lr * lr + li * li
    fr = ((abr - 1.0) * lr + abi * li) / den
    fi = (abi * lr - (abr - 1.0) * li) / den

    u = u_ref[...].reshape(tm, w)
    ub = u.astype(BF16)
    for c2 in range(_S5_CH // 2):
        ws = slice((c2 // 2) * LANES, (c2 // 2 + 1) * LANES)
        tr = _dot(ub[:, ws], bdr_ref[ws, c2 * 256:(c2 + 1) * 256])
        ti = _dot(ub[:, ws], bdi_ref[ws, c2 * 256:(c2 + 1) * 256])
        for e in range(2):
            ch = 2 * c2 + e
            trc, tic = tr[:, e * LANES:(e + 1) * LANES], ti[:, e * LANES:(e + 1) * LANES]
            frc, fic = fr[ch:ch + 1, :], fi[ch:ch + 1, :]
            sre[ch * tm:(ch + 1) * tm, :] = frc * trc - fic * tic
            sim[ch * tm:(ch + 1) * tm, :] = frc * tic + fic * trc

    ar0, ai0, ar1, ai1 = abr[0:8], abi[0:8], abr[8:16], abi[8:16]
    grp = min(nseq, 4)
    for g0 in range(0, nseq, grp):
        def body(t, carry, g0=g0):
            out = []
            for q in range(grp):
                r0, i0, r1, i1 = carry[4 * q:4 * q + 4]
                row = (g0 + q) * sl + t
                lo = pl.ds(row, SUBLANES, stride=tm)
                hi = pl.ds(SUBLANES * tm + row, SUBLANES, stride=tm)
                n_r0 = ar0 * r0 - ai0 * i0 + sre[lo, :]
                n_i0 = ar0 * i0 + ai0 * r0 + sim[lo, :]
                n_r1 = ar1 * r1 - ai1 * i1 + sre[hi, :]
                n_i1 = ar1 * i1 + ai1 * r1 + sim[hi, :]
                dst = pl.ds(pl.multiple_of(row * SUBLANES, SUBLANES), SUBLANES)
                hsc[0, dst, :] = n_r0
                hsc[1, dst, :] = n_i0
                hsc[2, dst, :] = n_r1
                hsc[3, dst, :] = n_i1
                out += [n_r0, n_i0, n_r1, n_i1]
            return tuple(out)

        init = tuple(a for q in range(grp) for a in (hcr[g0 + q, 0:8], hci[g0 + q, 0:8],
                                                     hcr[g0 + q, 8:16], hci[g0 + q, 8:16]))
        fin = lax.fori_loop(0, sl, body, init, unroll=8)
        for q in range(grp):
            hcr[g0 + q, 0:8] = fin[4 * q]
            hci[g0 + q, 0:8] = fin[4 * q + 1]
            hcr[g0 + q, 8:16] = fin[4 * q + 2]
            hci[g0 + q, 8:16] = fin[4 * q + 3]

    ywin = []
    for wi in range(w // LANES):
        ws = slice(wi * LANES, (wi + 1) * LANES)
        yw = d_ref[:, ws] * u[:, ws]
        for c2 in (2 * wi, 2 * wi + 1):
            part = 2 * ((2 * c2) // SUBLANES)
            ca, cb = (2 * c2) % SUBLANES, (2 * c2 + 1) % SUBLANES
            rows = lambda cc: pl.ds(cc, tm, stride=SUBLANES)
            hr, hi = hsc.at[part], hsc.at[part + 1]
            hre = jnp.concatenate([hr[rows(ca), :], hr[rows(cb), :]], axis=1).astype(BF16)
            him = jnp.concatenate([hi[rows(ca), :], hi[rows(cb), :]], axis=1).astype(BF16)
            yw = yw + _dot(hre, cdr_ref[c2 * 256:(c2 + 1) * 256, ws]) - _dot(him, cdi_ref[c2 * 256:(c2 + 1) * 256, ws])
        ywin.append(yw)
    y = _gelu(jnp.concatenate(ywin, axis=1))
    y = y * _sigmoid(_dot(y.astype(BF16), wg_ref[...]) + bg_ref[...])
    y_ref[...] = y.reshape(nseq, sl, w)

    @pl.when(j == pl.num_programs(0) - 1)
    def _():
        hfr_ref[...] = hcr[...]
        hfi_ref[...] = hci[...]


def _s5(z3, h0r, h0i, p, sl):
    nseq, length, _ = z3.shape
    tm = nseq * sl
    w = BRANCH_WIDTH
    const2 = lambda j: (0, 0)
    st_spec = pl.BlockSpec((nseq, _S5_CH, LANES), lambda j: (0, 0, 0))
    vec = pl.BlockSpec((_S5_CH, LANES), const2)
    kern = functools.partial(_s5_kernel, nseq=nseq, sl=sl)
    return pl.pallas_call(
        kern,
        grid=(length // sl,),
        in_specs=[pl.BlockSpec((nseq, sl, w), lambda j: (0, j, COL_SU // w)),
                  st_spec, st_spec, vec, vec, vec,
                  pl.BlockSpec((w, S5_STATES), const2), pl.BlockSpec((w, S5_STATES), const2),
                  pl.BlockSpec((S5_STATES, w), const2), pl.BlockSpec((S5_STATES, w), const2),
                  pl.BlockSpec((1, w), const2), pl.BlockSpec((w, w), const2), pl.BlockSpec((1, w), const2)],
        out_specs=[pl.BlockSpec((nseq, sl, w), lambda j: (0, j, 0)), st_spec, st_spec],
        out_shape=[jax.ShapeDtypeStruct((nseq, length, w), F32),
                   jax.ShapeDtypeStruct(h0r.shape, F32), jax.ShapeDtypeStruct(h0i.shape, F32)],
        scratch_shapes=[pltpu.VMEM((_S5_CH * tm, LANES), F32), pltpu.VMEM((_S5_CH * tm, LANES), F32),
                        pltpu.VMEM((4, SUBLANES * tm, LANES), F32),
                        pltpu.VMEM((nseq, _S5_CH, LANES), F32), pltpu.VMEM((nseq, _S5_CH, LANES), F32)],
        compiler_params=_cparams(("arbitrary",)),
        name="s5",
    )(z3, h0r, h0i, p["are"], p["aim"], p["ldt"], p["bdr"], p["bdi"], p["cdr"], p["cdi"],
      p["d"], p["wg"], p["bg"])


def _s5_params(a_re, a_im, log_dt, b_re, b_im, c_re, c_im, d, w_glu, b_glu):
    eye = jnp.eye(S5_GROUPS, dtype=F32)

    def bd_in(b):
        return (eye[:, None, :, None] * jnp.swapaxes(b, 1, 2)[:, :, None, :]).reshape(
            S5_GROUPS * S5_GROUP, S5_STATES).astype(BF16)

    def bd_out(c):
        return (eye[:, None, :, None] * jnp.swapaxes(c, 1, 2)[:, :, None, :]).reshape(
            S5_STATES, S5_GROUPS * S5_GROUP).astype(BF16)

    return dict(are=a_re.reshape(_S5_CH, LANES), aim=a_im.reshape(_S5_CH, LANES),
                ldt=jnp.broadcast_to(log_dt[:, None], (S5_GROUPS, S5_N)).reshape(_S5_CH, LANES),
                bdr=bd_in(b_re), bdi=bd_in(b_im), cdr=bd_out(c_re), cdi=bd_out(c_im),
                d=d.reshape(1, -1), wg=w_glu.astype(BF16), bg=b_glu.reshape(1, -1))


_HALO = SUBLANES


def _lru_kernel(lx_ref, lg_ref, zprev_ref, cst_ref, h0_ref, cw_ref, cb_ref, wa_ref, ba_ref, wx_ref, bx_ref,
                lam_ref, y_ref, hfin_ref, cs_scr, xc_scr, a_scr, b_scr, h_c, *, nseq, sl):
    tm = nseq * sl
    j = pl.program_id(0)

    @pl.when(j == 0)
    def _():
        h_c[...] = h0_ref[...]

    k = LRU_CONV
    for s in range(nseq):
        cs_scr[0:_HALO, :] = jnp.where(j == 0, cst_ref[s], zprev_ref[s])
        cs_scr[_HALO:_HALO + sl, :] = lx_ref[s]
        xc = cb_ref[...]
        for t in range(k):
            off = _HALO - (k - 1) + t
            xc = xc + cw_ref[t:t + 1, :] * cs_scr[off:off + sl, :]
        xc_scr[s * sl:(s + 1) * sl, :] = xc

    lam = lam_ref[...]
    sp = jnp.maximum(-lam, 0.0) + jnp.log1p(jnp.exp(-jnp.abs(lam)))
    for h in range(LRU_HEADS):
        hs = slice(h * LRU_BLOCK, (h + 1) * LRU_BLOCK)
        xc = xc_scr[:, hs]
        xb = xc.astype(BF16)
        r = _sigmoid(_dot(xb, wa_ref[h]) + ba_ref[:, hs])
        i = _sigmoid(_dot(xb, wx_ref[h]) + bx_ref[:, hs])
        la = -LRU_C * r * sp[:, hs]
        a_scr[:, hs] = jnp.exp(la)
        b_scr[:, hs] = jnp.sqrt(-jnp.tanh(la) * (jnp.exp(2.0 * la) + 1.0)) * (i * xc)

    grp = min(nseq, 4)
    for g0 in range(0, nseq, grp):
        def body(t, hs, g0=g0):
            out = []
            for q in range(grp):
                idx = pl.ds((g0 + q) * sl + t, 1)
                h = a_scr[idx, :] * hs[q] + b_scr[idx, :]
                b_scr[idx, :] = h
                out.append(h)
            return tuple(out)

        fin = lax.fori_loop(0, sl, body, tuple(h_c[g0 + q:g0 + q + 1, :] for q in range(grp)), unroll=8)
        for q in range(grp):
            h_c[g0 + q:g0 + q + 1, :] = fin[q]

    y = b_scr[...] * _gelu(lg_ref[...].reshape(tm, BRANCH_WIDTH))
    y_ref[...] = y.reshape(nseq, sl, BRANCH_WIDTH)

    @pl.when(j == pl.num_programs(0) - 1)
    def _():
        hfin_ref[...] = h_c[...]


def _lru(z3, cst, h0, p, sl):
    nseq, length, _ = z3.shape
    tm = nseq * sl
    w = BRANCH_WIDTH
    const2 = lambda j: (0, 0)
    const3 = lambda j: (0, 0, 0)
    st_spec = pl.BlockSpec((nseq, w), const2)
    vec = pl.BlockSpec((1, w), const2)
    wsp = pl.BlockSpec((LRU_HEADS, LRU_BLOCK, LRU_BLOCK), const3)
    rb = sl // _HALO
    kern = functools.partial(_lru_kernel, nseq=nseq, sl=sl)
    return pl.pallas_call(
        kern,
        grid=(length // sl,),
        in_specs=[pl.BlockSpec((nseq, sl, w), lambda j: (0, j, COL_LX // w)),
                  pl.BlockSpec((nseq, sl, w), lambda j: (0, j, COL_LG // w)),
                  pl.BlockSpec((nseq, _HALO, w), lambda j: (0, jnp.maximum(j * rb - 1, 0), COL_LX // w)),
                  pl.BlockSpec((nseq, _HALO, w), const3),
                  st_spec,
                  pl.BlockSpec((LRU_CONV, w), const2), vec, wsp, vec, wsp, vec, vec],
        out_specs=[pl.BlockSpec((nseq, sl, w), lambda j: (0, j, 0)), st_spec],
        out_shape=[jax.ShapeDtypeStruct((nseq, length, w), F32), jax.ShapeDtypeStruct(h0.shape, F32)],
        scratch_shapes=[pltpu.VMEM((_HALO + sl, w), F32), pltpu.VMEM((tm, w), F32),
                        pltpu.VMEM((tm, w), F32), pltpu.VMEM((tm, w), F32),
                        pltpu.VMEM((nseq, w), F32)],
        compiler_params=_cparams(("arbitrary",)),
        name="lru",
    )(z3, z3, z3, cst, h0, p["cw"], p["cb"], p["wa"], p["ba"], p["wx"], p["bx"], p["lam"])


def _da_lambda(lq1_ref, lk1_ref, lq2_ref, lk2_ref, lam_init):
    return (jnp.exp(jnp.sum(lq1_ref[...] * lk1_ref[...])) - jnp.exp(jnp.sum(lq2_ref[...] * lk2_ref[...]))
            + lam_init)


def _attn_kernel(q_ref, k_ref, v_ref, lq1_ref, lk1_ref, lq2_ref, lk2_ref, dn_ref, o_ref,
                 m_scr, l_scr, acc_scr, *, tq, tk, lam_init):
    qi, ki = pl.program_id(2), pl.program_id(3)

    @pl.when(ki == 0)
    def _():
        m_scr[...] = jnp.full(m_scr.shape, -jnp.inf, F32)
        l_scr[...] = jnp.zeros(l_scr.shape, F32)
        acc_scr[...] = jnp.zeros(acc_scr.shape, F32)

    def step(masked):
        q = q_ref[...] * (DA_DK ** -0.5)
        kb = k_ref[...].astype(BF16)
        vb = v_ref[...].astype(BF16)
        lane = lax.broadcasted_iota(jnp.int32, (tq, LANES), 1)
        if masked:
            row = qi * tq + lax.broadcasted_iota(jnp.int32, (tq, tk), 0)
            col = ki * tk + lax.broadcasted_iota(jnp.int32, (tq, tk), 1)
            mask = col <= row
        comps = range(2)
        ss = [lax.dot_general(jnp.where((lane // DA_DK) == c, q, 0.0).astype(BF16), kb, _NT,
                              preferred_element_type=F32) for c in comps]
        if masked:
            ss = [jnp.where(mask, s, -jnp.inf) for s in ss]
        m_prev = [m_scr[c] for c in comps]
        l_prev = [l_scr[c] for c in comps]
        a_prev = [acc_scr[c] for c in comps]
        m_new = [jnp.maximum(m_prev[c], jnp.max(ss[c], axis=-1, keepdims=True)) for c in comps]
        alpha = [jnp.exp(m_prev[c] - m_new[c]) for c in comps]
        pexp = [jnp.exp(ss[c] - m_new[c]) for c in comps]
        pv = [_dot(pexp[c].astype(BF16), vb) for c in comps]
        for c in comps:
            l_scr[c] = alpha[c] * l_prev[c] + jnp.sum(pexp[c], axis=-1, keepdims=True)
            acc_scr[c] = alpha[c] * a_prev[c] + pv[c]
            m_scr[c] = m_new[c]

    last_col, first_col = ki * tk + tk - 1, ki * tk
    first_row, last_row = qi * tq, qi * tq + tq - 1

    @pl.when(last_col <= first_row)
    def _():
        step(False)

    @pl.when((last_col > first_row) & (first_col <= last_row))
    def _():
        step(True)

    @pl.when(ki == pl.num_programs(3) - 1)
    def _():
        lam = _da_lambda(lq1_ref, lk1_ref, lq2_ref, lk2_ref, lam_init)
        o = acc_scr[0] / l_scr[0] - lam * (acc_scr[1] / l_scr[1])
        ms = jnp.mean(o * o, axis=-1, keepdims=True)
        o_ref[...] = o * lax.rsqrt(ms + EPS) * dn_ref[...] * (1.0 - lam_init)


def _attn_prompt(q_rot, k_rot, z, lam_p, dn, nb, length, tq, tk, lam_init):
    m = q_rot.shape[0]
    nq, nk = length // tq, length // tk
    const2 = lambda b, h, qi, ki: (0, 0)
    kv_row = lambda b, qi, ki: b * nk + jnp.minimum(ki, (qi * tq + tq - 1) // tk)
    vec = pl.BlockSpec((1, DA_DK), const2)
    kern = functools.partial(_attn_kernel, tq=tq, tk=tk, lam_init=lam_init)
    return pl.pallas_call(
        kern,
        grid=(nb, DA_HEADS, nq, nk),
        in_specs=[pl.BlockSpec((tq, LANES), lambda b, h, qi, ki: (b * nq + qi, h)),
                  pl.BlockSpec((tk, LANES), lambda b, h, qi, ki: (kv_row(b, qi, ki), h)),
                  pl.BlockSpec((tk, DA_DV), lambda b, h, qi, ki: (kv_row(b, qi, ki), COL_DV // DA_DV + h)),
                  vec, vec, vec, vec, pl.BlockSpec((1, DA_DV), const2)],
        out_specs=pl.BlockSpec((tq, DA_DV), lambda b, h, qi, ki: (b * nq + qi, h)),
        out_shape=jax.ShapeDtypeStruct((m, DA_HEADS * DA_DV), F32),
        scratch_shapes=[pltpu.VMEM((2, tq, 1), F32), pltpu.VMEM((2, tq, 1), F32),
                        pltpu.VMEM((2, tq, DA_DV), F32)],
        compiler_params=_cparams(("arbitrary", "arbitrary", "arbitrary", "arbitrary")),
        name="attn_prompt",
    )(q_rot, k_rot, z, *lam_p, dn)


_PAGES_PER_STEP = 16
_QROWS = DA_HEADS * 2 * SUBLANES


def _attn_sample_kernel(pt_ref, q_ref, *refs, lq, lam_init):
    npg = _PAGES_PER_STEP
    kt_refs, v_refs = refs[0:npg], refs[npg:2 * npg]
    kn_ref, vn_ref, lq1_ref, lk1_ref, lq2_ref, lk2_ref, dn_ref = refs[2 * npg:2 * npg + 7]
    o_ref = refs[2 * npg + 7]
    qbd_scr, m_scr, l_scr, acc_scr = refs[2 * npg + 8:]
    w = DA_HEADS * 2 * DA_DK
    hrows = 2 * lq
    j = pl.program_id(1)

    @pl.when(j == 0)
    def _():
        q = q_ref[...] * (DA_DK ** -0.5)
        lane = lax.broadcasted_iota(jnp.int32, (lq, w), 1)
        tiles = [jnp.where((lane // DA_DK) == hc, q, 0.0) for hc in range(DA_HEADS * 2)]
        qbd_scr[...] = jnp.concatenate(tiles, axis=0)
        m_scr[...] = jnp.full(m_scr.shape, -jnp.inf, F32)
        l_scr[...] = jnp.zeros(l_scr.shape, F32)
        acc_scr[...] = jnp.zeros(acc_scr.shape, F32)

    qbd = qbd_scr[...].astype(BF16)

    def update(s, nchunk, vget):
        m_prev = m_scr[...]
        m_new = jnp.maximum(m_prev, jnp.max(s, axis=-1, keepdims=True))
        alpha = jnp.exp(m_prev - m_new)
        pexp = jnp.exp(s - m_new)
        l_scr[...] = alpha * l_scr[...] + jnp.sum(pexp, axis=-1, keepdims=True)
        pb = pexp.astype(BF16)
        for h in range(DA_HEADS):
            rs = slice(h * hrows, (h + 1) * hrows)
            pv = _dot(pb[rs, 0:PAGE_SIZE], vget(0, h))
            for i in range(1, nchunk):
                pv = pv + _dot(pb[rs, i * PAGE_SIZE:(i + 1) * PAGE_SIZE], vget(i, h))
            acc_scr[rs, :] = alpha[rs] * acc_scr[rs, :] + pv
        m_scr[...] = m_new

    s_pages = jnp.concatenate([_dot(qbd, kt_refs[i][...].astype(BF16)) for i in range(npg)], axis=1)
    update(s_pages, npg, lambda i, h: v_refs[i][pl.ds(h, PAGE_SIZE, stride=DA_HEADS), :].astype(BF16))

    @pl.when(j == pl.num_programs(1) - 1)
    def _():
        nk = kn_ref.shape[0]
        t = lax.broadcasted_iota(jnp.int32, (_QROWS, nk), 0) % lq
        col = lax.broadcasted_iota(jnp.int32, (_QROWS, nk), 1)
        s_new = lax.dot_general(qbd, kn_ref[...].astype(BF16), _NT, preferred_element_type=F32)
        update(jnp.where(col <= t, s_new, -jnp.inf), 1,
               lambda i, h: vn_ref[:, h * DA_DV:(h + 1) * DA_DV].astype(BF16))
        lam = _da_lambda(lq1_ref, lk1_ref, lq2_ref, lk2_ref, lam_init)
        for h in range(DA_HEADS):
            r0 = h * hrows
            o0 = acc_scr[r0:r0 + lq, :] / l_scr[r0:r0 + lq]
            o1 = acc_scr[r0 + lq:r0 + 2 * lq, :] / l_scr[r0 + lq:r0 + 2 * lq]
            o = o0 - lam * o1
            ms = jnp.mean(o * o, axis=-1, keepdims=True)
            o_ref[:, h * DA_DV:(h + 1) * DA_DV] = o * lax.rsqrt(ms + EPS) * dn_ref[...] * (1.0 - lam_init)


def _attn_sample(q_rot, cache_k, cache_v, layer, page_table, k_new, v_new, lam_p, dn, lam_init):
    nb, n_pages = page_table.shape
    lq = q_rot.shape[0] // nb
    w = DA_HEADS * 2 * DA_DK
    wv = DA_HEADS * DA_DV
    npg = _PAGES_PER_STEP
    nsteps = n_pages // npg
    ckt = jnp.transpose(cache_k, (0, 1, 3, 4, 5, 2)).reshape(DEPTH, -1, w, PAGE_SIZE)
    cv2 = cache_v.reshape(DEPTH, -1, PAGE_SIZE * DA_HEADS, DA_DV)
    const2 = lambda b, j, pt: (0, 0)
    page = lambda b, j, pt, i: pt[b * n_pages + j * npg + i]

    def k_spec(i):
        return pl.BlockSpec((None, None, w, PAGE_SIZE), lambda b, j, pt: (layer, page(b, j, pt, i), 0, 0))

    def v_spec(i):
        return pl.BlockSpec((None, None, PAGE_SIZE * DA_HEADS, DA_DV),
                            lambda b, j, pt: (layer, page(b, j, pt, i), 0, 0))

    vec = pl.BlockSpec((1, DA_DK), const2)
    nk = k_new.shape[1]
    grid_spec = pltpu.PrefetchScalarGridSpec(
        num_scalar_prefetch=1,
        grid=(nb, nsteps),
        in_specs=([pl.BlockSpec((lq, w), lambda b, j, pt: (b, 0))]
                  + [k_spec(i) for i in range(npg)] + [v_spec(i) for i in range(npg)]
                  + [pl.BlockSpec((None, nk, w), lambda b, j, pt: (b, 0, 0)),
                     pl.BlockSpec((None, nk, wv), lambda b, j, pt: (b, 0, 0)),
                     vec, vec, vec, vec, pl.BlockSpec((1, DA_DV), const2)]),
        out_specs=pl.BlockSpec((lq, wv), lambda b, j, pt: (b, 0)),
        scratch_shapes=[pltpu.VMEM((_QROWS, w), F32), pltpu.VMEM((_QROWS, 1), F32),
                        pltpu.VMEM((_QROWS, 1), F32), pltpu.VMEM((_QROWS, DA_DV), F32)],
    )
    kern = functools.partial(_attn_sample_kernel, lq=lq, lam_init=lam_init)
    return pl.pallas_call(
        kern,
        grid_spec=grid_spec,
        out_shape=jax.ShapeDtypeStruct((nb * lq, wv), F32),
        compiler_params=_cparams(("arbitrary", "arbitrary")),
        name="attn_sample",
    )(page_table.reshape(-1), q_rot, *([ckt] * npg), *([cv2] * npg), k_new, v_new, *lam_p, dn)


def _merge_kernel(o0_ref, o1_ref, o2_ref, o3_ref, g_ref, wb_ref, wo_ref, x_ref, g1_ref, lng_ref, lnb_ref,
                  out_ref, acc_scr):
    b = pl.program_id(1)

    @pl.when(b == 0)
    def _():
        acc_scr[...] = jnp.zeros(acc_scr.shape, F32)

    o = jnp.where(b == 0, o0_ref[...], jnp.where(b == 1, o1_ref[...], jnp.where(b == 2, o2_ref[...], o3_ref[...])))
    acc_scr[...] += _sigmoid(g_ref[...].astype(F32)) * _dot(o.astype(BF16), wb_ref[...])

    @pl.when(b == N_BRANCH - 1)
    def _():
        mix = _dot(acc_scr[...].astype(BF16), wo_ref[...])
        y = DEEPNORM_ALPHA * x_ref[...] + g1_ref[...] * mix
        out_ref[...] = _layer_norm(y, lng_ref[...], lnb_ref[...])


def _merge(branches, gates, w_branch, w_out, layer, x2d, g1, gdiv, ln_g, ln_b, tm):
    m = x2d.shape[0]
    d = D_MODEL
    r = g1.shape[1]
    const2 = lambda i, b: (0, 0)
    osp = pl.BlockSpec((tm, BRANCH_WIDTH), lambda i, b: (i, 0))
    return pl.pallas_call(
        _merge_kernel,
        grid=(m // tm, N_BRANCH),
        in_specs=[osp, osp, osp, osp,
                  pl.BlockSpec((tm, d), lambda i, b: (i, b)),
                  pl.BlockSpec((None, None, BRANCH_WIDTH, d), lambda i, b: (layer, b, 0, 0)),
                  pl.BlockSpec((None, d, d), lambda i, b: (layer, 0, 0), pipeline_mode=pl.Buffered(1)),
                  pl.BlockSpec((tm, d), lambda i, b: (i, 0)),
                  pl.BlockSpec((None, r, d), lambda i, b: (i // gdiv, 0, 0)),
                  pl.BlockSpec((1, d), const2), pl.BlockSpec((1, d), const2)],
        out_specs=pl.BlockSpec((tm, d), lambda i, b: (i, 0)),
        out_shape=jax.ShapeDtypeStruct((m, d), F32),
        scratch_shapes=[pltpu.VMEM((tm, d), F32)],
        compiler_params=_cparams(("arbitrary", "arbitrary")),
        name="merge",
    )(*branches, gates, w_branch, w_out, x2d, g1, ln_g, ln_b)


_FHALO = 2 * SUBLANES


def _ffn_kernel(x_ref, xprev_ref, sc_ref, sh_ref, g2_ref, csta_ref, cstv_ref, wa_ref, wv_ref, cw_ref, cb_ref,
                wd_ref, lng_ref, lnb_ref, out_ref, sta_ref, stv_ref, u_scr, hsa, hsv, act_scr,
                *, nseq, sl, first_div, rc):
    tm = nseq * sl
    halo = _FHALO if nseq == 1 else 0
    seg = _FHALO + sl
    i, j = pl.program_id(0), pl.program_id(1)
    nf = pl.num_programs(1)

    @pl.when(j == 0)
    def _():
        sc, sh = 1.0 + sc_ref[...], sh_ref[...]
        if halo:
            u_scr[0:halo, :] = (xprev_ref[...] * sc + sh).astype(BF16)
        u_scr[halo:halo + tm, :] = (x_ref[...] * sc + sh).astype(BF16)
        out_ref[...] = jnp.zeros(out_ref.shape, F32)

    parts = ((wa_ref, csta_ref, hsa, sta_ref), (wv_ref, cstv_ref, hsv, stv_ref))
    if nseq == 1:
        half = tm // 2
        spans = ((0, half), (half, tm))
        first = i % first_div == 0
        for lo, hi in ((0, _FHALO + half), (_FHALO + half, _FHALO + tm)):
            for w_ref, cst_ref, hs, _ in parts:
                hs[lo:hi, :] = _dot(u_scr[lo:hi, :], w_ref[...])
                if lo == 0:
                    hs[_FHALO - _HALO:_FHALO, :] = jnp.where(first, cst_ref[0], hs[_FHALO - _HALO:_FHALO, :])
    else:
        spans = ((0, tm),)
        u = u_scr[...]
        for w_ref, cst_ref, hs, _ in parts:
            hup = _dot(u, w_ref[...])
            for s in range(nseq):
                hs[s * seg + _FHALO - _HALO:s * seg + _FHALO, :] = cst_ref[s]
                hs[s * seg + _FHALO:(s + 1) * seg, :] = hup[s * sl:(s + 1) * sl]
    for _, _, hs, st_ref in parts:
        for s in range(nseq):
            st_ref[s] = hs[(s + 1) * seg - _HALO:(s + 1) * seg, :]

    k = FFN_CONV
    base = _FHALO - (k - 1)
    for lo, hi in spans:
        for r0 in range(lo, hi, rc):
            s, t0 = divmod(r0, sl)
            o = s * seg + base + t0
            ya, yv = cb_ref[0], cb_ref[1]
            for t in range(k):
                ya = ya + cw_ref[0, t:t + 1, :] * hsa[o + t:o + t + rc, :]
                yv = yv + cw_ref[1, t:t + 1, :] * hsv[o + t:o + t + rc, :]
            act_scr[r0:r0 + rc, :] = (_gelu(ya) * yv).astype(act_scr.dtype)
        out_ref[lo:hi, :] += _dot(act_scr[lo:hi, :].astype(BF16), wd_ref[...])

    @pl.when(j == nf - 1)
    def _():
        y = DEEPNORM_ALPHA * x_ref[...] + g2_ref[...] * out_ref[...]
        out_ref[...] = _layer_norm(y, lng_ref[...], lnb_ref[...])


_FFN_TF = 512


def _ffn(x2d, sc, sh, g2, gdiv, cst, w_up, conv_w, conv_b, w_down, layer, ln_g, ln_b, nseq, sl, first_div, tf):
    m, d = x2d.shape
    tm = nseq * sl
    nblk = m // tm
    nf = D_FF // tf
    r = sc.shape[1]
    halo = _FHALO if nseq == 1 else 0
    const2 = lambda i, j: (0, 0)
    mod_spec = pl.BlockSpec((None, r, d), lambda i, j: (i // gdiv, 0, 0))
    rb = tm // _FHALO
    cw2 = conv_w.reshape(FFN_CONV, 2, D_FF).transpose(1, 0, 2)
    cb2 = conv_b.reshape(2, 1, D_FF)
    csta, cstv = cst[..., :D_FF], cst[..., D_FF:]
    cst_spec = pl.BlockSpec((None, nseq, _HALO, tf), lambda i, j: (i // first_div, 0, 0, j))
    st_spec = pl.BlockSpec((None, nseq, _HALO, tf), lambda i, j: (i, 0, 0, j))
    st_shape = jax.ShapeDtypeStruct((nblk, nseq, _HALO, D_FF), F32)
    rc = 32 if sl % 32 == 0 else sl
    act_dtype = BF16 if rc % (2 * SUBLANES) == 0 else F32
    kern = functools.partial(_ffn_kernel, nseq=nseq, sl=sl, first_div=first_div, rc=rc)
    return pl.pallas_call(
        kern,
        grid=(nblk, nf),
        in_specs=[pl.BlockSpec((tm, d), lambda i, j: (i, 0), pipeline_mode=pl.Buffered(1)),
                  pl.BlockSpec((_FHALO, d), lambda i, j: (jnp.maximum(i * rb - 1, 0), 0)),
                  mod_spec, mod_spec, mod_spec, cst_spec, cst_spec,
                  pl.BlockSpec((None, d, tf), lambda i, j: (layer, 0, j)),
                  pl.BlockSpec((None, d, tf), lambda i, j: (layer, 0, nf + j)),
                  pl.BlockSpec((2, FFN_CONV, tf), lambda i, j: (0, 0, j)),
                  pl.BlockSpec((2, 1, tf), lambda i, j: (0, 0, j)),
                  pl.BlockSpec((None, tf, d), lambda i, j: (layer, j, 0)),
                  pl.BlockSpec((1, d), const2), pl.BlockSpec((1, d), const2)],
        out_specs=[pl.BlockSpec((tm, d), lambda i, j: (i, 0), pipeline_mode=pl.Buffered(1)), st_spec, st_spec],
        out_shape=[jax.ShapeDtypeStruct((m, d), F32), st_shape, st_shape],
        scratch_shapes=[pltpu.VMEM((halo + tm, d), BF16),
                        pltpu.VMEM((nseq * (_FHALO + sl), tf), F32), pltpu.VMEM((nseq * (_FHALO + sl), tf), F32),
                        pltpu.VMEM((tm, tf), act_dtype)],
        compiler_params=_cparams(("arbitrary", "arbitrary")),
        name="ffn",
    )(x2d, x2d, sc, sh, g2, csta, cstv, w_up, w_up, cw2, cb2, w_down, ln_g, ln_b)


def _pad_state_rows(buf, rows):
    pad = [(0, 0)] * buf.ndim
    pad[-2] = (rows - buf.shape[-2], 0)
    return jnp.pad(buf, pad)


def _run_layer(x2d, mod, cfg, wts, l, states, paged):
    nb, length = cfg["nb"], cfg["len"]
    m = nb * length
    sh1, sc1, g1, sh2, sc2, g2 = mod
    gdiv = cfg["gdiv"]
    st_gla, st_s5re, st_s5im, st_lru, st_lconv, st_fconv = states

    gates, z, zlr = _inproj(x2d, sc1, sh1, gdiv(cfg["tm_in"]), wts["w_kn"], wts["w_qkv"], l,
                            cfg["tm_in"], cfg["tng"], cfg["tnm"])

    ng_seq, nseq = cfg["ng"], cfg["nseq"]
    o_gla, gla_new = _gla(z, zlr, _gla_state_pad(st_gla), wts["wa2p"], wts["gla_ba"], wts["gla_norm"],
                          ng_seq, m // ng_seq, cfg["gla_c"], cfg["gla_seq"])
    gla_new = _gla_state_unpad(gla_new)

    sl = cfg["sl"]
    o_s5, s5re_new, s5im_new = _s5(z.reshape(nb, length, Z_MIX), st_s5re.reshape(nb, _S5_CH, LANES),
                                   st_s5im.reshape(nb, _S5_CH, LANES), wts["s5"], cfg["s5_sl"])
    o_s5 = o_s5.reshape(m, BRANCH_WIDTH)
    s5re_new = s5re_new.reshape(nb, S5_GROUPS, S5_N)
    s5im_new = s5im_new.reshape(nb, S5_GROUPS, S5_N)

    o_lru, lru_new = _lru(z.reshape(nb, length, Z_MIX), _pad_state_rows(st_lconv, _HALO), st_lru, wts["lru"],
                          cfg["s5_sl"])
    o_lru = o_lru.reshape(m, BRANCH_WIDTH)
    lconv_new = z[:, COL_LX:COL_LX + BRANCH_WIDTH].reshape(nb, length, BRANCH_WIDTH)[:, length - (LRU_CONV - 1):]

    lam_init = 0.8 - 0.6 * math.exp(-0.3 * l)
    q_rot, k_rot = _rope(z, cfg["rope"], nb, length, cfg["tm_rope"])
    v_new = z[:, COL_DV:COL_DV + DA_HEADS * DA_DV]
    if paged is None:
        o_da = _attn_prompt(q_rot, k_rot, z, wts["lam_p"], wts["da_norm"], nb, length,
                            cfg["tq"], cfg["tk"], lam_init)
    else:
        page_table, cache_k, cache_v = paged
        padk = ((0, 0), (0, PAGE_SIZE - length), (0, 0))
        k_pad = jnp.pad(k_rot.reshape(nb, length, -1), padk)
        v_pad = jnp.pad(v_new.reshape(nb, length, -1), padk)
        o_da = _attn_sample(q_rot, cache_k, cache_v, l, page_table, k_pad, v_pad, wts["lam_p"],
                            wts["da_norm"], lam_init)

    x1 = _merge((o_gla, o_s5, o_lru, o_da), gates, wts["w_branch"], wts["w_out"], l, x2d, g1, gdiv(cfg["tm_mg"]),
                wts["ln1_g"], wts["ln1_b"], cfg["tm_mg"])

    fcst = _pad_state_rows(st_fconv, _HALO).reshape(ng_seq, nseq, _HALO, 2 * D_FF)
    fsl = cfg["ffn_sl"]
    x2, sta, stv = _ffn(x1, sc2, sh2, g2, gdiv(nseq * fsl), fcst, wts["w_up"], wts["ffn_conv_w"],
                        wts["ffn_conv_b"], wts["w_down"], l, wts["ln2_g"], wts["ln2_b"], nseq, fsl,
                        length // fsl if nseq == 1 else 1, _FFN_TF)
    fst = jnp.concatenate([sta, stv], axis=-1)
    if nseq == 1:
        per_seq = length // fsl
        fst = fst.reshape(nb, per_seq, _HALO, 2 * D_FF)[:, per_seq - 1]
    else:
        fst = fst.reshape(nb, _HALO, 2 * D_FF)
    fconv_new = fst[:, _HALO - (FFN_CONV - 1):]

    new = (k_rot.reshape(nb, length, DA_HEADS, 2, DA_DK), v_new.reshape(nb, length, DA_HEADS, DA_DV),
           gla_new, s5re_new, s5im_new, lru_new, lconv_new, fconv_new)
    return x2, new


def _layer_weights(l, w_in, gla_wa2, gla_ba, gla_norm, s5_raw, lru_raw, lam_raw, da_norm, w_branch, w_out,
                   ln1_g, ln1_b, ffn_w_up, ffn_conv_w, ffn_conv_b, ffn_w_down, ln2_g, ln2_b):
    w_kn = jnp.pad(w_in, ((0, 0), (0, 0), (_W_SHIFT, 0))).astype(BF16)
    w_qkv = w_in[:, :, :GLA_HEADS * (2 * GLA_DK + GLA_DV)].astype(BF16)
    wa2p = jnp.pad(gla_wa2[l], ((LANES - GLA_LOWRANK, 0), (0, 0))).astype(BF16)
    row = lambda a: a[l].reshape(1, -1)
    cw, cb, wa, ba, wx, bx, lam = [a[l] for a in lru_raw]
    return dict(
        w_kn=w_kn, w_qkv=w_qkv, wa2p=wa2p, gla_ba=row(gla_ba), gla_norm=row(gla_norm),
        s5=_s5_params(*[a[l] for a in s5_raw]),
        lru=dict(cw=cw, cb=cb.reshape(1, -1), wa=wa.astype(BF16), ba=ba.reshape(1, -1),
                 wx=wx.astype(BF16), bx=bx.reshape(1, -1), lam=lam.reshape(1, -1)),
        lam_p=tuple(row(a) for a in lam_raw), da_norm=row(da_norm),
        w_branch=w_branch.astype(BF16), w_out=w_out.astype(BF16),
        ln1_g=row(ln1_g), ln1_b=row(ln1_b),
        w_up=ffn_w_up.astype(BF16), ffn_conv_w=ffn_conv_w[l], ffn_conv_b=ffn_conv_b[l],
        w_down=ffn_w_down.astype(BF16), ln2_g=row(ln2_g), ln2_b=row(ln2_b))


def kernel(x_prompt, x_sample, cache_k, cache_v, state_gla, state_s5_re, state_s5_im, state_lru, state_lru_conv, state_ffn_conv, page_table, c_prompt, c_sample, w_ada, b_ada, w_in, gla_wa2, gla_ba, gla_norm, s5_a_re, s5_a_im, s5_log_dt, s5_b_re, s5_b_im, s5_c_re, s5_c_im, s5_d, s5_w_glu, s5_b_glu, lru_conv_w, lru_conv_b, lru_w_a, lru_b_a, lru_w_x, lru_b_x, lru_lambda, da_lq1, da_lk1, da_lq2, da_lk2, da_norm, w_branch, w_out, ln1_g, ln1_b, ffn_w_up, ffn_conv_w, ffn_conv_b, ffn_w_down, ln2_g, ln2_b):
    bp, lp, d = x_prompt.shape
    bs, ls, _ = x_sample.shape
    past_len = page_table.shape[1] * PAGE_SIZE

    rows = bp + bs
    rows_pad = -(-rows // SUBLANES) * SUBLANES
    c_all = jnp.pad(jnp.concatenate([c_prompt, c_sample], axis=0), ((0, rows_pad - rows), (0, 0)))
    mod_all = _ada(c_all, w_ada, b_ada)

    def mods(l, lo, n, per_row):
        parts = jnp.split(mod_all[l, lo:lo + n], 6, axis=-1)
        if per_row:
            return [jnp.repeat(p, per_row, axis=0)[None] for p in parts]
        return [p[:, None, :] for p in parts]

    cfg_p = dict(nb=bp, len=lp, ng=bp, nseq=1, sl=256, s5_sl=128, ffn_sl=1024, tm_in=1024, tng=1024, tnm=512, gla_c=128,
                 gla_seq=128, tm_rope=512, tq=1024, tk=512, tm_mg=512, tf=512,
                 gdiv=lambda tm: lp // tm, rope=_rope_tables(0, lp))
    cfg_s = dict(nb=bs, len=ls, ng=1, nseq=bs, sl=ls, s5_sl=ls, ffn_sl=ls, tm_in=bs * ls, tng=1024, tnm=512, gla_c=bs * ls,
                 gla_seq=ls, tm_rope=ls, tm_mg=bs * ls, tf=512,
                 gdiv=lambda tm: 1, rope=_rope_tables(past_len, ls))

    zeros_p = (jnp.zeros((bp, GLA_HEADS, GLA_DK, GLA_DV), F32), jnp.zeros((bp, S5_GROUPS, S5_N), F32),
               jnp.zeros((bp, S5_GROUPS, S5_N), F32), jnp.zeros((bp, BRANCH_WIDTH), F32),
               jnp.zeros((bp, LRU_CONV - 1, BRANCH_WIDTH), F32), jnp.zeros((bp, FFN_CONV - 1, 2 * D_FF), F32))

    xp = x_prompt.reshape(bp * lp, d)
    xs = x_sample.reshape(bs * ls, d)
    col_p = [[] for _ in range(8)]
    col_s = [[] for _ in range(8)]
    s5_raw = (s5_a_re, s5_a_im, s5_log_dt, s5_b_re, s5_b_im, s5_c_re, s5_c_im, s5_d, s5_w_glu, s5_b_glu)
    lru_raw = (lru_conv_w, lru_conv_b, lru_w_a, lru_b_a, lru_w_x, lru_b_x, lru_lambda)
    lam_raw = (da_lq1, da_lk1, da_lq2, da_lk2)
    for l in range(DEPTH):
        wts = _layer_weights(l, w_in, gla_wa2, gla_ba, gla_norm, s5_raw, lru_raw, lam_raw, da_norm, w_branch,
                             w_out, ln1_g, ln1_b, ffn_w_up, ffn_conv_w, ffn_conv_b, ffn_w_down, ln2_g, ln2_b)
        xp, new_p = _run_layer(xp, mods(l, 0, bp, 0), cfg_p, wts, l, zeros_p, None)
        st_s = (state_gla[l], state_s5_re[l], state_s5_im[l], state_lru[l], state_lru_conv[l], state_ffn_conv[l])
        xs, new_s = _run_layer(xs, mods(l, bp, bs, ls), cfg_s, wts, l, st_s, (page_table, cache_k, cache_v))
        for lst, s in zip(col_p, new_p):
            lst.append(s)
        for lst, s in zip(col_s, new_s):
            lst.append(s)

    sp = [jnp.stack(s) for s in col_p]
    ss = [jnp.stack(s) for s in col_s]
    out = [xp.reshape(bp, lp, d), xs.reshape(bs, ls, d)]
    for a, b in zip(sp, ss):
        out += [a, b]
    return tuple(out)
```

```python
import functools
import math

import numpy as np
import jax
import jax.numpy as jnp
from jax import lax
from jax.experimental import pallas as pl
from jax.experimental.pallas import tpu as pltpu

F32 = jnp.float32
BF16 = jnp.bfloat16

D_MODEL = 2048
DEPTH = 2
PAGE_SIZE = 128
N_BRANCH = 4
BRANCH_WIDTH = 512
GLA_HEADS = 4
GLA_DK = 64
GLA_DV = 128
GLA_LOWRANK = 16
GLA_TAU = 16.0
S5_GROUP = 16
S5_GROUPS = 32
S5_N = 64
S5_STATES = S5_GROUPS * S5_N
LRU_HEADS = 4
LRU_BLOCK = 128
LRU_CONV = 4
LRU_C = 8.0
DA_HEADS = 4
DA_DK = 64
DA_DV = 128
ROPE_DIM = 16
ROPE_THETA = 500000.0
D_FF = 5632
FFN_CONV = 3
DEEPNORM_ALPHA = (2.0 * DEPTH) ** 0.25
EPS = 1e-5

LANES = 128
SUBLANES = 8
VMEM_LIMIT = 58 * 1024 * 1024

Z_GATE = N_BRANCH * D_MODEL
Z_MIX = 4608
COL_GQ, COL_GK, COL_GV, COL_OG = 0, 256, 512, 1024
COL_SU, COL_LX, COL_LG = 1536, 2048, 2560
COL_DQ, COL_DK, COL_DV = 3072, 3584, 4096

_NT = (((1,), (1,)), ((), ()))
_TN = (((0,), (0,)), ((), ()))


def _cparams(sem):
    return pltpu.CompilerParams(dimension_semantics=sem, vmem_limit_bytes=VMEM_LIMIT)


def _dot(a, b):
    return jnp.dot(a, b, preferred_element_type=F32)


def _sigmoid(x):
    return 0.5 * jnp.tanh(0.5 * x) + 0.5


def _gelu(x):
    return 0.5 * x * (1.0 + jnp.tanh(math.sqrt(2.0 / math.pi) * (x + 0.044715 * (x * x * x))))


def _layer_norm(y, g, b):
    mu = jnp.mean(y, axis=-1, keepdims=True)
    d = y - mu
    var = jnp.mean(d * d, axis=-1, keepdims=True)
    return d * lax.rsqrt(var + EPS) * g + b


def _ada_kernel(c_ref, w_ref, b_ref, o_ref):
    c = c_ref[...]
    s = c * _sigmoid(c)
    o_ref[...] = _dot(s.astype(BF16), w_ref[...].astype(BF16)) + b_ref[...]


def _ada(c_all, w_ada, b_ada):
    rows = c_all.shape[0]
    n = w_ada.shape[-1]
    tn = 1536
    return pl.pallas_call(
        _ada_kernel,
        grid=(DEPTH, n // tn),
        in_specs=[pl.BlockSpec((rows, D_MODEL), lambda l, j: (0, 0)),
                  pl.BlockSpec((None, D_MODEL, tn), lambda l, j: (l, 0, j)),
                  pl.BlockSpec((None, 1, tn), lambda l, j: (l, 0, j))],
        out_specs=pl.BlockSpec((None, rows, tn), lambda l, j: (l, 0, j)),
        out_shape=jax.ShapeDtypeStruct((DEPTH, rows, n), F32),
        compiler_params=_cparams(("arbitrary", "arbitrary")),
        name="ada",
    )(c_all, w_ada, b_ada.reshape(DEPTH, 1, n))


def _inproj_kernel(x_ref, sc_ref, sh_ref, wg_ref, wm_ref, wlr_ref, zg_ref, zm_ref, zlr_ref, u_scr, *, ngate):
    j = pl.program_id(1)

    @pl.when(j == 0)
    def _():
        u = (x_ref[...] * (1.0 + sc_ref[...]) + sh_ref[...]).astype(BF16)
        u_scr[...] = u
        zlr_ref[...] = lax.dot_general(u, wlr_ref[0], _NT, preferred_element_type=F32)

    @pl.when(j < ngate)
    def _():
        zg_ref[...] = lax.dot_general(u_scr[...], wg_ref[0], _NT, preferred_element_type=F32).astype(BF16)

    @pl.when(j >= ngate)
    def _():
        zm_ref[...] = lax.dot_general(u_scr[...], wm_ref[0], _NT, preferred_element_type=F32)


def _inproj(x2d, sc, sh, gdiv, wt, layer, tm, tng, tnm):
    m = x2d.shape[0]
    r = sc.shape[1]
    ngate, nmix = Z_GATE // tng, Z_MIX // tnm
    n_qkv = 1024
    rest0 = n_qkv + GLA_LOWRANK
    gate0 = wt.shape[1] - Z_GATE
    assert n_qkv % tnm == 0

    def mix_row(t):
        return jnp.where(t * tnm >= n_qkv, t * tnm + (rest0 - n_qkv), t * tnm)

    row_align = 2 * SUBLANES
    assert gate0 % row_align == 0 and rest0 % row_align == 0
    aligned = lambda r: pl.multiple_of(r, row_align)

    def el_spec(rows, index_map):
        return pl.BlockSpec((pl.Element(1), pl.Element(rows), pl.Element(D_MODEL)), index_map)
    mod_spec = pl.BlockSpec((None, r, D_MODEL), lambda i, j: (i // gdiv, 0, 0))
    kern = functools.partial(_inproj_kernel, ngate=ngate)
    return pl.pallas_call(
        kern,
        grid=(m // tm, ngate + nmix),
        in_specs=[pl.BlockSpec((tm, D_MODEL), lambda i, j: (i, 0), pipeline_mode=pl.Buffered(1)),
                  mod_spec, mod_spec,
                  el_spec(tng, lambda i, j: (layer, aligned(gate0 + jnp.minimum(j, ngate - 1) * tng), 0)),
                  el_spec(tnm, lambda i, j: (layer, aligned(mix_row(jnp.maximum(j - ngate, 0))), 0)),
                  el_spec(LANES, lambda i, j: (layer, n_qkv, 0))],
        out_specs=[pl.BlockSpec((tm, tng), lambda i, j: (i, jnp.minimum(j, ngate - 1))),
                   pl.BlockSpec((tm, tnm), lambda i, j: (i, jnp.maximum(j - ngate, 0))),
                   pl.BlockSpec((tm, LANES), lambda i, j: (i, 0))],
        out_shape=[jax.ShapeDtypeStruct((m, Z_GATE), BF16), jax.ShapeDtypeStruct((m, Z_MIX), F32),
                   jax.ShapeDtypeStruct((m, LANES), F32)],
        scratch_shapes=[pltpu.VMEM((tm, D_MODEL), BF16)],
        compiler_params=_cparams(("arbitrary", "arbitrary")),
        name="inproj",
    )(x2d, sc, sh, wt, wt, wt)


def _rope_tables(pos0, length):
    half = ROPE_DIM // 2
    inv = ROPE_THETA ** (-jnp.arange(half, dtype=F32) * 2.0 / ROPE_DIM)
    ang = (pos0 + jnp.arange(length)).astype(F32)[:, None] * inv
    cos, sin = jnp.cos(ang), jnp.sin(ang)
    ones = jnp.ones((length, DA_DK - ROPE_DIM), F32)
    zeros = jnp.zeros((length, DA_DK - ROPE_DIM), F32)
    zh = jnp.zeros((length, half), F32)
    c = jnp.concatenate([cos, cos, ones], axis=1)
    s_next = jnp.concatenate([-sin, zh, zeros], axis=1)
    s_prev = jnp.concatenate([zh, sin, zeros], axis=1)
    rep = LANES // DA_DK
    return jnp.tile(c, (1, rep)), jnp.tile(s_next, (1, rep)), jnp.tile(s_prev, (1, rep))


def _rope_kernel(q_ref, k_ref, c_ref, sn_ref, sp_ref, qo_ref, ko_ref):
    half = ROPE_DIM // 2
    c, sn, sp = c_ref[...], sn_ref[...], sp_ref[...]
    for src, dst in ((q_ref, qo_ref), (k_ref, ko_ref)):
        for i in range(src.shape[1] // LANES):
            x = src[:, i * LANES:(i + 1) * LANES]
            nxt = pltpu.roll(x, LANES - half, axis=1)
            prv = pltpu.roll(x, half, axis=1)
            dst[:, i * LANES:(i + 1) * LANES] = x * c + nxt * sn + prv * sp


def _rope(z, tabs, nb, length, tm):
    m = z.shape[0]
    w = DA_HEADS * 2 * DA_DK
    nj = length // tm
    tab_spec = pl.BlockSpec((tm, LANES), lambda b, j: (j, 0))
    return pl.pallas_call(
        _rope_kernel,
        grid=(nb, nj),
        in_specs=[pl.BlockSpec((tm, w), lambda b, j: (b * nj + j, COL_DQ // w)),
                  pl.BlockSpec((tm, w), lambda b, j: (b * nj + j, COL_DK // w)),
                  tab_spec, tab_spec, tab_spec],
        out_specs=[pl.BlockSpec((tm, w), lambda b, j: (b * nj + j, 0))] * 2,
        out_shape=[jax.ShapeDtypeStruct((m, w), F32)] * 2,
        compiler_params=_cparams(("arbitrary", "arbitrary")),
        name="rope",
    )(z, z, *tabs)


def _gla_consts(c, seq):
    nlev = int(math.log2(seq))
    t = np.arange(c)
    tt, rr = t[:, None], t[None, :]
    same = (tt // seq) == (rr // seq)
    tril = same & (rr <= tt)
    lvl = np.full((c, c), -1, np.int32)
    lvl[t, t] = nlev
    for i in range(nlev):
        b = seq >> (i + 1)
        valid = ((tt // (2 * b)) == (rr // (2 * b))) & ((tt % (2 * b)) >= b) & ((rr % (2 * b)) < b)
        lvl[valid] = i
    return tril.astype(np.float32), lvl, nlev


def _gla_kernel(q_ref, k_ref, v_ref, og_ref, lr_ref, s0_ref, lmat_ref, lvl_ref, wa2_ref, ba_ref, gn_ref,
                o_ref, st_ref, cum_scr, *, c, seq, nlev):
    nb = c // seq
    j = pl.program_id(1)

    @pl.when(j == 0)
    def _():
        st_ref[...] = s0_ref[...]

    x = _dot(lr_ref[...].astype(BF16), wa2_ref[...]) + ba_ref[...]
    la = (jnp.minimum(x, 0.0) - jnp.log1p(jnp.exp(-jnp.abs(x)))) * (1.0 / GLA_TAU)
    cum = jnp.dot(lmat_ref[...], la, precision=lax.Precision.HIGHEST, preferred_element_type=F32)
    cum_scr[...] = cum
    wq = cum.shape[1]

    def cum_at(idx, group):
        sub = lax.broadcasted_iota(jnp.int32, (SUBLANES, wq), 0)
        bcast = lambda r: jnp.broadcast_to(cum_scr[r:r + 1, :], (SUBLANES, wq))
        tiles = []
        for t0 in range(0, c, SUBLANES):
            tile = bcast(idx(t0))
            for g in range(group, SUBLANES, group):
                tile = jnp.where(sub >= g, bcast(idx(t0 + g)), tile)
            tiles.append(tile)
        return jnp.concatenate(tiles, axis=0)

    e_end = cum_at(lambda t: (t // seq) * seq + seq - 1, seq) - cum
    e_lvl = [cum - cum_at(lambda t, b=seq >> (i + 1): (t // (2 * b)) * (2 * b) + b - 1, 2 * (seq >> (i + 1)))
             for i in range(nlev)]
    q = q_ref[...] * (GLA_DK ** -0.5)
    k = k_ref[...]
    lvl = lvl_ref[...]
    lane = lax.broadcasted_iota(jnp.int32, (c, LANES), 1)
    if nb > 1:
        rowb = lax.broadcasted_iota(jnp.int32, (c, nb * LANES), 0) // seq
        colb = lax.broadcasted_iota(jnp.int32, (c, nb * LANES), 1) // LANES
        bmask = rowb == colb

    def spread(a):
        if nb == 1:
            return a
        return jnp.where(bmask, jnp.concatenate([a] * nb, axis=1), 0.0)

    for p in range(2):
        sl = slice(p * LANES, (p + 1) * LANES)
        qp, kp, cump = q[:, sl], k[:, sl], cum[:, sl]
        qts, kts = [], []
        for i in range(nlev):
            eb = e_lvl[i][:, sl]
            qts.append(qp * jnp.exp(eb))
            kts.append((kp * jnp.exp(-eb)).astype(BF16))
        qg = qp * jnp.exp(cump)
        kh = kp * jnp.exp(e_end[:, sl])
        kpb = kp.astype(BF16)
        decays = []
        for b in range(nb):
            cl = cump[b * seq + seq - 1:b * seq + seq, :]
            decays.append(jnp.broadcast_to(jnp.exp(cl), (LANES, LANES)).T)
        for hh in range(2):
            h = 2 * p + hh
            hm = (lane // GLA_DK) == hh
            att = lax.dot_general(jnp.where(hm, qp, 0.0).astype(BF16), kpb, _NT, preferred_element_type=F32)
            att = jnp.where(lvl == nlev, att, 0.0)
            for i in range(nlev):
                a = lax.dot_general(jnp.where(hm, qts[i], 0.0).astype(BF16), kts[i], _NT,
                                    preferred_element_type=F32)
                att = jnp.where(lvl == i, a, att)
            vh = v_ref[:, h * GLA_DV:(h + 1) * GLA_DV].astype(BF16)
            o = _dot(att.astype(BF16), vh)
            s_all = st_ref[:, h].reshape(nb * LANES, GLA_DV)
            o = o + _dot(spread(jnp.where(hm, qg, 0.0)).astype(BF16), s_all.astype(BF16))
            upd = lax.dot_general(spread(jnp.where(hm, kh, 0.0)).astype(BF16), vh, _TN,
                                  preferred_element_type=F32)
            for b in range(nb):
                st_ref[b, h] = st_ref[b, h] * decays[b] + upd[b * LANES:(b + 1) * LANES]
            ms = jnp.mean(o * o, axis=-1, keepdims=True)
            og = og_ref[:, h * GLA_DV:(h + 1) * GLA_DV]
            o_ref[:, h * GLA_DV:(h + 1) * GLA_DV] = o * lax.rsqrt(ms + EPS) * gn_ref[...] * (og * _sigmoid(og))


def _gla(z, zlr, s0_pad, wa2p, ba, gn, ng, length, c, seq):
    m = z.shape[0]
    nb = c // seq
    nj = length // c
    lmat, lvl, nlev = _gla_consts(c, seq)
    w = GLA_HEADS * GLA_DK
    wv = GLA_HEADS * GLA_DV
    row = lambda g, j: g * nj + j
    const2 = lambda g, j: (0, 0)
    st_spec = pl.BlockSpec((nb, GLA_HEADS, LANES, GLA_DV), lambda g, j: (g, 0, 0, 0))
    kern = functools.partial(_gla_kernel, c=c, seq=seq, nlev=nlev)
    return pl.pallas_call(
        kern,
        grid=(ng, nj),
        in_specs=[pl.BlockSpec((c, w), lambda g, j: (row(g, j), COL_GQ // w)),
                  pl.BlockSpec((c, w), lambda g, j: (row(g, j), COL_GK // w)),
                  pl.BlockSpec((c, wv), lambda g, j: (row(g, j), COL_GV // wv)),
                  pl.BlockSpec((c, wv), lambda g, j: (row(g, j), COL_OG // wv)),
                  pl.BlockSpec((c, LANES), lambda g, j: (row(g, j), 0)),
                  st_spec,
                  pl.BlockSpec(lmat.shape, const2),
                  pl.BlockSpec(lvl.shape, const2),
                  pl.BlockSpec((LANES, w), const2),
                  pl.BlockSpec((1, w), const2),
                  pl.BlockSpec((1, GLA_DV), const2)],
        out_specs=[pl.BlockSpec((c, wv), lambda g, j: (row(g, j), 0)), st_spec],
        out_shape=[jax.ShapeDtypeStruct((m, wv), F32),
                   jax.ShapeDtypeStruct(s0_pad.shape, F32)],
        scratch_shapes=[pltpu.VMEM((c, w), F32)],
        compiler_params=_cparams(("arbitrary", "arbitrary")),
        name="gla",
    )(z, z, z, z, zlr, s0_pad, jnp.asarray(lmat), jnp.asarray(lvl), wa2p, ba, gn)


def _gla_state_pad(s):
    b = s.shape[0]
    s6 = s.reshape(b, 2, 2, 1, GLA_DK, GLA_DV) * jnp.eye(2, dtype=F32)[None, None, :, :, None, None]
    return s6.reshape(b, GLA_HEADS, LANES, GLA_DV)


def _gla_state_unpad(sp):
    b = sp.shape[0]
    s6 = sp.reshape(b, 2, 2, 2, GLA_DK, GLA_DV)
    return jnp.stack([s6[:, :, 0, 0], s6[:, :, 1, 1]], axis=2).reshape(b, GLA_HEADS, GLA_DK, GLA_DV)


_S5_CH = S5_STATES // LANES


def _s5_kernel(u_ref, h0r_ref, h0i_ref, are_ref, aim_ref, ldt_ref, bdr_ref, bdi_ref, cdr_ref, cdi_ref,
               d_ref, wg_ref, bg_ref, y_ref, hfr_ref, hfi_ref, sre, sim, hsc, hcr, hci, *, nseq, sl):
    tm = nseq * sl
    w = BRANCH_WIDTH
    j = pl.program_id(0)

    @pl.when(j == 0)
    def _():
        hcr[...] = h0r_ref[...]
        hci[...] = h0i_ref[...]

    lr, li = are_ref[...], aim_ref[...]
    dt = jnp.exp(ldt_ref[...])
    mag = jnp.exp(lr * dt)
    abr, abi = mag * jnp.cos(li * dt), mag * jnp.sin(li * dt)
    den = lr * lr + li * li
    fr = ((abr - 1.0) * lr + abi * li) / den
    fi = (abi * lr - (abr - 1.0) * li) / den

    u = u_ref[...].reshape(tm, w)
    ub = u.astype(BF16)
    for c2 in range(_S5_CH // 2):
        ws = slice((c2 // 2) * LANES, (c2 // 2 + 1) * LANES)
        tr = _dot(ub[:, ws], bdr_ref[ws, c2 * 256:(c2 + 1) * 256])
        ti = _dot(ub[:, ws], bdi_ref[ws, c2 * 256:(c2 + 1) * 256])
        for e in range(2):
            ch = 2 * c2 + e
            trc, tic = tr[:, e * LANES:(e + 1) * LANES], ti[:, e * LANES:(e + 1) * LANES]
            frc, fic = fr[ch:ch + 1, :], fi[ch:ch + 1, :]
            sre[ch * tm:(ch + 1) * tm, :] = frc * trc - fic * tic
            sim[ch * tm:(ch + 1) * tm, :] = frc * tic + fic * trc

    ar0, ai0, ar1, ai1 = abr[0:8], abi[0:8], abr[8:16], abi[8:16]
    grp = min(nseq, 4)
    for g0 in range(0, nseq, grp):
        def body(t, carry, g0=g0):
            out = []
            for q in range(grp):
                r0, i0, r1, i1 = carry[4 * q:4 * q + 4]
                row = (g0 + q) * sl + t
                lo = pl.ds(row, SUBLANES, stride=tm)
                hi = pl.ds(SUBLANES * tm + row, SUBLANES, stride=tm)
                n_r0 = ar0 * r0 - ai0 * i0 + sre[lo, :]
                n_i0 = ar0 * i0 + ai0 * r0 + sim[lo, :]
                n_r1 = ar1 * r1 - ai1 * i1 + sre[hi, :]
                n_i1 = ar1 * i1 + ai1 * r1 + sim[hi, :]
                dst = pl.ds(pl.multiple_of(row * SUBLANES, SUBLANES), SUBLANES)
                hsc[0, dst, :] = n_r0
                hsc[1, dst, :] = n_i0
                hsc[2, dst, :] = n_r1
                hsc[3, dst, :] = n_i1
                out += [n_r0, n_i0, n_r1, n_i1]
            return tuple(out)

        init = tuple(a for q in range(grp) for a in (hcr[g0 + q, 0:8], hci[g0 + q, 0:8],
                                                     hcr[g0 + q, 8:16], hci[g0 + q, 8:16]))
        fin = lax.fori_loop(0, sl, body, init, unroll=8)
        for q in range(grp):
            hcr[g0 + q, 0:8] = fin[4 * q]
            hci[g0 + q, 0:8] = fin[4 * q + 1]
            hcr[g0 + q, 8:16] = fin[4 * q + 2]
            hci[g0 + q, 8:16] = fin[4 * q + 3]

    ywin = []
    for wi in range(w // LANES):
        ws = slice(wi * LANES, (wi + 1) * LANES)
        yw = d_ref[:, ws] * u[:, ws]
        for c2 in (2 * wi, 2 * wi + 1):
            part = 2 * ((2 * c2) // SUBLANES)
            ca, cb = (2 * c2) % SUBLANES, (2 * c2 + 1) % SUBLANES
            rows = lambda cc: pl.ds(cc, tm, stride=SUBLANES)
            hr, hi = hsc.at[part], hsc.at[part + 1]
            hre = jnp.concatenate([hr[rows(ca), :], hr[rows(cb), :]], axis=1).astype(BF16)
            him = jnp.concatenate([hi[rows(ca), :], hi[rows(cb), :]], axis=1).astype(BF16)
            yw = yw + _dot(hre, cdr_ref[c2 * 256:(c2 + 1) * 256, ws]) - _dot(him, cdi_ref[c2 * 256:(c2 + 1) * 256, ws])
        ywin.append(yw)
    y = _gelu(jnp.concatenate(ywin, axis=1))
    y = y * _sigmoid(_dot(y.astype(BF16), wg_ref[...]) + bg_ref[...])
    y_ref[...] = y.reshape(nseq, sl, w)

    @pl.when(j == pl.num_programs(0) - 1)
    def _():
        hfr_ref[...] = hcr[...]
        hfi_ref[...] = hci[...]


def _s5(z3, h0r, h0i, p, sl):
    nseq, length, _ = z3.shape
    tm = nseq * sl
    w = BRANCH_WIDTH
    const2 = lambda j: (0, 0)
    st_spec = pl.BlockSpec((nseq, _S5_CH, LANES), lambda j: (0, 0, 0))
    vec = pl.BlockSpec((_S5_CH, LANES), const2)
    kern = functools.partial(_s5_kernel, nseq=nseq, sl=sl)
    return pl.pallas_call(
        kern,
        grid=(length // sl,),
        in_specs=[pl.BlockSpec((nseq, sl, w), lambda j: (0, j, COL_SU // w)),
                  st_spec, st_spec, vec, vec, vec,
                  pl.BlockSpec((w, S5_STATES), const2), pl.BlockSpec((w, S5_STATES), const2),
                  pl.BlockSpec((S5_STATES, w), const2), pl.BlockSpec((S5_STATES, w), const2),
                  pl.BlockSpec((1, w), const2), pl.BlockSpec((w, w), const2), pl.BlockSpec((1, w), const2)],
        out_specs=[pl.BlockSpec((nseq, sl, w), lambda j: (0, j, 0)), st_spec, st_spec],
        out_shape=[jax.ShapeDtypeStruct((nseq, length, w), F32),
                   jax.ShapeDtypeStruct(h0r.shape, F32), jax.ShapeDtypeStruct(h0i.shape, F32)],
        scratch_shapes=[pltpu.VMEM((_S5_CH * tm, LANES), F32), pltpu.VMEM((_S5_CH * tm, LANES), F32),
                        pltpu.VMEM((4, SUBLANES * tm, LANES), F32),
                        pltpu.VMEM((nseq, _S5_CH, LANES), F32), pltpu.VMEM((nseq, _S5_CH, LANES), F32)],
        compiler_params=_cparams(("arbitrary",)),
        name="s5",
    )(z3, h0r, h0i, p["are"], p["aim"], p["ldt"], p["bdr"], p["bdi"], p["cdr"], p["cdi"],
      p["d"], p["wg"], p["bg"])


def _s5_params(a_re, a_im, log_dt, b_re, b_im, c_re, c_im, d, w_glu, b_glu):
    eye = jnp.eye(S5_GROUPS, dtype=F32)

    def bd_in(b):
        return (eye[:, None, :, None] * jnp.swapaxes(b, 1, 2)[:, :, None, :]).reshape(
            S5_GROUPS * S5_GROUP, S5_STATES).astype(BF16)

    def bd_out(c):
        return (eye[:, None, :, None] * jnp.swapaxes(c, 1, 2)[:, :, None, :]).reshape(
            S5_STATES, S5_GROUPS * S5_GROUP).astype(BF16)

    return dict(are=a_re.reshape(_S5_CH, LANES), aim=a_im.reshape(_S5_CH, LANES),
                ldt=jnp.broadcast_to(log_dt[:, None], (S5_GROUPS, S5_N)).reshape(_S5_CH, LANES),
                bdr=bd_in(b_re), bdi=bd_in(b_im), cdr=bd_out(c_re), cdi=bd_out(c_im),
                d=d.reshape(1, -1), wg=w_glu.astype(BF16), bg=b_glu.reshape(1, -1))


_HALO = SUBLANES


def _lru_kernel(lx_ref, lg_ref, zprev_ref, cst_ref, h0_ref, cw_ref, cb_ref, wa_ref, ba_ref, wx_ref, bx_ref,
                lam_ref, y_ref, hfin_ref, cs_scr, xc_scr, a_scr, b_scr, h_c, *, nseq, sl):
    tm = nseq * sl
    j = pl.program_id(0)

    @pl.when(j == 0)
    def _():
        h_c[...] = h0_ref[...]

    k = LRU_CONV
    for s in range(nseq):
        cs_scr[0:_HALO, :] = jnp.where(j == 0, cst_ref[s], zprev_ref[s])
        cs_scr[_HALO:_HALO + sl, :] = lx_ref[s]
        xc = cb_ref[...]
        for t in range(k):
            off = _HALO - (k - 1) + t
            xc = xc + cw_ref[t:t + 1, :] * cs_scr[off:off + sl, :]
        xc_scr[s * sl:(s + 1) * sl, :] = xc

    lam = lam_ref[...]
    sp = jnp.maximum(-lam, 0.0) + jnp.log1p(jnp.exp(-jnp.abs(lam)))
    for h in range(LRU_HEADS):
        hs = slice(h * LRU_BLOCK, (h + 1) * LRU_BLOCK)
        xc = xc_scr[:, hs]
        xb = xc.astype(BF16)
        r = _sigmoid(_dot(xb, wa_ref[h]) + ba_ref[:, hs])
        i = _sigmoid(_dot(xb, wx_ref[h]) + bx_ref[:, hs])
        la = -LRU_C * r * sp[:, hs]
        a_scr[:, hs] = jnp.exp(la)
        b_scr[:, hs] = jnp.sqrt(-jnp.tanh(la) * (jnp.exp(2.0 * la) + 1.0)) * (i * xc)

    grp = min(nseq, 4)
    for g0 in range(0, nseq, grp):
        def body(t, hs, g0=g0):
            out = []
            for q in range(grp):
                idx = pl.ds((g0 + q) * sl + t, 1)
                h = a_scr[idx, :] * hs[q] + b_scr[idx, :]
                b_scr[idx, :] = h
                out.append(h)
            return tuple(out)

        fin = lax.fori_loop(0, sl, body, tuple(h_c[g0 + q:g0 + q + 1, :] for q in range(grp)), unroll=8)
        for q in range(grp):
            h_c[g0 + q:g0 + q + 1, :] = fin[q]

    y = b_scr[...] * _gelu(lg_ref[...].reshape(tm, BRANCH_WIDTH))
    y_ref[...] = y.reshape(nseq, sl, BRANCH_WIDTH)

    @pl.when(j == pl.num_programs(0) - 1)
    def _():
        hfin_ref[...] = h_c[...]


def _lru(z3, cst, h0, p, sl):
    nseq, length, _ = z3.shape
    tm = nseq * sl
    w = BRANCH_WIDTH
    const2 = lambda j: (0, 0)
    const3 = lambda j: (0, 0, 0)
    st_spec = pl.BlockSpec((nseq, w), const2)
    vec = pl.BlockSpec((1, w), const2)
    wsp = pl.BlockSpec((LRU_HEADS, LRU_BLOCK, LRU_BLOCK), const3)
    rb = sl // _HALO
    kern = functools.partial(_lru_kernel, nseq=nseq, sl=sl)
    return pl.pallas_call(
        kern,
        grid=(length // sl,),
        in_specs=[pl.BlockSpec((nseq, sl, w), lambda j: (0, j, COL_LX // w)),
                  pl.BlockSpec((nseq, sl, w), lambda j: (0, j, COL_LG // w)),
                  pl.BlockSpec((nseq, _HALO, w), lambda j: (0, jnp.maximum(j * rb - 1, 0), COL_LX // w)),
                  pl.BlockSpec((nseq, _HALO, w), const3),
                  st_spec,
                  pl.BlockSpec((LRU_CONV, w), const2), vec, wsp, vec, wsp, vec, vec],
        out_specs=[pl.BlockSpec((nseq, sl, w), lambda j: (0, j, 0)), st_spec],
        out_shape=[jax.ShapeDtypeStruct((nseq, length, w), F32), jax.ShapeDtypeStruct(h0.shape, F32)],
        scratch_shapes=[pltpu.VMEM((_HALO + sl, w), F32), pltpu.VMEM((tm, w), F32),
                        pltpu.VMEM((tm, w), F32), pltpu.VMEM((tm, w), F32),
                        pltpu.VMEM((nseq, w), F32)],
        compiler_params=_cparams(("arbitrary",)),
        name="lru",
    )(z3, z3, z3, cst, h0, p["cw"], p["cb"], p["wa"], p["ba"], p["wx"], p["bx"], p["lam"])


def _da_lambda(lq1_ref, lk1_ref, lq2_ref, lk2_ref, lam_init):
    return (jnp.exp(jnp.sum(lq1_ref[...] * lk1_ref[...])) - jnp.exp(jnp.sum(lq2_ref[...] * lk2_ref[...]))
            + lam_init)


def _attn_kernel(q_ref, k_ref, v_ref, lq1_ref, lk1_ref, lq2_ref, lk2_ref, dn_ref, o_ref,
                 m_scr, l_scr, acc_scr, *, tq, tk, lam_init):
    qi, ki = pl.program_id(2), pl.program_id(3)

    @pl.when(ki == 0)
    def _():
        m_scr[...] = jnp.full(m_scr.shape, -jnp.inf, F32)
        l_scr[...] = jnp.zeros(l_scr.shape, F32)
        acc_scr[...] = jnp.zeros(acc_scr.shape, F32)

    def step(masked):
        q = q_ref[...] * (DA_DK ** -0.5)
        kb = k_ref[...].astype(BF16)
        vb = v_ref[...].astype(BF16)
        lane = lax.broadcasted_iota(jnp.int32, (tq, LANES), 1)
        if masked:
            row = qi * tq + lax.broadcasted_iota(jnp.int32, (tq, tk), 0)
            col = ki * tk + lax.broadcasted_iota(jnp.int32, (tq, tk), 1)
            mask = col <= row
        comps = range(2)
        ss = [lax.dot_general(jnp.where((lane // DA_DK) == c, q, 0.0).astype(BF16), kb, _NT,
                              preferred_element_type=F32) for c in comps]
        if masked:
            ss = [jnp.where(mask, s, -jnp.inf) for s in ss]
        m_prev = [m_scr[c] for c in comps]
        l_prev = [l_scr[c] for c in comps]
        a_prev = [acc_scr[c] for c in comps]
        m_new = [jnp.maximum(m_prev[c], jnp.max(ss[c], axis=-1, keepdims=True)) for c in comps]
        alpha = [jnp.exp(m_prev[c] - m_new[c]) for c in comps]
        pexp = [jnp.exp(ss[c] - m_new[c]) for c in comps]
        pv = [_dot(pexp[c].astype(BF16), vb) for c in comps]
        for c in comps:
            l_scr[c] = alpha[c] * l_prev[c] + jnp.sum(pexp[c], axis=-1, keepdims=True)
            acc_scr[c] = alpha[c] * a_prev[c] + pv[c]
            m_scr[c] = m_new[c]

    last_col, first_col = ki * tk + tk - 1, ki * tk
    first_row, last_row = qi * tq, qi * tq + tq - 1

    @pl.when(last_col <= first_row)
    def _():
        step(False)

    @pl.when((last_col > first_row) & (first_col <= last_row))
    def _():
        step(True)

    @pl.when(ki == pl.num_programs(3) - 1)
    def _():
        lam = _da_lambda(lq1_ref, lk1_ref, lq2_ref, lk2_ref, lam_init)
        o = acc_scr[0] / l_scr[0] - lam * (acc_scr[1] / l_scr[1])
        ms = jnp.mean(o * o, axis=-1, keepdims=True)
        o_ref[...] = o * lax.rsqrt(ms + EPS) * dn_ref[...] * (1.0 - lam_init)


def _attn_prompt(q_rot, k_rot, z, lam_p, dn, nb, length, tq, tk, lam_init):
    m = q_rot.shape[0]
    nq, nk = length // tq, length // tk
    const2 = lambda b, h, qi, ki: (0, 0)
    kv_row = lambda b, qi, ki: b * nk + jnp.minimum(ki, (qi * tq + tq - 1) // tk)
    vec = pl.BlockSpec((1, DA_DK), const2)
    kern = functools.partial(_attn_kernel, tq=tq, tk=tk, lam_init=lam_init)
    return pl.pallas_call(
        kern,
        grid=(nb, DA_HEADS, nq, nk),
        in_specs=[pl.BlockSpec((tq, LANES), lambda b, h, qi, ki: (b * nq + qi, h)),
                  pl.BlockSpec((tk, LANES), lambda b, h, qi, ki: (kv_row(b, qi, ki), h)),
                  pl.BlockSpec((tk, DA_DV), lambda b, h, qi, ki: (kv_row(b, qi, ki), COL_DV // DA_DV + h)),
                  vec, vec, vec, vec, pl.BlockSpec((1, DA_DV), const2)],
        out_specs=pl.BlockSpec((tq, DA_DV), lambda b, h, qi, ki: (b * nq + qi, h)),
        out_shape=jax.ShapeDtypeStruct((m, DA_HEADS * DA_DV), F32),
        scratch_shapes=[pltpu.VMEM((2, tq, 1), F32), pltpu.VMEM((2, tq, 1), F32),
                        pltpu.VMEM((2, tq, DA_DV), F32)],
        compiler_params=_cparams(("arbitrary", "arbitrary", "arbitrary", "arbitrary")),
        name="attn_prompt",
    )(q_rot, k_rot, z, *lam_p, dn)


_PAGES_PER_STEP = 16
_QROWS = DA_HEADS * 2 * SUBLANES


def _attn_sample_kernel(pt_ref, q_ref, *refs, lq, lam_init):
    npg = _PAGES_PER_STEP
    kt_refs, v_refs = refs[0:npg], refs[npg:2 * npg]
    kn_ref, vn_ref, lq1_ref, lk1_ref, lq2_ref, lk2_ref, dn_ref = refs[2 * npg:2 * npg + 7]
    o_ref = refs[2 * npg + 7]
    qbd_scr, m_scr, l_scr, acc_scr = refs[2 * npg + 8:]
    w = DA_HEADS * 2 * DA_DK
    hrows = 2 * lq
    j = pl.program_id(1)

    @pl.when(j == 0)
    def _():
        q = q_ref[...] * (DA_DK ** -0.5)
        lane = lax.broadcasted_iota(jnp.int32, (lq, w), 1)
        tiles = [jnp.where((lane // DA_DK) == hc, q, 0.0) for hc in range(DA_HEADS * 2)]
        qbd_scr[...] = jnp.concatenate(tiles, axis=0)
        m_scr[...] = jnp.full(m_scr.shape, -jnp.inf, F32)
        l_scr[...] = jnp.zeros(l_scr.shape, F32)
        acc_scr[...] = jnp.zeros(acc_scr.shape, F32)

    qbd = qbd_scr[...].astype(BF16)

    def update(s, nchunk, vget):
        m_prev = m_scr[...]
        m_new = jnp.maximum(m_prev, jnp.max(s, axis=-1, keepdims=True))
        alpha = jnp.exp(m_prev - m_new)
        pexp = jnp.exp(s - m_new)
        l_scr[...] = alpha * l_scr[...] + jnp.sum(pexp, axis=-1, keepdims=True)
        pb = pexp.astype(BF16)
        for h in range(DA_HEADS):
            rs = slice(h * hrows, (h + 1) * hrows)
            pv = _dot(pb[rs, 0:PAGE_SIZE], vget(0, h))
            for i in range(1, nchunk):
                pv = pv + _dot(pb[rs, i * PAGE_SIZE:(i + 1) * PAGE_SIZE], vget(i, h))
            acc_scr[rs, :] = alpha[rs] * acc_scr[rs, :] + pv
        m_scr[...] = m_new

    s_pages = jnp.concatenate([_dot(qbd, kt_refs[i][...].astype(BF16)) for i in range(npg)], axis=1)
    update(s_pages, npg, lambda i, h: v_refs[i][pl.ds(h, PAGE_SIZE, stride=DA_HEADS), :].astype(BF16))

    @pl.when(j == pl.num_programs(1) - 1)
    def _():
        nk = kn_ref.shape[0]
        t = lax.broadcasted_iota(jnp.int32, (_QROWS, nk), 0) % lq
        col = lax.broadcasted_iota(jnp.int32, (_QROWS, nk), 1)
        s_new = lax.dot_general(qbd, kn_ref[...].astype(BF16), _NT, preferred_element_type=F32)
        update(jnp.where(col <= t, s_new, -jnp.inf), 1,
               lambda i, h: vn_ref[:, h * DA_DV:(h + 1) * DA_DV].astype(BF16))
        lam = _da_lambda(lq1_ref, lk1_ref, lq2_ref, lk2_ref, lam_init)
        for h in range(DA_HEADS):
            r0 = h * hrows
            o0 = acc_scr[r0:r0 + lq, :] / l_scr[r0:r0 + lq]
            o1 = acc_scr[r0 + lq:r0 + 2 * lq, :] / l_scr[r0 + lq:r0 + 2 * lq]
            o = o0 - lam * o1
            ms = jnp.mean(o * o, axis=-1, keepdims=True)
            o_ref[:, h * DA_DV:(h + 1) * DA_DV] = o * lax.rsqrt(ms + EPS) * dn_ref[...] * (1.0 - lam_init)


def _attn_sample(q_rot, cache_k, cache_v, layer, page_table, k_new, v_new, lam_p, dn, lam_init):
    nb, n_pages = page_table.shape
    lq = q_rot.shape[0] // nb
    w = DA_HEADS * 2 * DA_DK
    wv = DA_HEADS * DA_DV
    npg = _PAGES_PER_STEP
    nsteps = n_pages // npg
    ckt = jnp.transpose(cache_k, (0, 1, 3, 4, 5, 2)).reshape(DEPTH, -1, w, PAGE_SIZE)
    cv2 = cache_v.reshape(DEPTH, -1, PAGE_SIZE * DA_HEADS, DA_DV)
    const2 = lambda b, j, pt: (0, 0)
    page = lambda b, j, pt, i: pt[b * n_pages + j * npg + i]

    def k_spec(i):
        return pl.BlockSpec((None, None, w, PAGE_SIZE), lambda b, j, pt: (layer, page(b, j, pt, i), 0, 0))

    def v_spec(i):
        return pl.BlockSpec((None, None, PAGE_SIZE * DA_HEADS, DA_DV),
                            lambda b, j, pt: (layer, page(b, j, pt, i), 0, 0))

    vec = pl.BlockSpec((1, DA_DK), const2)
    nk = k_new.shape[1]
    grid_spec = pltpu.PrefetchScalarGridSpec(
        num_scalar_prefetch=1,
        grid=(nb, nsteps),
        in_specs=([pl.BlockSpec((lq, w), lambda b, j, pt: (b, 0))]
                  + [k_spec(i) for i in range(npg)] + [v_spec(i) for i in range(npg)]
                  + [pl.BlockSpec((None, nk, w), lambda b, j, pt: (b, 0, 0)),
                     pl.BlockSpec((None, nk, wv), lambda b, j, pt: (b, 0, 0)),
                     vec, vec, vec, vec, pl.BlockSpec((1, DA_DV), const2)]),
        out_specs=pl.BlockSpec((lq, wv), lambda b, j, pt: (b, 0)),
        scratch_shapes=[pltpu.VMEM((_QROWS, w), F32), pltpu.VMEM((_QROWS, 1), F32),
                        pltpu.VMEM((_QROWS, 1), F32), pltpu.VMEM((_QROWS, DA_DV), F32)],
    )
    kern = functools.partial(_attn_sample_kernel, lq=lq, lam_init=lam_init)
    return pl.pallas_call(
        kern,
        grid_spec=grid_spec,
        out_shape=jax.ShapeDtypeStruct((nb * lq, wv), F32),
        compiler_params=_cparams(("arbitrary", "arbitrary")),
        name="attn_sample",
    )(page_table.reshape(-1), q_rot, *([ckt] * npg), *([cv2] * npg), k_new, v_new, *lam_p, dn)


def _merge_kernel(o0_ref, o1_ref, o2_ref, o3_ref, g_ref, wb_ref, wo_ref, x_ref, g1_ref, lng_ref, lnb_ref,
                  out_ref, acc_scr):
    b = pl.program_id(1)

    @pl.when(b == 0)
    def _():
        acc_scr[...] = jnp.zeros(acc_scr.shape, F32)

    o = jnp.where(b == 0, o0_ref[...], jnp.where(b == 1, o1_ref[...], jnp.where(b == 2, o2_ref[...], o3_ref[...])))
    acc_scr[...] += _sigmoid(g_ref[...].astype(F32)) * _dot(o.astype(BF16), wb_ref[...])

    @pl.when(b == N_BRANCH - 1)
    def _():
        mix = _dot(acc_scr[...].astype(BF16), wo_ref[...])
        y = DEEPNORM_ALPHA * x_ref[...] + g1_ref[...] * mix
        out_ref[...] = _layer_norm(y, lng_ref[...], lnb_ref[...])


def _merge(branches, gates, w_branch, w_out, layer, x2d, g1, gdiv, ln_g, ln_b, tm):
    m = x2d.shape[0]
    d = D_MODEL
    r = g1.shape[1]
    const2 = lambda i, b: (0, 0)
    osp = pl.BlockSpec((tm, BRANCH_WIDTH), lambda i, b: (i, 0))
    return pl.pallas_call(
        _merge_kernel,
        grid=(m // tm, N_BRANCH),
        in_specs=[osp, osp, osp, osp,
                  pl.BlockSpec((tm, d), lambda i, b: (i, b)),
                  pl.BlockSpec((None, None, BRANCH_WIDTH, d), lambda i, b: (layer, b, 0, 0)),
                  pl.BlockSpec((None, d, d), lambda i, b: (layer, 0, 0), pipeline_mode=pl.Buffered(1)),
                  pl.BlockSpec((tm, d), lambda i, b: (i, 0)),
                  pl.BlockSpec((None, r, d), lambda i, b: (i // gdiv, 0, 0)),
                  pl.BlockSpec((1, d), const2), pl.BlockSpec((1, d), const2)],
        out_specs=pl.BlockSpec((tm, d), lambda i, b: (i, 0)),
        out_shape=jax.ShapeDtypeStruct((m, d), F32),
        scratch_shapes=[pltpu.VMEM((tm, d), F32)],
        compiler_params=_cparams(("arbitrary", "arbitrary")),
        name="merge",
    )(*branches, gates, w_branch, w_out, x2d, g1, ln_g, ln_b)


_FHALO = 2 * SUBLANES


def _ffn_kernel(x_ref, xprev_ref, sc_ref, sh_ref, g2_ref, csta_ref, cstv_ref, wa_ref, wv_ref, cw_ref, cb_ref,
                wd_ref, lng_ref, lnb_ref, out_ref, sta_ref, stv_ref, u_scr, hsa, hsv, act_scr,
                *, nseq, sl, first_div, rc):
    tm = nseq * sl
    halo = _FHALO if nseq == 1 else 0
    seg = _FHALO + sl
    i, j = pl.program_id(0), pl.program_id(1)
    nf = pl.num_programs(1)

    @pl.when(j == 0)
    def _():
        sc, sh = 1.0 + sc_ref[...], sh_ref[...]
        if halo:
            u_scr[0:halo, :] = (xprev_ref[...] * sc + sh).astype(BF16)
        u_scr[halo:halo + tm, :] = (x_ref[...] * sc + sh).astype(BF16)
        out_ref[...] = jnp.zeros(out_ref.shape, F32)

    parts = ((wa_ref, csta_ref, hsa, sta_ref), (wv_ref, cstv_ref, hsv, stv_ref))
    if nseq == 1:
        half = tm // 2
        spans = ((0, half), (half, tm))
        first = i % first_div == 0
        for lo, hi in ((0, _FHALO + half), (_FHALO + half, _FHALO + tm)):
            for w_ref, cst_ref, hs, _ in parts:
                hs[lo:hi, :] = _dot(u_scr[lo:hi, :], w_ref[...])
                if lo == 0:
                    hs[_FHALO - _HALO:_FHALO, :] = jnp.where(first, cst_ref[0], hs[_FHALO - _HALO:_FHALO, :])
    else:
        spans = ((0, tm),)
        u = u_scr[...]
        for w_ref, cst_ref, hs, _ in parts:
            hup = _dot(u, w_ref[...])
            for s in range(nseq):
                hs[s * seg + _FHALO - _HALO:s * seg + _FHALO, :] = cst_ref[s]
                hs[s * seg + _FHALO:(s + 1) * seg, :] = hup[s * sl:(s + 1) * sl]
    for _, _, hs, st_ref in parts:
        for s in range(nseq):
            st_ref[s] = hs[(s + 1) * seg - _HALO:(s + 1) * seg, :]

    k = FFN_CONV
    base = _FHALO - (k - 1)
    for lo, hi in spans:
        for r0 in range(lo, hi, rc):
            s, t0 = divmod(r0, sl)
            o = s * seg + base + t0
            ya, yv = cb_ref[0], cb_ref[1]
            for t in range(k):
                ya = ya + cw_ref[0, t:t + 1, :] * hsa[o + t:o + t + rc, :]
                yv = yv + cw_ref[1, t:t + 1, :] * hsv[o + t:o + t + rc, :]
            act_scr[r0:r0 + rc, :] = (_gelu(ya) * yv).astype(act_scr.dtype)
        out_ref[lo:hi, :] += _dot(act_scr[lo:hi, :].astype(BF16), wd_ref[...])

    @pl.when(j == nf - 1)
    def _():
        y = DEEPNORM_ALPHA * x_ref[...] + g2_ref[...] * out_ref[...]
        out_ref[...] = _layer_norm(y, lng_ref[...], lnb_ref[...])


_FFN_TF = 512


def _ffn(x2d, sc, sh, g2, gdiv, cst, w_up, conv_w, conv_b, w_down, layer, ln_g, ln_b, nseq, sl, first_div, tf):
    m, d = x2d.shape
    tm = nseq * sl
    nblk = m // tm
    nf = D_FF // tf
    r = sc.shape[1]
    halo = _FHALO if nseq == 1 else 0
    const2 = lambda i, j: (0, 0)
    mod_spec = pl.BlockSpec((None, r, d), lambda i, j: (i // gdiv, 0, 0))
    rb = tm // _FHALO
    cw2 = conv_w.reshape(FFN_CONV, 2, D_FF).transpose(1, 0, 2)
    cb2 = conv_b.reshape(2, 1, D_FF)
    csta, cstv = cst[..., :D_FF], cst[..., D_FF:]
    cst_spec = pl.BlockSpec((None, nseq, _HALO, tf), lambda i, j: (i // first_div, 0, 0, j))
    st_spec = pl.BlockSpec((None, nseq, _HALO, tf), lambda i, j: (i, 0, 0, j))
    st_shape = jax.ShapeDtypeStruct((nblk, nseq, _HALO, D_FF), F32)
    rc = 32 if sl % 32 == 0 else sl
    act_dtype = BF16 if rc % (2 * SUBLANES) == 0 else F32
    kern = functools.partial(_ffn_kernel, nseq=nseq, sl=sl, first_div=first_div, rc=rc)
    return pl.pallas_call(
        kern,
        grid=(nblk, nf),
        in_specs=[pl.BlockSpec((tm, d), lambda i, j: (i, 0), pipeline_mode=pl.Buffered(1)),
                  pl.BlockSpec((_FHALO, d), lambda i, j: (jnp.maximum(i * rb - 1, 0), 0)),
                  mod_spec, mod_spec, mod_spec, cst_spec, cst_spec,
                  pl.BlockSpec((None, d, tf), lambda i, j: (layer, 0, j)),
                  pl.BlockSpec((None, d, tf), lambda i, j: (layer, 0, nf + j)),
                  pl.BlockSpec((2, FFN_CONV, tf), lambda i, j: (0, 0, j)),
                  pl.BlockSpec((2, 1, tf), lambda i, j: (0, 0, j)),
                  pl.BlockSpec((None, tf, d), lambda i, j: (layer, j, 0)),
                  pl.BlockSpec((1, d), const2), pl.BlockSpec((1, d), const2)],
        out_specs=[pl.BlockSpec((tm, d), lambda i, j: (i, 0), pipeline_mode=pl.Buffered(1)), st_spec, st_spec],
        out_shape=[jax.ShapeDtypeStruct((m, d), F32), st_shape, st_shape],
        scratch_shapes=[pltpu.VMEM((halo + tm, d), BF16),
                        pltpu.VMEM((nseq * (_FHALO + sl), tf), F32), pltpu.VMEM((nseq * (_FHALO + sl), tf), F32),
                        pltpu.VMEM((tm, tf), act_dtype)],
        compiler_params=_cparams(("arbitrary", "arbitrary")),
        name="ffn",
    )(x2d, x2d, sc, sh, g2, csta, cstv, w_up, w_up, cw2, cb2, w_down, ln_g, ln_b)


def _pad_state_rows(buf, rows):
    pad = [(0, 0)] * buf.ndim
    pad[-2] = (rows - buf.shape[-2], 0)
    return jnp.pad(buf, pad)


def _run_layer(x2d, mod, cfg, wts, l, states, paged):
    nb, length = cfg["nb"], cfg["len"]
    m = nb * length
    sh1, sc1, g1, sh2, sc2, g2 = mod
    gdiv = cfg["gdiv"]
    st_gla, st_s5re, st_s5im, st_lru, st_lconv, st_fconv = states

    gates, z, zlr = _inproj(x2d, sc1, sh1, gdiv(cfg["tm_in"]), wts["w_in_t"], l,
                            cfg["tm_in"], cfg["tng"], cfg["tnm"])

    ng_seq, nseq = cfg["ng"], cfg["nseq"]
    o_gla, gla_new = _gla(z, zlr, _gla_state_pad(st_gla), wts["wa2p"], wts["gla_ba"], wts["gla_norm"],
                          ng_seq, m // ng_seq, cfg["gla_c"], cfg["gla_seq"])
    gla_new = _gla_state_unpad(gla_new)

    sl = cfg["sl"]
    o_s5, s5re_new, s5im_new = _s5(z.reshape(nb, length, Z_MIX), st_s5re.reshape(nb, _S5_CH, LANES),
                                   st_s5im.reshape(nb, _S5_CH, LANES), wts["s5"], cfg["s5_sl"])
    o_s5 = o_s5.reshape(m, BRANCH_WIDTH)
    s5re_new = s5re_new.reshape(nb, S5_GROUPS, S5_N)
    s5im_new = s5im_new.reshape(nb, S5_GROUPS, S5_N)

    o_lru, lru_new = _lru(z.reshape(nb, length, Z_MIX), _pad_state_rows(st_lconv, _HALO), st_lru, wts["lru"],
                          cfg["s5_sl"])
    o_lru = o_lru.reshape(m, BRANCH_WIDTH)
    lconv_new = z[:, COL_LX:COL_LX + BRANCH_WIDTH].reshape(nb, length, BRANCH_WIDTH)[:, length - (LRU_CONV - 1):]

    lam_init = 0.8 - 0.6 * math.exp(-0.3 * l)
    q_rot, k_rot = _rope(z, cfg["rope"], nb, length, cfg["tm_rope"])
    v_new = z[:, COL_DV:COL_DV + DA_HEADS * DA_DV]
    if paged is None:
        o_da = _attn_prompt(q_rot, k_rot, z, wts["lam_p"], wts["da_norm"], nb, length,
                            cfg["tq"], cfg["tk"], lam_init)
    else:
        page_table, cache_k, cache_v = paged
        padk = ((0, 0), (0, PAGE_SIZE - length), (0, 0))
        k_pad = jnp.pad(k_rot.reshape(nb, length, -1), padk)
        v_pad = jnp.pad(v_new.reshape(nb, length, -1), padk)
        o_da = _attn_sample(q_rot, cache_k, cache_v, l, page_table, k_pad, v_pad, wts["lam_p"],
                            wts["da_norm"], lam_init)

    x1 = _merge((o_gla, o_s5, o_lru, o_da), gates, wts["w_branch"], wts["w_out"], l, x2d, g1, gdiv(cfg["tm_mg"]),
                wts["ln1_g"], wts["ln1_b"], cfg["tm_mg"])

    fcst = _pad_state_rows(st_fconv, _HALO).reshape(ng_seq, nseq, _HALO, 2 * D_FF)
    fsl = cfg["ffn_sl"]
    x2, sta, stv = _ffn(x1, sc2, sh2, g2, gdiv(nseq * fsl), fcst, wts["w_up"], wts["ffn_conv_w"],
                        wts["ffn_conv_b"], wts["w_down"], l, wts["ln2_g"], wts["ln2_b"], nseq, fsl,
                        length // fsl if nseq == 1 else 1, _FFN_TF)
    fst = jnp.concatenate([sta, stv], axis=-1)
    if nseq == 1:
        per_seq = length // fsl
        fst = fst.reshape(nb, per_seq, _HALO, 2 * D_FF)[:, per_seq - 1]
    else:
        fst = fst.reshape(nb, _HALO, 2 * D_FF)
    fconv_new = fst[:, _HALO - (FFN_CONV - 1):]

    new = (k_rot.reshape(nb, length, DA_HEADS, 2, DA_DK), v_new.reshape(nb, length, DA_HEADS, DA_DV),
           gla_new, s5re_new, s5im_new, lru_new, lconv_new, fconv_new)
    return x2, new


def _layer_weights(l, w_in, gla_wa2, gla_ba, gla_norm, s5_raw, lru_raw, lam_raw, da_norm, w_branch, w_out,
                   ln1_g, ln1_b, ffn_w_up, ffn_conv_w, ffn_conv_b, ffn_w_down, ln2_g, ln2_b):
    w_in_t = jnp.swapaxes(w_in, 1, 2).astype(BF16)
    wa2p = jnp.pad(gla_wa2[l], ((0, LANES - GLA_LOWRANK), (0, 0))).astype(BF16)
    row = lambda a: a[l].reshape(1, -1)
    cw, cb, wa, ba, wx, bx, lam = [a[l] for a in lru_raw]
    return dict(
        w_in_t=w_in_t, wa2p=wa2p, gla_ba=row(gla_ba), gla_norm=row(gla_norm),
        s5=_s5_params(*[a[l] for a in s5_raw]),
        lru=dict(cw=cw, cb=cb.reshape(1, -1), wa=wa.astype(BF16), ba=ba.reshape(1, -1),
                 wx=wx.astype(BF16), bx=bx.reshape(1, -1), lam=lam.reshape(1, -1)),
        lam_p=tuple(row(a) for a in lam_raw), da_norm=row(da_norm),
        w_branch=w_branch.astype(BF16), w_out=w_out.astype(BF16),
        ln1_g=row(ln1_g), ln1_b=row(ln1_b),
        w_up=ffn_w_up.astype(BF16), ffn_conv_w=ffn_conv_w[l], ffn_conv_b=ffn_conv_b[l],
        w_down=ffn_w_down.astype(BF16), ln2_g=row(ln2_g), ln2_b=row(ln2_b))


def kernel(x_prompt, x_sample, cache_k, cache_v, state_gla, state_s5_re, state_s5_im, state_lru, state_lru_conv, state_ffn_conv, page_table, c_prompt, c_sample, w_ada, b_ada, w_in, gla_wa2, gla_ba, gla_norm, s5_a_re, s5_a_im, s5_log_dt, s5_b_re, s5_b_im, s5_c_re, s5_c_im, s5_d, s5_w_glu, s5_b_glu, lru_conv_w, lru_conv_b, lru_w_a, lru_b_a, lru_w_x, lru_b_x, lru_lambda, da_lq1, da_lk1, da_lq2, da_lk2, da_norm, w_branch, w_out, ln1_g, ln1_b, ffn_w_up, ffn_conv_w, ffn_conv_b, ffn_w_down, ln2_g, ln2_b):
    bp, lp, d = x_prompt.shape
    bs, ls, _ = x_sample.shape
    past_len = page_table.shape[1] * PAGE_SIZE

    rows = bp + bs
    rows_pad = -(-rows // SUBLANES) * SUBLANES
    c_all = jnp.pad(jnp.concatenate([c_prompt, c_sample], axis=0), ((0, rows_pad - rows), (0, 0)))
    mod_all = _ada(c_all, w_ada, b_ada)

    def mods(l, lo, n, per_row):
        parts = jnp.split(mod_all[l, lo:lo + n], 6, axis=-1)
        if per_row:
            return [jnp.repeat(p, per_row, axis=0)[None] for p in parts]
        return [p[:, None, :] for p in parts]

    cfg_p = dict(nb=bp, len=lp, ng=bp, nseq=1, sl=256, s5_sl=128, ffn_sl=1024, tm_in=1024, tng=2048, tnm=512, gla_c=128,
                 gla_seq=128, tm_rope=512, tq=1024, tk=512, tm_mg=512, tf=512,
                 gdiv=lambda tm: lp // tm, rope=_rope_tables(0, lp))
    cfg_s = dict(nb=bs, len=ls, ng=1, nseq=bs, sl=ls, s5_sl=ls, ffn_sl=ls, tm_in=bs * ls, tng=1024, tnm=512, gla_c=bs * ls,
                 gla_seq=ls, tm_rope=ls, tm_mg=bs * ls, tf=512,
                 gdiv=lambda tm: 1, rope=_rope_tables(past_len, ls))

    zeros_p = (jnp.zeros((bp, GLA_HEADS, GLA_DK, GLA_DV), F32), jnp.zeros((bp, S5_GROUPS, S5_N), F32),
               jnp.zeros((bp, S5_GROUPS, S5_N), F32), jnp.zeros((bp, BRANCH_WIDTH), F32),
               jnp.zeros((bp, LRU_CONV - 1, BRANCH_WIDTH), F32), jnp.zeros((bp, FFN_CONV - 1, 2 * D_FF), F32))

    xp = x_prompt.reshape(bp * lp, d)
    xs = x_sample.reshape(bs * ls, d)
    col_p = [[] for _ in range(8)]
    col_s = [[] for _ in range(8)]
    s5_raw = (s5_a_re, s5_a_im, s5_log_dt, s5_b_re, s5_b_im, s5_c_re, s5_c_im, s5_d, s5_w_glu, s5_b_glu)
    lru_raw = (lru_conv_w, lru_conv_b, lru_w_a, lru_b_a, lru_w_x, lru_b_x, lru_lambda)
    lam_raw = (da_lq1, da_lk1, da_lq2, da_lk2)
    for l in range(DEPTH):
        wts = _layer_weights(l, w_in, gla_wa2, gla_ba, gla_norm, s5_raw, lru_raw, lam_raw, da_norm, w_branch,
                             w_out, ln1_g, ln1_b, ffn_w_up, ffn_conv_w, ffn_conv_b, ffn_w_down, ln2_g, ln2_b)
        xp, new_p = _run_layer(xp, mods(l, 0, bp, 0), cfg_p, wts, l, zeros_p, None)
        st_s = (state_gla[l], state_s5_re[l], state_s5_im[l], state_lru[l], state_lru_conv[l], state_ffn_conv[l])
        xs, new_s = _run_layer(xs, mods(l, bp, bs, ls), cfg_s, wts, l, st_s, (page_table, cache_k, cache_v))
        for lst, s in zip(col_p, new_p):
            lst.append(s)
        for lst, s in zip(col_s, new_s):
            lst.append(s)

    sp = [jnp.stack(s) for s in col_p]
    ss = [jnp.stack(s) for s in col_s]
    out = [xp.reshape(bp, lp, d), xs.reshape(bs, ls, d)]
    for a, b in zip(sp, ss):
        out += [a, b]
    return tuple(out)
```
